```python
import math
import jax, jax.numpy as jnp
from jax import lax
import numpy as np

D_MODEL = 2048
BATCH = 1
SEQ = 8192
DEPTH = 1
DEC_BATCH = 32
DEC_SEQ = 8
PAST_LEN = 16384
PAGE_SIZE = 128

HEAD_DIM = 128
ATTN_HEADS = D_MODEL // 256
ATTN_WIDTH = ATTN_HEADS * HEAD_DIM
MOBA_BLOCK = 256
MOBA_TOPK = 3
Q_BLOCK = 64
N_BUCKETS = 32
MAX_DISTANCE = 128
RW_HEAD = 64
RW_HEADS = D_MODEL // 128
RW_WIDTH = RW_HEADS * RW_HEAD
DECAY_LORA = max(32, int(round(1.8 * D_MODEL ** 0.5 / 32)) * 32)
AAA_LORA = max(32, int(round(1.8 * D_MODEL ** 0.5 / 32)) * 32)
GATE_LORA = max(32, int(round(0.6 * D_MODEL ** 0.8 / 32)) * 32)
GN_EPS = RW_HEAD * 1e-5
D_FF = ((8 * D_MODEL + 3 * 256 - 1) // (3 * 256)) * 256
LN_EPS = 1e-5
DEEPNORM_ALPHA = (2 * DEPTH) ** 0.25
DEEPNORM_BETA = (8 * DEPTH) ** -0.25

RW_R = 0
RW_K = RW_WIDTH
RW_V = 2 * RW_WIDTH
RW_W = 3 * RW_WIDTH
RW_A = RW_W + DECAY_LORA
RW_G = RW_A + AAA_LORA
RW_COLS = RW_G + GATE_LORA
Q_COL = 0
K_COL = ATTN_WIDTH
V_COL = 2 * ATTN_WIDTH
RW_COL = 3 * ATTN_WIDTH
GATE_COL = RW_COL + RW_COLS
IN_COLS = GATE_COL + 2 * D_MODEL

kernel_name = 'moba_rwkv7_gated_hybrid_step'


def _layer_norm(x, g, b):
    xf = x.astype(jnp.float32)
    mu = jnp.mean(xf, axis=-1, keepdims=True)
    var = jnp.mean(jnp.square(xf - mu), axis=-1, keepdims=True)
    return ((xf - mu) * lax.rsqrt(var + LN_EPS) * g + b).astype(x.dtype)


def _t5_bucket(dist):
    dist = jnp.maximum(dist, 0)
    exact = N_BUCKETS // 2
    log_ratio = jnp.log(jnp.maximum(dist, 1).astype(jnp.float32) / exact) / math.log(MAX_DISTANCE / exact)
    large = jnp.minimum(exact + (log_ratio * (N_BUCKETS - exact)).astype(jnp.int32), N_BUCKETS - 1)
    return jnp.where(dist < exact, dist, large)


def _split_qkv(z):
    n, t = z.shape[0], z.shape[1]
    shp = (n, t, ATTN_HEADS, HEAD_DIM)
    return (z[..., Q_COL:K_COL].reshape(shp), z[..., K_COL:V_COL].reshape(shp),
            z[..., V_COL:RW_COL].reshape(shp))


def _moba_select(q, qpos, blk_mean, n_blocks):
    score = jnp.einsum('nqhd,njhd->nqhj', q.astype(jnp.float32), blk_mean)
    qblk = qpos // MOBA_BLOCK
    past = jnp.arange(blk_mean.shape[1])[None, :] < qblk[:, None]
    score = jnp.where(past[None, :, None, :], score, -jnp.inf)
    _, idx = lax.top_k(score, MOBA_TOPK)
    idx = jnp.minimum(idx, max(n_blocks - 1, 0))
    valid = jnp.arange(MOBA_TOPK)[None, :] < jnp.minimum(qblk, MOBA_TOPK)[:, None]
    return idx, valid


def _moba_attend(q, qpos, k_sel, v_sel, idx, valid, k_own, v_own, own_pos, rel_bias):
    n, nq, h, _ = q.shape
    qf = q.astype(jnp.float32) * (HEAD_DIM ** -0.5)
    head = jnp.arange(h)[:, None, None]
    sel_pos = idx[..., None] * MOBA_BLOCK + jnp.arange(MOBA_BLOCK)
    sel_bias = rel_bias[_t5_bucket(qpos[None, :, None, None, None] - sel_pos), head]
    s_sel = jnp.einsum('nqhd,nqhtkd->nqhtk', qf, k_sel.astype(jnp.float32)) + sel_bias
    s_sel = jnp.where(valid[None, :, None, :, None], s_sel, -jnp.inf)
    own_ok = ((own_pos[None, :] <= qpos[:, None])
              & (own_pos[None, :] // MOBA_BLOCK == qpos[:, None] // MOBA_BLOCK))
    own_bias = jnp.transpose(rel_bias[_t5_bucket(qpos[:, None] - own_pos[None, :])], (0, 2, 1))
    s_own = jnp.einsum('nqhd,nlhd->nqhl', qf, k_own.astype(jnp.float32)) + own_bias[None]
    s_own = jnp.where(own_ok[None, :, None, :], s_own, -jnp.inf)
    n_sel = MOBA_TOPK * MOBA_BLOCK
    p = jax.nn.softmax(jnp.concatenate([s_sel.reshape(n, nq, h, n_sel), s_own], axis=-1), axis=-1)
    o = (jnp.einsum('nqhtk,nqhtkd->nqhd', p[..., :n_sel].reshape(s_sel.shape), v_sel.astype(jnp.float32))
         + jnp.einsum('nqhl,nlhd->nqhd', p[..., n_sel:], v_own.astype(jnp.float32)))
    return o.astype(q.dtype)


def _moba_prompt(q, k, v, rel_bias):
    n, t, h, d = q.shape
    nb = -(-t // MOBA_BLOCK)
    pad = ((0, 0), (0, nb * MOBA_BLOCK - t), (0, 0), (0, 0))
    k_pad, v_pad = jnp.pad(k, pad), jnp.pad(v, pad)
    k_blk = k_pad.reshape(n, nb, MOBA_BLOCK, h, d)
    v_blk = v_pad.reshape(n, nb, MOBA_BLOCK, h, d)
    blk_mean = jnp.sum(k_blk, axis=2, dtype=jnp.float32) / MOBA_BLOCK
    if nb < MOBA_TOPK:
        blk_mean = jnp.pad(blk_mean, ((0, 0), (0, MOBA_TOPK - nb), (0, 0), (0, 0)))
    b_ix = jnp.arange(n)[:, None, None, None, None]
    h_ix = jnp.arange(h)[None, None, :, None, None]
    r_ix = jnp.arange(MOBA_BLOCK)

    def query_block(start):
        q_c = lax.dynamic_slice_in_dim(q, start, Q_BLOCK, axis=1)
        qpos = start + jnp.arange(Q_BLOCK)
        idx, valid = _moba_select(q_c, qpos, blk_mean, nb)
        sel = idx[..., None]
        k_sel = k_blk[b_ix, sel, r_ix, h_ix]
        v_sel = v_blk[b_ix, sel, r_ix, h_ix]
        own_start = (start // MOBA_BLOCK) * MOBA_BLOCK
        k_own = lax.dynamic_slice_in_dim(k_pad, own_start, MOBA_BLOCK, axis=1)
        v_own = lax.dynamic_slice_in_dim(v_pad, own_start, MOBA_BLOCK, axis=1)
        own_pos = own_start + jnp.arange(MOBA_BLOCK)
        return _moba_attend(q_c, qpos, k_sel, v_sel, idx, valid, k_own, v_own, own_pos, rel_bias)

    o = lax.map(query_block, jnp.arange(t // Q_BLOCK) * Q_BLOCK)
    return jnp.transpose(o, (1, 0, 2, 3, 4)).reshape(n, t, h * d)


def _moba_sample(q, k_new, v_new, cache_k, cache_v, layer, page_ksum, page_table, rel_bias):
    n, s, h, d = q.shape
    n_pages = page_table.shape[1]
    past_len = n_pages * PAGE_SIZE
    ppb = MOBA_BLOCK // PAGE_SIZE
    nbp = past_len // MOBA_BLOCK
    assert MOBA_BLOCK % PAGE_SIZE == 0 and past_len % MOBA_BLOCK + s <= MOBA_BLOCK
    full = page_table[:, :nbp * ppb]
    blk_mean = page_ksum[full].reshape(n, nbp, ppb, h, d).sum(axis=2) / MOBA_BLOCK
    if nbp < MOBA_TOPK:
        blk_mean = jnp.pad(blk_mean, ((0, 0), (0, MOBA_TOPK - nbp), (0, 0), (0, 0)))
    own_pages = page_table[:, nbp * ppb:]
    n_rem = own_pages.shape[1] * PAGE_SIZE
    k_own = jnp.concatenate([cache_k[layer, own_pages].reshape(n, n_rem, h, d), k_new], axis=1)
    v_own = jnp.concatenate([cache_v[layer, own_pages].reshape(n, n_rem, h, d), v_new], axis=1)
    own_pos = jnp.concatenate([nbp * MOBA_BLOCK + jnp.arange(n_rem), past_len + jnp.arange(s)])
    b_ix = jnp.arange(n)[:, None, None, None, None]
    h_ix = jnp.arange(h)[None, None, :, None, None]
    r_ix = jnp.arange(MOBA_BLOCK)

    def one_token(args):
        q_t, qpos = args
        idx, valid = _moba_select(q_t, qpos, blk_mean, nbp)
        logical = jnp.minimum(idx[..., None] * ppb + r_ix // PAGE_SIZE, n_pages - 1)
        phys = page_table[b_ix, logical]
        row = r_ix % PAGE_SIZE
        k_sel = cache_k[layer, phys, row, h_ix]
        v_sel = cache_v[layer, phys, row, h_ix]
        return _moba_attend(q_t, qpos, k_sel, v_sel, idx, valid, k_own, v_own, own_pos, rel_bias)

    q_seq = jnp.transpose(q, (1, 0, 2, 3))[:, :, None]
    pos = (past_len + jnp.arange(s))[:, None]
    o = lax.map(one_token, (q_seq, pos))
    return jnp.transpose(o[:, :, 0], (1, 0, 2, 3)).reshape(n, s, h * d)


def _rwkv7_branch(z, z_prev, s0, mu, w0, w2, a0, a2, g2, k_k, k_a, r_k, ln_g, ln_b):
    n, t, _ = z.shape
    zf = z.astype(jnp.float32)
    z_shift = jnp.concatenate([z_prev.astype(jnp.float32)[:, None], zf[:, :-1]], axis=1)
    zm = zf + (z_shift - zf) * mu
    r, k, v = zm[..., RW_R:RW_K], zm[..., RW_K:RW_V], zm[..., RW_V:RW_W]
    zw, za, zg = zm[..., RW_W:RW_A], zm[..., RW_A:RW_G], zm[..., RW_G:RW_COLS]
    w_log = -jax.nn.softplus(-(w0 + jnp.tanh(zw) @ w2)) - 0.5
    decay = jnp.exp(-jnp.exp(w_log))
    a = jax.nn.sigmoid(a0 + za @ a2)
    g = jax.nn.sigmoid(zg) @ g2

    def heads(u):
        return u.reshape(n, t, RW_HEADS, RW_HEAD)

    kk = heads(k * k_k)
    kk = kk / jnp.maximum(jnp.sqrt(jnp.sum(kk * kk, axis=-1, keepdims=True)), 1e-12)
    k = k * (1.0 + (a - 1.0) * k_a)
    rh, kh, vh, wh, ah = heads(r), heads(k), heads(v), heads(decay), heads(a)

    def step(state, inp):
        r_t, w_t, k_t, v_t, kk_t, a_t = inp
        sa = jnp.einsum('nhvk,nhk->nhv', state, -kk_t)
        state = (state * w_t[:, :, None, :] + sa[..., None] * (kk_t * a_t)[:, :, None, :]
                 + v_t[..., None] * k_t[:, :, None, :])
        return state, jnp.einsum('nhvk,nhk->nhv', state, r_t)

    xs = tuple(jnp.transpose(u, (1, 0, 2, 3)) for u in (rh, wh, kh, vh, kk, ah))
    s_fin, ys = lax.scan(step, s0.astype(jnp.float32), xs)
    y = jnp.transpose(ys, (1, 0, 2, 3))
    y_mu = jnp.mean(y, axis=-1, keepdims=True)
    y_var = jnp.mean(jnp.square(y - y_mu), axis=-1, keepdims=True)
    yn = ((y - y_mu) * lax.rsqrt(y_var + GN_EPS)).reshape(n, t, RW_WIDTH) * ln_g + ln_b
    bonus = (jnp.sum(rh * kh * r_k, axis=-1, keepdims=True) * vh).reshape(n, t, RW_WIDTH)
    return ((yn + bonus) * g).astype(z.dtype), s_fin


def _merge_ffn(x, o_attn, o_rwkv, z_gate, w_up_attn, w_up_rwkv, w_o, ln1_g, ln1_b,
               w_ffn_gate, w_ffn_up, w_ffn_down, ln2_g, ln2_b):
    g_attn = jax.nn.sigmoid(z_gate[..., :D_MODEL])
    g_rwkv = jax.nn.sigmoid(z_gate[..., D_MODEL:])
    mixed = (g_attn * (o_attn @ w_up_attn) + g_rwkv * (o_rwkv @ w_up_rwkv)) @ w_o
    h = _layer_norm(DEEPNORM_ALPHA * x + mixed, ln1_g, ln1_b)
    f = (jax.nn.silu(h @ w_ffn_gate) * (h @ w_ffn_up)) @ w_ffn_down
    return _layer_norm(DEEPNORM_ALPHA * h + f, ln2_g, ln2_b)


def setup_inputs(seed: int = 0) -> dict:
    key = jax.random.key(seed)
    ks = iter(jax.random.split(key, 40))

    def nrm(shape, scale):
        return scale * jax.random.normal(next(ks), shape, jnp.float32)

    n_pages = PAST_LEN // PAGE_SIZE
    used = DEC_BATCH * n_pages
    n_pool = used + (used + 3) // 4
    perm = jax.random.permutation(next(ks), n_pool).astype(jnp.int32)
    page_table = perm[:used].reshape(DEC_BATCH, n_pages)
    col_scale = (jnp.ones((IN_COLS,), jnp.float32)
                 .at[V_COL:RW_COL].set(DEEPNORM_BETA)
                 .at[RW_COL + RW_V:RW_COL + RW_W].set(DEEPNORM_BETA))
    return {
        'x_prompt': nrm((BATCH, SEQ, D_MODEL), 1.0),
        'x_sample': nrm((DEC_BATCH, DEC_SEQ, D_MODEL), 1.0),
        'cache_k': nrm((DEPTH, n_pool, PAGE_SIZE, ATTN_HEADS, HEAD_DIM), 1.0),
        'cache_v': nrm((DEPTH, n_pool, PAGE_SIZE, ATTN_HEADS, HEAD_DIM), DEEPNORM_BETA),
        'page_table': page_table,
        'state_wkv': nrm((DEPTH, DEC_BATCH, RW_HEADS, RW_HEAD, RW_HEAD), 0.3),
        'state_shift': nrm((DEPTH, DEC_BATCH, D_MODEL), 1.0),
        'w_in': nrm((DEPTH, D_MODEL, IN_COLS), D_MODEL ** -0.5) * col_scale,
        'rel_bias': nrm((N_BUCKETS, ATTN_HEADS), 0.3),
        'rw_mu': jax.random.uniform(next(ks), (DEPTH, RW_COLS), jnp.float32),
        'rw_w0': nrm((DEPTH, RW_WIDTH), 0.5),
        'rw_w2': nrm((DEPTH, DECAY_LORA, RW_WIDTH), 0.5 * DECAY_LORA ** -0.5),
        'rw_a0': nrm((DEPTH, RW_WIDTH), 0.5),
        'rw_a2': nrm((DEPTH, AAA_LORA, RW_WIDTH), 0.5 * AAA_LORA ** -0.5),
        'rw_g2': nrm((DEPTH, GATE_LORA, RW_WIDTH), GATE_LORA ** -0.5),
        'rw_k_k': 0.85 + nrm((DEPTH, RW_WIDTH), 0.1),
        'rw_k_a': 1.0 + nrm((DEPTH, RW_WIDTH), 0.1),
        'rw_r_k': nrm((DEPTH, RW_HEADS, RW_HEAD), 0.1),
        'rw_ln_g': 1.0 + nrm((DEPTH, RW_WIDTH), 0.1),
        'rw_ln_b': nrm((DEPTH, RW_WIDTH), 0.02),
        'w_up_attn': nrm((DEPTH, ATTN_WIDTH, D_MODEL), DEEPNORM_BETA * ATTN_WIDTH ** -0.5),
        'w_up_rwkv': nrm((DEPTH, RW_WIDTH, D_MODEL), DEEPNORM_BETA * RW_WIDTH ** -0.5),
        'w_o': nrm((DEPTH, D_MODEL, D_MODEL), DEEPNORM_BETA * D_MODEL ** -0.5),
        'ln1_g': 1.0 + nrm((DEPTH, D_MODEL), 0.1),
        'ln1_b': nrm((DEPTH, D_MODEL), 0.02),
        'w_ffn_gate': nrm((DEPTH, D_MODEL, D_FF), DEEPNORM_BETA * D_MODEL ** -0.5),
        'w_ffn_up': nrm((DEPTH, D_MODEL, D_FF), DEEPNORM_BETA * D_MODEL ** -0.5),
        'w_ffn_down': nrm((DEPTH, D_FF, D_MODEL), DEEPNORM_BETA * D_FF ** -0.5),
        'ln2_g': 1.0 + nrm((DEPTH, D_MODEL), 0.1),
        'ln2_b': nrm((DEPTH, D_MODEL), 0.02),
    }


def reference(x_prompt, x_sample, cache_k, cache_v, page_table, state_wkv, state_shift, w_in, rel_bias,
              rw_mu, rw_w0, rw_w2, rw_a0, rw_a2, rw_g2, rw_k_k, rw_k_a, rw_r_k, rw_ln_g, rw_ln_b,
              w_up_attn, w_up_rwkv, w_o, ln1_g, ln1_b, w_ffn_gate, w_ffn_up, w_ffn_down, ln2_g, ln2_b):
    page_ksum = jnp.sum(cache_k, axis=2, dtype=jnp.float32)
    hp, hs = x_prompt, x_sample
    nbat = x_prompt.shape[0]
    kp, vp, sp, shp, ksm, vsm, ssm, shs = [], [], [], [], [], [], [], []
    for l in range(DEPTH):
        rw = (rw_mu[l], rw_w0[l], rw_w2[l], rw_a0[l], rw_a2[l], rw_g2[l], rw_k_k[l], rw_k_a[l],
              rw_r_k[l], rw_ln_g[l], rw_ln_b[l])
        tail = (w_up_attn[l], w_up_rwkv[l], w_o[l], ln1_g[l], ln1_b[l], w_ffn_gate[l], w_ffn_up[l],
                w_ffn_down[l], ln2_g[l], ln2_b[l])
        zp = hp @ w_in[l]
        q, k, v = _split_qkv(zp)
        o_attn = _moba_prompt(q, k, v, rel_bias)
        o_rwkv, s_fin = _rwkv7_branch(zp[..., RW_COL:GATE_COL], jnp.zeros((nbat, RW_COLS), zp.dtype),
                                      jnp.zeros((nbat, RW_HEADS, RW_HEAD, RW_HEAD), jnp.float32), *rw)
        kp.append(k)
        vp.append(v)
        sp.append(s_fin.astype(x_prompt.dtype))
        shp.append(hp[:, -1])
        hp = _merge_ffn(hp, o_attn, o_rwkv, zp[..., GATE_COL:], *tail)
        zs = hs @ w_in[l]
        q, k, v = _split_qkv(zs)
        o_attn = _moba_sample(q, k, v, cache_k, cache_v, l, page_ksum[l], page_table, rel_bias)
        z_prev = state_shift[l] @ w_in[l][:, RW_COL:GATE_COL]
        o_rwkv, s_fin = _rwkv7_branch(zs[..., RW_COL:GATE_COL], z_prev, state_wkv[l], *rw)
        ksm.append(k)
        vsm.append(v)
        ssm.append(s_fin.astype(state_wkv.dtype))
        shs.append(hs[:, -1])
        hs = _merge_ffn(hs, o_attn, o_rwkv, zs[..., GATE_COL:], *tail)
    new_k_prompt = jnp.stack(kp)
    new_v_prompt = jnp.stack(vp)
    new_wkv_prompt = jnp.stack(sp)
    new_shift_prompt = jnp.stack(shp)
    new_k_sample = jnp.stack(ksm)
    new_v_sample = jnp.stack(vsm)
    new_wkv_sample = jnp.stack(ssm)
    new_shift_sample = jnp.stack(shs)
    return (hp, hs, new_k_prompt, new_v_prompt, new_wkv_prompt, new_shift_prompt,
            new_k_sample, new_v_sample, new_wkv_sample, new_shift_sample)
```

```python
import functools
import math

import numpy as np
import jax
import jax.numpy as jnp
from jax import lax
from jax.experimental import pallas as pl
from jax.experimental.pallas import tpu as pltpu

F32 = jnp.float32
BF16 = jnp.bfloat16

D_MODEL = 2048
HEAD_DIM = 128
ATTN_HEADS = D_MODEL // 256
ATTN_WIDTH = ATTN_HEADS * HEAD_DIM
MOBA_BLOCK = 256
MOBA_TOPK = 3
N_BUCKETS = 32
MAX_DISTANCE = 128
PAGE_SIZE = 128
RW_HEAD = 64
RW_HEADS = D_MODEL // 128
RW_WIDTH = RW_HEADS * RW_HEAD
DECAY_LORA = 96
AAA_LORA = 96
GATE_LORA = 256
LORA_COLS = DECAY_LORA + AAA_LORA + GATE_LORA
LORA_PAD = 512
GN_EPS = RW_HEAD * 1e-5
D_FF = 5632
LN_EPS = 1e-5
DEPTH = 1
DEEPNORM_ALPHA = (2 * DEPTH) ** 0.25

RKV_COL = 3 * ATTN_WIDTH
LORA_COL = RKV_COL + 3 * RW_WIDTH
GATE_COL = LORA_COL + LORA_COLS
MAIN_COLS = LORA_COL

NEG = -1e30
SEG_CHUNK = 256
PAGES_PER_BLOCK = MOBA_BLOCK // PAGE_SIZE
VMEM_LIMIT = 56 * 1024 * 1024


def _cparams(sem):
    return pltpu.CompilerParams(dimension_semantics=sem, vmem_limit_bytes=VMEM_LIMIT)


def _dot(a, b):
    return jnp.dot(a, b, preferred_element_type=F32)


def _dot_nt(a, b):
    return lax.dot_general(a, b, (((1,), (1,)), ((), ())), preferred_element_type=F32)


def _split(x):
    hi = x.astype(BF16)
    lo = (x - hi.astype(F32)).astype(BF16)
    return hi, lo


def _sigmoid(x):
    return 1.0 / (1.0 + jnp.exp(-x))


def _tile(m, pref):
    t = min(m, pref)
    assert m % t == 0, (m, pref)
    return t


def _proj_kernel(x_ref, w_ref, o_ref, xb_ref):
    @pl.when(pl.program_id(1) == 0)
    def _():
        xb_ref[...] = x_ref[...].astype(BF16)

    o_ref[...] = _dot(xb_ref[...], w_ref[...].astype(BF16)).astype(o_ref.dtype)


def _proj(x, w, col0, ncols, tm=1024, tn=512):
    m, k = x.shape
    tm = _tile(m, tm)
    tn = _tile(ncols, tn)
    assert col0 % tn == 0
    c0 = col0 // tn
    return pl.pallas_call(
        _proj_kernel,
        grid=(m // tm, ncols // tn),
        in_specs=[pl.BlockSpec((tm, k), lambda i, j: (i, 0)),
                  pl.BlockSpec((k, tn), lambda i, j: (0, c0 + j))],
        out_specs=pl.BlockSpec((tm, tn), lambda i, j: (i, j)),
        out_shape=jax.ShapeDtypeStruct((m, ncols), F32),
        scratch_shapes=[pltpu.VMEM((tm, k), BF16)],
        compiler_params=_cparams(("parallel", "arbitrary")),
        name="proj",
    )(x, w)


def _merge_kernel(oa_ref, or_ref, wa_ref, wr_ref, ga_ref, gr_ref, o_ref):
    a = _dot(oa_ref[...], wa_ref[...].astype(BF16))
    r = _dot(or_ref[...], wr_ref[...].astype(BF16))
    o_ref[...] = (_sigmoid(ga_ref[...]) * a + _sigmoid(gr_ref[...]) * r).astype(o_ref.dtype)


def _merge(o_attn, o_rwkv, zg, w_up_attn, w_up_rwkv, tm=1024, tn=512):
    m = o_attn.shape[0]
    tm = _tile(m, tm)
    nj = D_MODEL // tn
    return pl.pallas_call(
        _merge_kernel,
        grid=(m // tm, nj),
        in_specs=[pl.BlockSpec((tm, ATTN_WIDTH), lambda i, j: (i, 0)),
                  pl.BlockSpec((tm, RW_WIDTH), lambda i, j: (i, 0)),
                  pl.BlockSpec((ATTN_WIDTH, tn), lambda i, j: (0, j)),
                  pl.BlockSpec((RW_WIDTH, tn), lambda i, j: (0, j)),
                  pl.BlockSpec((tm, tn), lambda i, j: (i, j)),
                  pl.BlockSpec((tm, tn), lambda i, j: (i, j + nj))],
        out_specs=pl.BlockSpec((tm, tn), lambda i, j: (i, j)),
        out_shape=jax.ShapeDtypeStruct((m, D_MODEL), BF16),
        compiler_params=_cparams(("parallel", "arbitrary")),
        name="merge",
    )(o_attn, o_rwkv, w_up_attn, w_up_rwkv, zg, zg)


def _proj_ln_kernel(m_ref, w_ref, x_ref, g_ref, b_ref, o_ref, *, tn, nj):
    j = pl.program_id(1)
    col = pl.multiple_of(j * tn, tn)
    mb = m_ref[...].astype(BF16)
    o_ref[:, pl.ds(col, tn)] = DEEPNORM_ALPHA * x_ref[...] + _dot(mb, w_ref[...].astype(BF16))

    @pl.when(j == nj - 1)
    def _():
        y = o_ref[...]
        mu = jnp.mean(y, axis=-1, keepdims=True)
        d = y - mu
        var = jnp.mean(d * d, axis=-1, keepdims=True)
        o_ref[...] = d * lax.rsqrt(var + LN_EPS) * g_ref[...] + b_ref[...]


def _proj_ln(mat, w, x, g, b, tm=512, tn=256):
    m, k = mat.shape
    tm = _tile(m, tm)
    nj = D_MODEL // tn
    return pl.pallas_call(
        functools.partial(_proj_ln_kernel, tn=tn, nj=nj),
        grid=(m // tm, nj),
        in_specs=[pl.BlockSpec((tm, k), lambda i, j: (i, 0)),
                  pl.BlockSpec((k, tn), lambda i, j: (0, j)),
                  pl.BlockSpec((tm, tn), lambda i, j: (i, j)),
                  pl.BlockSpec((1, D_MODEL), lambda i, j: (0, 0)),
                  pl.BlockSpec((1, D_MODEL), lambda i, j: (0, 0))],
        out_specs=pl.BlockSpec((tm, D_MODEL), lambda i, j: (i, 0)),
        out_shape=jax.ShapeDtypeStruct((m, D_MODEL), F32),
        compiler_params=_cparams(("parallel", "arbitrary")),
        name="proj_ln",
    )(mat, w, x, g, b)


def _ffn_up_kernel(h_ref, wg_ref, wu_ref, o_ref, hb_ref):
    @pl.when(pl.program_id(1) == 0)
    def _():
        hb_ref[...] = h_ref[...].astype(BF16)

    hb = hb_ref[...]
    a = _dot(hb, wg_ref[...].astype(BF16))
    u = _dot(hb, wu_ref[...].astype(BF16))
    o_ref[...] = (a * _sigmoid(a) * u).astype(o_ref.dtype)


def _ffn_up(h, wg, wu, tm=1024, tn=256):
    m = h.shape[0]
    tm = _tile(m, tm)
    return pl.pallas_call(
        _ffn_up_kernel,
        grid=(m // tm, D_FF // tn),
        in_specs=[pl.BlockSpec((tm, D_MODEL), lambda i, j: (i, 0)),
                  pl.BlockSpec((D_MODEL, tn), lambda i, j: (0, j)),
                  pl.BlockSpec((D_MODEL, tn), lambda i, j: (0, j))],
        out_specs=pl.BlockSpec((tm, tn), lambda i, j: (i, j)),
        out_shape=jax.ShapeDtypeStruct((m, D_FF), BF16),
        scratch_shapes=[pltpu.VMEM((tm, D_MODEL), BF16)],
        compiler_params=_cparams(("parallel", "arbitrary")),
        name="ffn_up",
    )(h, wg, wu)


def _seg_sum(x, bones):
    outs = []
    for c in range(x.shape[1] // SEG_CHUNK):
        hi, lo = _split(x[:, c * SEG_CHUNK:(c + 1) * SEG_CHUNK])
        outs.append(_dot(hi, bones) + _dot(lo, bones))
    return jnp.concatenate(outs, axis=1)


def _shifted(z, prev):
    zs = pltpu.roll(z, 1, 0)
    row = lax.broadcasted_iota(jnp.int32, z.shape, 0)
    return jnp.where(row == 0, prev, zs)


def _rwkv_prep_kernel(z_ref, zl_ref, zp_ref, zlp_ref, mu_ref, mul_ref, w0_ref, a0_ref, w2_ref, a2_ref,
                      g2_ref, kkw_ref, kaw_ref, bones_ref,
                      r_o, w_o, k_o, v_o, kk_o, b_o, g_o, prev_ref, prevl_ref):
    @pl.when(pl.program_id(1) == 0)
    def _():
        prev_ref[...] = zp_ref[...]
        prevl_ref[...] = zlp_ref[...]

    z = z_ref[...]
    zl = zl_ref[...]
    tt = z.shape[0]
    zm = z + (_shifted(z, prev_ref[...]) - z) * mu_ref[...]
    zlm = zl + (_shifted(zl, prevl_ref[...]) - zl) * mul_ref[...]
    prev_ref[...] = z[tt - 1:tt, :]
    prevl_ref[...] = zl[tt - 1:tt, :]

    r = zm[:, 0:RW_WIDTH]
    k = zm[:, RW_WIDTH:2 * RW_WIDTH]
    v = zm[:, 2 * RW_WIDTH:3 * RW_WIDTH]
    xw = w0_ref[...] + _dot(jnp.tanh(zlm).astype(BF16), w2_ref[...])
    nx = -xw
    softplus = jnp.maximum(nx, 0.0) + jnp.log(1.0 + jnp.exp(-jnp.abs(nx)))
    decay = jnp.exp(-jnp.exp(-softplus - 0.5))
    a = _sigmoid(a0_ref[...] + _dot(zlm.astype(BF16), a2_ref[...]))
    g = _dot(_sigmoid(zlm).astype(BF16), g2_ref[...])
    kk = k * kkw_ref[...]
    ssq = _seg_sum(kk * kk, bones_ref[...])
    kk = kk / jnp.maximum(jnp.sqrt(ssq), 1e-12)
    r_o[...] = r
    w_o[...] = decay
    k_o[...] = k * (1.0 + (a - 1.0) * kaw_ref[...])
    v_o[...] = v
    kk_o[...] = kk
    b_o[...] = kk * a
    g_o[...] = g


def _rwkv_prep(z1, zl, zprev, zlprev, prm, tt):
    n, t, _ = z1.shape
    tt = _tile(t, tt)
    row = lambda c: pl.BlockSpec((1, c), lambda i, j: (0, 0))
    mat = lambda r, c: pl.BlockSpec((r, c), lambda i, j: (0, 0))
    seq = lambda c: pl.BlockSpec((None, tt, c), lambda i, j: (i, j, 0))
    outs = pl.pallas_call(
        _rwkv_prep_kernel,
        grid=(n, t // tt),
        in_specs=[pl.BlockSpec((None, tt, 3 * RW_WIDTH), lambda i, j: (i, j, 1)),
                  seq(LORA_PAD),
                  pl.BlockSpec((None, 1, 3 * RW_WIDTH), lambda i, j: (i, 0, 0)),
                  pl.BlockSpec((None, 1, LORA_PAD), lambda i, j: (i, 0, 0)),
                  row(3 * RW_WIDTH), row(LORA_PAD), row(RW_WIDTH), row(RW_WIDTH),
                  mat(LORA_PAD, RW_WIDTH), mat(LORA_PAD, RW_WIDTH), mat(LORA_PAD, RW_WIDTH),
                  row(RW_WIDTH), row(RW_WIDTH), mat(SEG_CHUNK, SEG_CHUNK)],
        out_specs=[seq(RW_WIDTH)] * 7,
        out_shape=[jax.ShapeDtypeStruct((n, t, RW_WIDTH), F32)] * 7,
        scratch_shapes=[pltpu.VMEM((1, 3 * RW_WIDTH), F32), pltpu.VMEM((1, LORA_PAD), F32)],
        compiler_params=_cparams(("parallel", "arbitrary")),
        name="rwkv_prep",
    )(z1, zl, zprev, zlprev, prm["mu"], prm["mul"], prm["w0"], prm["a0"], prm["w2"], prm["a2"],
      prm["g2"], prm["k_k"], prm["k_a"], prm["bones"])
    return outs


def _rwkv_scan_kernel(r_ref, w_ref, k_ref, v_ref, kk_ref, b_ref, s0_ref, bones_ref, eye_ref,
                      y_ref, sout_ref, s_ref, *, tb, nblk):
    t = pl.program_id(1)

    @pl.when(t == 0)
    def _():
        s_ref[...] = s0_ref[...]

    bones = bones_ref[...]
    eye = eye_ref[...]

    def bc(ref, s):
        return jnp.concatenate(
            [jnp.broadcast_to(ref[pl.ds(s, 1), c * SEG_CHUNK:(c + 1) * SEG_CHUNK], (RW_HEAD, SEG_CHUNK))
             for c in range(4)], axis=0)

    def step(s, carry):
        st = s_ref[...]
        hi, lo = _split(st * bc(kk_ref, s))
        sa = _dot(hi, bones) + _dot(lo, bones)
        vcol = _dot((bc(v_ref, s) * eye).astype(BF16), bones)
        sn = st * bc(w_ref, s) - sa * bc(b_ref, s) + vcol * bc(k_ref, s)
        s_ref[...] = sn
        yb = _dot((sn * bc(r_ref, s)).astype(BF16), bones)
        y4 = jnp.sum((yb * eye).reshape(4, RW_HEAD, SEG_CHUNK), axis=1)
        for c in range(4):
            y_ref[pl.ds(s, 1), c * SEG_CHUNK:(c + 1) * SEG_CHUNK] = y4[c:c + 1, :]
        return carry

    lax.fori_loop(0, tb, step, 0)

    @pl.when(t == nblk - 1)
    def _():
        sout_ref[...] = s_ref[...]


def _rwkv_scan(r, w, k, v, kk, b, s0, prm, tb=8):
    n, t, _ = r.shape
    tb = _tile(t, tb)
    nblk = t // tb
    seq = pl.BlockSpec((None, tb, RW_WIDTH), lambda i, j: (i, j, 0))
    st = pl.BlockSpec((None, 4 * RW_HEAD, SEG_CHUNK), lambda i, j: (i, 0, 0))
    cst = pl.BlockSpec((SEG_CHUNK, SEG_CHUNK), lambda i, j: (0, 0))
    return pl.pallas_call(
        functools.partial(_rwkv_scan_kernel, tb=tb, nblk=nblk),
        grid=(n, nblk),
        in_specs=[seq] * 6 + [st, cst, cst],
        out_specs=[seq, st],
        out_shape=[jax.ShapeDtypeStruct((n, t, RW_WIDTH), F32),
                   jax.ShapeDtypeStruct((n, 4 * RW_HEAD, SEG_CHUNK), F32)],
        scratch_shapes=[pltpu.VMEM((4 * RW_HEAD, SEG_CHUNK), F32)],
        compiler_params=_cparams(("parallel", "arbitrary")),
        name="rwkv_scan",
    )(r, w, k, v, kk, b, s0, prm["bones"], prm["eye4"])


def _rwkv_post_kernel(y_ref, r_ref, k_ref, v_ref, g_ref, rk_ref, lg_ref, lb_ref, bones_ref, o_ref):
    bones = bones_ref[...]
    y = y_ref[...]
    d = y - _seg_sum(y, bones) * (1.0 / RW_HEAD)
    var = _seg_sum(d * d, bones) * (1.0 / RW_HEAD)
    yn = d * lax.rsqrt(var + GN_EPS) * lg_ref[...] + lb_ref[...]
    bonus = _seg_sum(r_ref[...] * k_ref[...] * rk_ref[...], bones) * v_ref[...]
    o_ref[...] = ((yn + bonus) * g_ref[...]).astype(o_ref.dtype)


def _rwkv_post(y, r, k, v, g, prm, tm=256):
    m = y.shape[0]
    tm = _tile(m, tm)
    big = pl.BlockSpec((tm, RW_WIDTH), lambda i: (i, 0))
    row = pl.BlockSpec((1, RW_WIDTH), lambda i: (0, 0))
    return pl.pallas_call(
        _rwkv_post_kernel,
        grid=(m // tm,),
        in_specs=[big] * 5 + [row] * 3 + [pl.BlockSpec((SEG_CHUNK, SEG_CHUNK), lambda i: (0, 0))],
        out_specs=big,
        out_shape=jax.ShapeDtypeStruct((m, RW_WIDTH), BF16),
        compiler_params=_cparams(("parallel",)),
        name="rwkv_post",
    )(y, r, k, v, g, prm["r_k"], prm["ln_g"], prm["ln_b"], prm["bones"])


def _state_to_stacked(s):
    n = s.shape[0]
    s = s.reshape(n, 4, 4, RW_HEAD, RW_HEAD)
    return jnp.transpose(s, (0, 1, 3, 2, 4)).reshape(n, 4 * RW_HEAD, SEG_CHUNK)


def _state_from_stacked(s):
    n = s.shape[0]
    s = s.reshape(n, 4, RW_HEAD, 4, RW_HEAD)
    return jnp.transpose(s, (0, 1, 3, 2, 4)).reshape(n, RW_HEADS, RW_HEAD, RW_HEAD)


def _top3_rows(sc, idx, nvalid_mask):
    big = float(sc.shape[0])
    sc = jnp.where(nvalid_mask, sc, NEG)
    sel = jnp.zeros(sc.shape, F32)
    for _ in range(MOBA_TOPK):
        mx = jnp.max(sc, axis=0, keepdims=True)
        first = jnp.min(jnp.where(sc == mx, idx, big), axis=0, keepdims=True)
        hit = (idx == first) & (mx > 0.5 * NEG)
        sel = jnp.where(hit, 1.0, sel)
        sc = jnp.where(hit, NEG, sc)
    return sel


def _moba_prompt_kernel(q_ref, k_ref, v_ref, bown_ref, bprev_ref, bfar_ref, o_ref,
                        kb_ref, vt_ref, kmh_ref, kml_ref, sel_ref, m_ref, l_ref, acc_ref, *, nb):
    i = pl.program_id(1)
    blk = MOBA_BLOCK

    @pl.when(i == 0)
    def _():
        def prep(j, c):
            rows = pl.ds(pl.multiple_of(j * blk, blk), blk)
            kj = k_ref[rows, :]
            kb_ref[rows, :] = kj.astype(BF16)
            km = jnp.sum(kj, axis=0, keepdims=True) * (1.0 / blk)
            hi, lo = _split(km)
            kmh_ref[pl.ds(j, 1), :] = hi.astype(F32)
            kml_ref[pl.ds(j, 1), :] = lo.astype(F32)
            vt_ref[j] = v_ref[rows, :].T.astype(BF16)
            return c
        lax.fori_loop(0, nb, prep, 0)

    qt = q_ref[...].T
    qh, ql = _split(qt)
    kmh = kmh_ref[...].astype(BF16)
    kml = kml_ref[...].astype(BF16)
    sc = _dot(kmh, qh) + _dot(kmh, ql) + _dot(kml, qh)
    bidx = lax.broadcasted_iota(jnp.int32, sc.shape, 0)
    sel_ref[...] = _top3_rows(sc, bidx.astype(F32), bidx < i)

    qs = (qt * (HEAD_DIM ** -0.5)).astype(BF16)

    def keys(j):
        return kb_ref[pl.ds(pl.multiple_of(j * blk, blk), blk), :]

    s = _dot(keys(i), qs) + bown_ref[...]
    kidx = lax.broadcasted_iota(jnp.int32, s.shape, 0)
    qidx = lax.broadcasted_iota(jnp.int32, s.shape, 1)
    s = jnp.where(kidx <= qidx, s, NEG)
    m0 = jnp.max(s, axis=0, keepdims=True)
    p = jnp.exp(s - m0)
    m_ref[...] = m0
    l_ref[...] = jnp.sum(p, axis=0, keepdims=True)
    acc_ref[...] = _dot(vt_ref[i], p.astype(BF16))

    def update(j, bias):
        s = _dot(keys(j), qs) + bias
        s = jnp.where(sel_ref[pl.ds(j, 1), :] > 0.5, s, NEG)
        m_old = m_ref[...]
        m_new = jnp.maximum(m_old, jnp.max(s, axis=0, keepdims=True))
        a = jnp.exp(m_old - m_new)
        p = jnp.exp(s - m_new)
        l_ref[...] = a * l_ref[...] + jnp.sum(p, axis=0, keepdims=True)
        acc_ref[...] = a * acc_ref[...] + _dot(vt_ref[j], p.astype(BF16))
        m_ref[...] = m_new

    @pl.when(i >= 1)
    def _():
        update(i - 1, bprev_ref[...])

    def far(j, c):
        update(j, bfar_ref[0:1, 0:1])
        return c
    lax.fori_loop(0, jnp.maximum(i - 1, 0), far, 0)

    o_ref[...] = (acc_ref[...] / l_ref[...]).T.astype(o_ref.dtype)


def _moba_prompt(z1, bias):
    t = z1.shape[0]
    assert t % MOBA_BLOCK == 0
    nb = t // MOBA_BLOCK
    blk = MOBA_BLOCK
    h8 = ATTN_HEADS
    tile = pl.BlockSpec((None, blk, blk), lambda h, i: (h, 0, 0))
    return pl.pallas_call(
        functools.partial(_moba_prompt_kernel, nb=nb),
        grid=(h8, nb),
        in_specs=[pl.BlockSpec((blk, HEAD_DIM), lambda h, i: (i, h)),
                  pl.BlockSpec((t, HEAD_DIM), lambda h, i: (0, h8 + h)),
                  pl.BlockSpec((t, HEAD_DIM), lambda h, i: (0, 2 * h8 + h)),
                  tile, tile,
                  pl.BlockSpec((None, 1, HEAD_DIM), lambda h, i: (h, 0, 0))],
        out_specs=pl.BlockSpec((blk, HEAD_DIM), lambda h, i: (i, h)),
        out_shape=jax.ShapeDtypeStruct((t, ATTN_WIDTH), BF16),
        scratch_shapes=[pltpu.VMEM((t, HEAD_DIM), BF16),
                        pltpu.VMEM((nb, HEAD_DIM, blk), BF16),
                        pltpu.VMEM((nb, HEAD_DIM), F32),
                        pltpu.VMEM((nb, HEAD_DIM), F32),
                        pltpu.VMEM((nb, blk), F32),
                        pltpu.VMEM((1, blk), F32),
                        pltpu.VMEM((1, blk), F32),
                        pltpu.VMEM((HEAD_DIM, blk), F32)],
        compiler_params=_cparams(("arbitrary", "arbitrary")),
        name="moba_prompt",
    )(z1, z1, z1, bias["own_t"], bias["prev_t"], bias["far"])


def _diag_extract(s):
    lane = lax.broadcasted_iota(jnp.int32, (ATTN_HEADS, s.shape[1]), 1) % ATTN_HEADS
    out = jnp.zeros((ATTN_HEADS, s.shape[1]), F32)
    for hp in range(ATTN_HEADS):
        out = out + jnp.where(lane == hp, s[hp * 8:(hp + 1) * 8, :], 0.0)
    return out


def _diag_expand(p):
    lane = lax.broadcasted_iota(jnp.int32, p.shape, 1) % ATTN_HEADS
    return jnp.concatenate([jnp.where(lane == hp, p, 0.0) for hp in range(ATTN_HEADS)], axis=0)


def _class_allreduce(x, op):
    for sh in (8, 16, 32, 64):
        x = op(x, pltpu.roll(x, sh, 1))
    return x


def _fold_tiles(x, op):
    out = x[:, 0:128]
    for c in range(1, x.shape[1] // 128):
        out = op(out, x[:, c * 128:(c + 1) * 128])
    return out


def _moba_sample_kernel(pt_ref, q_ref, k0_ref, k1_ref, v0_ref, v1_ref, kn_ref, vn_ref,
                        blast_ref, bown_ref, bfar_ref, o_ref,
                        sc_ref, bsum_ref, sel_ref, m_ref, li_ref, acc_ref, *, nblk):
    ph = pl.program_id(1)
    jj = pl.program_id(2)
    ntok = 8
    rows = PAGE_SIZE * ATTN_HEADS
    nkeys = 2 * rows
    far16 = jnp.concatenate([bfar_ref[...]] * (nkeys // 128), axis=1)

    def block_bias(j):
        return jnp.where(j == nblk - 1, blast_ref[...], far16)

    def tile16(x):
        return jnp.concatenate([x] * (nkeys // 128), axis=1)

    @pl.when(ph == 0)
    def _():
        k0 = k0_ref[...]
        k1 = k1_ref[...]
        bsum_ref[pl.ds(pl.multiple_of(jj * 8, 8), 8), :] = jnp.sum(k0, axis=0) + jnp.sum(k1, axis=0)
        k2 = jnp.concatenate([k0.reshape(rows, HEAD_DIM), k1.reshape(rows, HEAD_DIM)], axis=0).astype(BF16)
        qs = (q_ref[...] * (HEAD_DIM ** -0.5)).astype(BF16)
        sc_ref[jj] = _diag_extract(_dot_nt(qs, k2))

    @pl.when((ph == 0) & (jj == nblk - 1))
    def _():
        q = q_ref[...]
        qh, ql = _split(q)
        bh, bl = _split(bsum_ref[...] * (1.0 / MOBA_BLOCK))
        scx = _diag_extract(_dot_nt(qh, bh) + _dot_nt(qh, bl) + _dot_nt(ql, bh))
        width = nblk * 8
        jidx = (lax.broadcasted_iota(jnp.int32, (ntok, width), 1) // 8).astype(F32)

        def creduce(x, op):
            y = _class_allreduce(_fold_tiles(x, op), op)
            return jnp.concatenate([y] * (width // 128), axis=1)

        selx = jnp.zeros((ntok, width), F32)
        for _ in range(MOBA_TOPK):
            mx = creduce(scx, jnp.maximum)
            first = creduce(jnp.where(scx == mx, jidx, float(nblk)), jnp.minimum)
            hit = jidx == first
            selx = jnp.where(hit, 1.0, selx)
            scx = jnp.where(hit, NEG, scx)
        selb = selx.astype(BF16)
        erow = lax.broadcasted_iota(jnp.int32, (width, 128), 0)
        ecls = lax.broadcasted_iota(jnp.int32, (width, 128), 1) % 8

        def spread(j, c):
            e = jnp.where(erow == j * 8 + ecls, 1.0, 0.0).astype(BF16)
            sel_ref[j] = _dot(selb, e)
            return c
        lax.fori_loop(0, nblk, spread, 0)

        qs = (q * (HEAD_DIM ** -0.5)).astype(BF16)
        so = _diag_extract(_dot_nt(qs, kn_ref[...].astype(BF16))) + bown_ref[...]
        tq = lax.broadcasted_iota(jnp.int32, so.shape, 0)
        tk = lax.broadcasted_iota(jnp.int32, so.shape, 1) // 8
        so = jnp.concatenate([jnp.where(tk <= tq, so, NEG), jnp.full((ntok, 64), NEG, F32)], axis=1)

        def masked(j):
            x = sc_ref[j] + block_bias(j)
            return jnp.where(tile16(sel_ref[j]) > 0.5, x, NEG)

        def mx_body(j, m):
            return jnp.maximum(m, _fold_tiles(masked(j), jnp.maximum))
        m = lax.fori_loop(0, nblk, mx_body, so)
        m = _class_allreduce(m, jnp.maximum)
        m16 = tile16(m)

        def sum_body(j, l):
            return l + _fold_tiles(jnp.exp(masked(j) - m16), jnp.add)
        po = jnp.exp(so - m)
        l = lax.fori_loop(0, nblk, sum_body, po)
        li = 1.0 / _class_allreduce(l, jnp.add)
        m_ref[...] = m
        li_ref[...] = li
        pfull = _diag_expand((po * li)[:, 0:64]).astype(BF16)
        acc_ref[...] = _dot(pfull, vn_ref[...].astype(BF16))

    @pl.when(ph == 1)
    def _():
        x = sc_ref[jj] + block_bias(jj)
        p = jnp.exp(x - tile16(m_ref[...])) * tile16(li_ref[...])
        p = jnp.where(tile16(sel_ref[jj]) > 0.5, p, 0.0)
        v2 = jnp.concatenate([v0_ref[...].reshape(rows, HEAD_DIM), v1_ref[...].reshape(rows, HEAD_DIM)],
                             axis=0).astype(BF16)
        acc_ref[...] += _dot(_diag_expand(p).astype(BF16), v2)

    @pl.when((ph == 1) & (jj == nblk - 1))
    def _():
        o_ref[...] = acc_ref[...]


def _moba_sample(qht, knew, vnew, cache_k, cache_v, page_table, bias):
    nb_, n_pages = page_table.shape
    assert n_pages % PAGES_PER_BLOCK == 0 and PAGES_PER_BLOCK == 2
    nblk = n_pages // 2
    assert nblk >= MOBA_TOPK and (nblk * 8) % 128 == 0
    nkeys = 2 * PAGE_SIZE * ATTN_HEADS

    def kmap(u):
        return lambda b, p, j, pt: (0, pt[b, 2 * jnp.where(p == 0, j, nblk - 1) + u], 0, 0, 0)

    def vmap_(u):
        return lambda b, p, j, pt: (0, pt[b, 2 * jnp.where(p == 0, 0, j) + u], 0, 0, 0)

    page = (None, None, PAGE_SIZE, ATTN_HEADS, HEAD_DIM)
    per_b = pl.BlockSpec((None, 64, HEAD_DIM), lambda b, p, j, pt: (b, 0, 0))
    cst = lambda r, c: pl.BlockSpec((r, c), lambda b, p, j, pt: (0, 0))
    grid_spec = pltpu.PrefetchScalarGridSpec(
        num_scalar_prefetch=1,
        grid=(nb_, 2, nblk),
        in_specs=[per_b,
                  pl.BlockSpec(page, kmap(0)), pl.BlockSpec(page, kmap(1)),
                  pl.BlockSpec(page, vmap_(0)), pl.BlockSpec(page, vmap_(1)),
                  per_b, per_b, cst(8, nkeys), cst(8, 64), cst(1, 128)],
        out_specs=per_b,
        scratch_shapes=[pltpu.VMEM((nblk, 8, nkeys), F32),
                        pltpu.VMEM((nblk * 8, HEAD_DIM), F32),
                        pltpu.VMEM((nblk, 8, 128), F32),
                        pltpu.VMEM((8, 128), F32),
                        pltpu.VMEM((8, 128), F32),
                        pltpu.VMEM((64, HEAD_DIM), F32)])
    return pl.pallas_call(
        functools.partial(_moba_sample_kernel, nblk=nblk),
        grid_spec=grid_spec,
        out_shape=jax.ShapeDtypeStruct((nb_, 64, HEAD_DIM), F32),
        compiler_params=_cparams(("arbitrary", "arbitrary", "arbitrary")),
        name="moba_sample",
    )(page_table, qht, cache_k, cache_k, cache_v, cache_v, knew, vnew,
      bias["last_s"], bias["own_s"], bias["far_s"])


def _bias_of_distance(dist, rel_bias):
    dist = jnp.maximum(dist, 0)
    exact = N_BUCKETS // 2
    log_ratio = jnp.log(jnp.maximum(dist, 1).astype(F32) / exact) / math.log(MAX_DISTANCE / exact)
    large = jnp.minimum(exact + (log_ratio * (N_BUCKETS - exact)).astype(jnp.int32), N_BUCKETS - 1)
    return rel_bias[jnp.where(dist < exact, dist, large)]


def _bias_tiles(rel_bias, past_len, dec_seq):
    blk = MOBA_BLOCK
    key = jnp.arange(blk)[:, None]
    qry = jnp.arange(blk)[None, :]
    own_t = jnp.transpose(_bias_of_distance(qry - key, rel_bias), (2, 0, 1))
    prev_t = jnp.transpose(_bias_of_distance(blk + qry - key, rel_bias), (2, 0, 1))
    far = jnp.broadcast_to(_bias_of_distance(jnp.array(2 * blk), rel_bias)[:, None, None],
                           (ATTN_HEADS, 1, HEAD_DIM))
    assert past_len % blk == 0
    tq = jnp.arange(dec_seq)[:, None]
    row = jnp.arange(blk)[None, :]
    hidx = jnp.arange(ATTN_HEADS)
    last = _bias_of_distance(blk + tq - row, rel_bias)
    last_s = last.reshape(dec_seq, blk * ATTN_HEADS)
    own = _bias_of_distance(tq - jnp.arange(dec_seq)[None, :], rel_bias)
    own_s = own.reshape(dec_seq, dec_seq * ATTN_HEADS)
    far_s = jnp.tile(_bias_of_distance(jnp.array(2 * blk), rel_bias)[hidx], 128 // ATTN_HEADS)[None, :]
    return dict(own_t=own_t, prev_t=prev_t, far=far, last_s=last_s, own_s=own_s, far_s=far_s)


def _np_consts():
    seg = np.arange(SEG_CHUNK) // RW_HEAD
    bones = (seg[:, None] == seg[None, :]).astype(np.float32)
    idx = np.arange(SEG_CHUNK) % RW_HEAD
    eye4 = (idx[:, None] == idx[None, :]).astype(np.float32)
    return bones, eye4


def _pad_rows(w, row0, total):
    return jnp.zeros((total, w.shape[1]), BF16).at[row0:row0 + w.shape[0]].set(w.astype(BF16))


def kernel(x_prompt, x_sample, cache_k, cache_v, page_table, state_wkv, state_shift, w_in, rel_bias, rw_mu, rw_w0, rw_w2, rw_a0, rw_a2, rw_g2, rw_k_k, rw_k_a, rw_r_k, rw_ln_g, rw_ln_b, w_up_attn, w_up_rwkv, w_o, ln1_g, ln1_b, w_ffn_gate, w_ffn_up, w_ffn_down, ln2_g, ln2_b):
    assert x_prompt.shape[0] == 1 and w_in.shape[0] == DEPTH == 1
    t_p = x_prompt.shape[1]
    nb_s, t_s, _ = x_sample.shape
    assert t_s == 8
    past_len = page_table.shape[1] * PAGE_SIZE

    w_in2 = w_in[0]
    w_lora = jnp.pad(w_in2[:, LORA_COL:GATE_COL], ((0, 0), (0, LORA_PAD - LORA_COLS)))
    w_gate = w_in2[:, GATE_COL:]
    bones_np, eye_np = _np_consts()
    mu = rw_mu[0]
    prm = dict(
        mu=mu[None, :3 * RW_WIDTH],
        mul=jnp.pad(mu[3 * RW_WIDTH:], (0, LORA_PAD - LORA_COLS))[None, :],
        w0=rw_w0, a0=rw_a0,
        w2=_pad_rows(rw_w2[0], 0, LORA_PAD),
        a2=_pad_rows(rw_a2[0], DECAY_LORA, LORA_PAD),
        g2=_pad_rows(rw_g2[0], DECAY_LORA + AAA_LORA, LORA_PAD),
        k_k=rw_k_k, k_a=rw_k_a, r_k=rw_r_k.reshape(1, RW_WIDTH), ln_g=rw_ln_g, ln_b=rw_ln_b,
        bones=jnp.asarray(bones_np, BF16), eye4=jnp.asarray(eye_np, F32))
    bias = _bias_tiles(rel_bias, past_len, t_s)

    def group(x2d, nseq, tseq, zprev, zlprev, s0, attend, tt, tb):
        z1 = _proj(x2d, w_in2, 0, MAIN_COLS)
        zl = _proj(x2d, w_lora, 0, LORA_PAD)
        zg = _proj(x2d, w_gate, 0, 2 * D_MODEL)
        o_attn = attend(z1)
        r, w, k, v, kk, b, g = _rwkv_prep(z1.reshape(nseq, tseq, MAIN_COLS), zl.reshape(nseq, tseq, LORA_PAD),
                                          zprev, zlprev, prm, tt)
        y, s_fin = _rwkv_scan(r, w, k, v, kk, b, s0, prm, tb)
        flat = lambda u: u.reshape(nseq * tseq, RW_WIDTH)
        o_rwkv = _rwkv_post(flat(y), flat(r), flat(k), flat(v), flat(g), prm)
        mixed = _merge(o_attn, o_rwkv, zg, w_up_attn[0], w_up_rwkv[0])
        h = _proj_ln(mixed, w_o[0], x2d, ln1_g, ln1_b)
        act = _ffn_up(h, w_ffn_gate[0], w_ffn_up[0])
        out = _proj_ln(act, w_ffn_down[0], h, ln2_g, ln2_b)
        new_k = z1[:, ATTN_WIDTH:2 * ATTN_WIDTH]
        new_v = z1[:, 2 * ATTN_WIDTH:3 * ATTN_WIDTH]
        return out, new_k, new_v, _state_from_stacked(s_fin)

    xp = x_prompt[0]
    yp, kp, vp, sp = group(
        xp, 1, t_p,
        jnp.zeros((1, 1, 3 * RW_WIDTH), F32), jnp.zeros((1, 1, LORA_PAD), F32),
        jnp.zeros((1, 4 * RW_HEAD, SEG_CHUNK), F32),
        lambda z1: _moba_prompt(z1, bias), 256, 8)

    xs = x_sample.reshape(nb_s * t_s, D_MODEL)
    sh = state_shift[0]
    zprev = _proj(sh, w_in2, RKV_COL, 3 * RW_WIDTH)[:, None, :]
    zlprev = _proj(sh, w_lora, 0, LORA_PAD)[:, None, :]

    def attend_sample(z1):
        q = z1[:, 0:ATTN_WIDTH].reshape(nb_s, t_s, ATTN_HEADS, HEAD_DIM)
        qht = jnp.transpose(q, (0, 2, 1, 3)).reshape(nb_s, ATTN_HEADS * t_s, HEAD_DIM)
        kn = z1[:, ATTN_WIDTH:2 * ATTN_WIDTH].reshape(nb_s, t_s * ATTN_HEADS, HEAD_DIM)
        vn = z1[:, 2 * ATTN_WIDTH:3 * ATTN_WIDTH].reshape(nb_s, t_s * ATTN_HEADS, HEAD_DIM)
        o = _moba_sample(qht, kn, vn, cache_k, cache_v, page_table, bias)
        o = jnp.transpose(o.reshape(nb_s, ATTN_HEADS, t_s, HEAD_DIM), (0, 2, 1, 3))
        return o.reshape(nb_s * t_s, ATTN_WIDTH).astype(BF16)

    ys, ks, vs, ss = group(xs, nb_s, t_s, zprev, zlprev, _state_to_stacked(state_wkv[0]),
                           attend_sample, 8, 8)

    return (yp[None], ys.reshape(nb_s, t_s, D_MODEL),
            kp.reshape(1, 1, t_p, ATTN_HEADS, HEAD_DIM), vp.reshape(1, 1, t_p, ATTN_HEADS, HEAD_DIM),
            sp[None], xp[None, -1:, :],
            ks.reshape(1, nb_s, t_s, ATTN_HEADS, HEAD_DIM), vs.reshape(1, nb_s, t_s, ATTN_HEADS, HEAD_DIM),
            ss[None], x_sample[None, :, -1, :])
```

```python
import functools
import math

import numpy as np
import jax
import jax.numpy as jnp
from jax import lax
from jax.experimental import pallas as pl
from jax.experimental.pallas import tpu as pltpu

F32 = jnp.float32
BF16 = jnp.bfloat16

D_MODEL = 2048
HEAD_DIM = 128
ATTN_HEADS = D_MODEL // 256
ATTN_WIDTH = ATTN_HEADS * HEAD_DIM
MOBA_BLOCK = 256
MOBA_TOPK = 3
N_BUCKETS = 32
MAX_DISTANCE = 128
PAGE_SIZE = 128
RW_HEAD = 64
RW_HEADS = D_MODEL // 128
RW_WIDTH = RW_HEADS * RW_HEAD
DECAY_LORA = 96
AAA_LORA = 96
GATE_LORA = 256
LORA_COLS = DECAY_LORA + AAA_LORA + GATE_LORA
LORA_PAD = 512
GN_EPS = RW_HEAD * 1e-5
D_FF = 5632
LN_EPS = 1e-5
DEPTH = 1
DEEPNORM_ALPHA = (2 * DEPTH) ** 0.25

RKV_COL = 3 * ATTN_WIDTH
LORA_COL = RKV_COL + 3 * RW_WIDTH
GATE_COL = LORA_COL + LORA_COLS
MAIN_COLS = LORA_COL

NEG = -1e30
SEG_CHUNK = 256
PAGES_PER_BLOCK = MOBA_BLOCK // PAGE_SIZE
SAMPLE_BLOCKS_PER_STEP = 4
RW_CHUNK = 64
VMEM_LIMIT = 56 * 1024 * 1024


def _cparams(sem):
    return pltpu.CompilerParams(dimension_semantics=sem, vmem_limit_bytes=VMEM_LIMIT)


def _dot(a, b):
    return jnp.dot(a, b, preferred_element_type=F32)


def _dot_nt(a, b):
    return lax.dot_general(a, b, (((1,), (1,)), ((), ())), preferred_element_type=F32)


def _split(x):
    hi = x.astype(BF16)
    lo = (x - hi.astype(F32)).astype(BF16)
    return hi, lo


def _sigmoid(x):
    return 1.0 / (1.0 + jnp.exp(-x))


def _tile(m, pref):
    t = min(m, pref)
    assert m % t == 0, (m, pref)
    return t


def _proj_kernel(x_ref, w_ref, o_ref, xb_ref):
    @pl.when(pl.program_id(1) == 0)
    def _():
        xb_ref[...] = x_ref[...].astype(BF16)

    o_ref[...] = _dot(xb_ref[...], w_ref[...].astype(BF16)).astype(o_ref.dtype)


def _proj(x, w, col0, ncols, tm=1024, tn=512):
    m, k = x.shape
    tm = _tile(m, tm)
    tn = _tile(ncols, tn)
    assert col0 % tn == 0
    c0 = col0 // tn
    return pl.pallas_call(
        _proj_kernel,
        grid=(m // tm, ncols // tn),
        in_specs=[pl.BlockSpec((tm, k), lambda i, j: (i, 0)),
                  pl.BlockSpec((k, tn), lambda i, j: (0, c0 + j))],
        out_specs=pl.BlockSpec((tm, tn), lambda i, j: (i, j)),
        out_shape=jax.ShapeDtypeStruct((m, ncols), F32),
        scratch_shapes=[pltpu.VMEM((tm, k), BF16)],
        compiler_params=_cparams(("parallel", "arbitrary")),
        name="proj",
    )(x, w)


def _merge_kernel(oa_ref, or_ref, wa_ref, wr_ref, ga_ref, gr_ref, o_ref):
    a = _dot(oa_ref[...], wa_ref[...].astype(BF16))
    r = _dot(or_ref[...], wr_ref[...].astype(BF16))
    o_ref[...] = (_sigmoid(ga_ref[...]) * a + _sigmoid(gr_ref[...]) * r).astype(o_ref.dtype)


def _merge(o_attn, o_rwkv, zg, w_up_attn, w_up_rwkv, tm=1024, tn=512):
    m = o_attn.shape[0]
    tm = _tile(m, tm)
    nj = D_MODEL // tn
    return pl.pallas_call(
        _merge_kernel,
        grid=(m // tm, nj),
        in_specs=[pl.BlockSpec((tm, ATTN_WIDTH), lambda i, j: (i, 0)),
                  pl.BlockSpec((tm, RW_WIDTH), lambda i, j: (i, 0)),
                  pl.BlockSpec((ATTN_WIDTH, tn), lambda i, j: (0, j)),
                  pl.BlockSpec((RW_WIDTH, tn), lambda i, j: (0, j)),
                  pl.BlockSpec((tm, tn), lambda i, j: (i, j)),
                  pl.BlockSpec((tm, tn), lambda i, j: (i, j + nj))],
        out_specs=pl.BlockSpec((tm, tn), lambda i, j: (i, j)),
        out_shape=jax.ShapeDtypeStruct((m, D_MODEL), BF16),
        compiler_params=_cparams(("parallel", "arbitrary")),
        name="merge",
    )(o_attn, o_rwkv, w_up_attn, w_up_rwkv, zg, zg)


def _proj_ln_kernel(m_ref, w_ref, x_ref, g_ref, b_ref, o_ref, *, tn, nj):
    j = pl.program_id(1)
    col = pl.multiple_of(j * tn, tn)
    mb = m_ref[...].astype(BF16)
    o_ref[:, pl.ds(col, tn)] = DEEPNORM_ALPHA * x_ref[...] + _dot(mb, w_ref[...].astype(BF16))

    @pl.when(j == nj - 1)
    def _():
        y = o_ref[...]
        mu = jnp.mean(y, axis=-1, keepdims=True)
        d = y - mu
        var = jnp.mean(d * d, axis=-1, keepdims=True)
        o_ref[...] = d * lax.rsqrt(var + LN_EPS) * g_ref[...] + b_ref[...]


def _proj_ln(mat, w, x, g, b, tm=512, tn=256):
    m, k = mat.shape
    tm = _tile(m, tm)
    nj = D_MODEL // tn
    return pl.pallas_call(
        functools.partial(_proj_ln_kernel, tn=tn, nj=nj),
        grid=(m // tm, nj),
        in_specs=[pl.BlockSpec((tm, k), lambda i, j: (i, 0)),
                  pl.BlockSpec((k, tn), lambda i, j: (0, j)),
                  pl.BlockSpec((tm, tn), lambda i, j: (i, j)),
                  pl.BlockSpec((1, D_MODEL), lambda i, j: (0, 0)),
                  pl.BlockSpec((1, D_MODEL), lambda i, j: (0, 0))],
        out_specs=pl.BlockSpec((tm, D_MODEL), lambda i, j: (i, 0)),
        out_shape=jax.ShapeDtypeStruct((m, D_MODEL), F32),
        compiler_params=_cparams(("parallel", "arbitrary")),
        name="proj_ln",
    )(mat, w, x, g, b)


def _ffn_up_kernel(h_ref, wg_ref, wu_ref, o_ref, hb_ref):
    @pl.when(pl.program_id(1) == 0)
    def _():
        hb_ref[...] = h_ref[...].astype(BF16)

    hb = hb_ref[...]
    a = _dot(hb, wg_ref[...].astype(BF16))
    u = _dot(hb, wu_ref[...].astype(BF16))
    o_ref[...] = (a * _sigmoid(a) * u).astype(o_ref.dtype)


def _ffn_up(h, wg, wu, tm=1024, tn=256):
    m = h.shape[0]
    tm = _tile(m, tm)
    return pl.pallas_call(
        _ffn_up_kernel,
        grid=(m // tm, D_FF // tn),
        in_specs=[pl.BlockSpec((tm, D_MODEL), lambda i, j: (i, 0)),
                  pl.BlockSpec((D_MODEL, tn), lambda i, j: (0, j)),
                  pl.BlockSpec((D_MODEL, tn), lambda i, j: (0, j))],
        out_specs=pl.BlockSpec((tm, tn), lambda i, j: (i, j)),
        out_shape=jax.ShapeDtypeStruct((m, D_FF), BF16),
        scratch_shapes=[pltpu.VMEM((tm, D_MODEL), BF16)],
        compiler_params=_cparams(("parallel", "arbitrary")),
        name="ffn_up",
    )(h, wg, wu)


def _seg_sum(x, bones):
    outs = []
    for c in range(x.shape[1] // SEG_CHUNK):
        hi, lo = _split(x[:, c * SEG_CHUNK:(c + 1) * SEG_CHUNK])
        outs.append(_dot(hi, bones) + _dot(lo, bones))
    return jnp.concatenate(outs, axis=1)


def _shifted(z, prev):
    zs = pltpu.roll(z, 1, 0)
    row = lax.broadcasted_iota(jnp.int32, z.shape, 0)
    return jnp.where(row == 0, prev, zs)


def _rwkv_prep_kernel(z_ref, zl_ref, zp_ref, zlp_ref, mu_ref, mul_ref, w0_ref, a0_ref, w2_ref, a2_ref,
                      g2_ref, kkw_ref, kaw_ref, bones_ref,
                      r_o, w_o, k_o, v_o, kk_o, b_o, g_o, prev_ref, prevl_ref):
    @pl.when(pl.program_id(1) == 0)
    def _():
        prev_ref[...] = zp_ref[...]
        prevl_ref[...] = zlp_ref[...]

    z = z_ref[...]
    zl = zl_ref[...]
    tt = z.shape[0]
    zm = z + (_shifted(z, prev_ref[...]) - z) * mu_ref[...]
    zlm = zl + (_shifted(zl, prevl_ref[...]) - zl) * mul_ref[...]
    prev_ref[...] = z[tt - 1:tt, :]
    prevl_ref[...] = zl[tt - 1:tt, :]

    r = zm[:, 0:RW_WIDTH]
    k = zm[:, RW_WIDTH:2 * RW_WIDTH]
    v = zm[:, 2 * RW_WIDTH:3 * RW_WIDTH]
    xw = w0_ref[...] + _dot(jnp.tanh(zlm).astype(BF16), w2_ref[...])
    nx = -xw
    softplus = jnp.maximum(nx, 0.0) + jnp.log(1.0 + jnp.exp(-jnp.abs(nx)))
    log_decay = -jnp.exp(-softplus - 0.5)
    a = _sigmoid(a0_ref[...] + _dot(zlm.astype(BF16), a2_ref[...]))
    g = _dot(_sigmoid(zlm).astype(BF16), g2_ref[...])
    kk = k * kkw_ref[...]
    ssq = _seg_sum(kk * kk, bones_ref[...])
    kk = kk / jnp.maximum(jnp.sqrt(ssq), 1e-12)
    r_o[...] = r
    w_o[...] = log_decay
    k_o[...] = k * (1.0 + (a - 1.0) * kaw_ref[...])
    v_o[...] = v
    kk_o[...] = kk
    b_o[...] = kk * a
    g_o[...] = g


def _rwkv_prep(z1, zl, zprev, zlprev, prm, tt):
    n, t, _ = z1.shape
    tt = _tile(t, tt)
    row = lambda c: pl.BlockSpec((1, c), lambda i, j: (0, 0))
    mat = lambda r, c: pl.BlockSpec((r, c), lambda i, j: (0, 0))
    seq = lambda c: pl.BlockSpec((None, tt, c), lambda i, j: (i, j, 0))
    outs = pl.pallas_call(
        _rwkv_prep_kernel,
        grid=(n, t // tt),
        in_specs=[pl.BlockSpec((None, tt, 3 * RW_WIDTH), lambda i, j: (i, j, 1)),
                  seq(LORA_PAD),
                  pl.BlockSpec((None, 1, 3 * RW_WIDTH), lambda i, j: (i, 0, 0)),
                  pl.BlockSpec((None, 1, LORA_PAD), lambda i, j: (i, 0, 0)),
                  row(3 * RW_WIDTH), row(LORA_PAD), row(RW_WIDTH), row(RW_WIDTH),
                  mat(LORA_PAD, RW_WIDTH), mat(LORA_PAD, RW_WIDTH), mat(LORA_PAD, RW_WIDTH),
                  row(RW_WIDTH), row(RW_WIDTH), mat(SEG_CHUNK, SEG_CHUNK)],
        out_specs=[seq(RW_WIDTH)] * 7,
        out_shape=[jax.ShapeDtypeStruct((n, t, RW_WIDTH), F32)] * 7,
        scratch_shapes=[pltpu.VMEM((1, 3 * RW_WIDTH), F32), pltpu.VMEM((1, LORA_PAD), F32)],
        compiler_params=_cparams(("parallel", "arbitrary")),
        name="rwkv_prep",
    )(z1, zl, zprev, zlprev, prm["mu"], prm["mul"], prm["w0"], prm["a0"], prm["w2"], prm["a2"],
      prm["g2"], prm["k_k"], prm["k_a"], prm["bones"])
    return outs


def _rwkv_scan_kernel(r_ref, w_ref, k_ref, v_ref, kk_ref, b_ref, s0_ref, bones_ref, eye_ref,
                      y_ref, sout_ref, s_ref, *, tb, nblk):
    t = pl.program_id(1)

    @pl.when(t == 0)
    def _():
        s_ref[...] = s0_ref[...]

    bones = bones_ref[...]
    eye = eye_ref[...]

    def bc(ref, s):
        return jnp.concatenate(
            [jnp.broadcast_to(ref[pl.ds(s, 1), c * SEG_CHUNK:(c + 1) * SEG_CHUNK], (RW_HEAD, SEG_CHUNK))
             for c in range(4)], axis=0)

    def step(s, carry):
        st = s_ref[...]
        hi, lo = _split(st * bc(kk_ref, s))
        sa = _dot(hi, bones) + _dot(lo, bones)
        vcol = _dot((bc(v_ref, s) * eye).astype(BF16), bones)
        sn = st * jnp.exp(bc(w_ref, s)) - sa * bc(b_ref, s) + vcol * bc(k_ref, s)
        s_ref[...] = sn
        yb = _dot((sn * bc(r_ref, s)).astype(BF16), bones)
        y4 = jnp.sum((yb * eye).reshape(4, RW_HEAD, SEG_CHUNK), axis=1)
        for c in range(4):
            y_ref[pl.ds(s, 1), c * SEG_CHUNK:(c + 1) * SEG_CHUNK] = y4[c:c + 1, :]
        return carry

    lax.fori_loop(0, tb, step, 0)

    @pl.when(t == nblk - 1)
    def _():
        sout_ref[...] = s_ref[...]


def _rwkv_scan(r, w, k, v, kk, b, s0, prm, tb=8):
    n, t, _ = r.shape
    tb = _tile(t, tb)
    nblk = t // tb
    seq = pl.BlockSpec((None, tb, RW_WIDTH), lambda i, j: (i, j, 0))
    st = pl.BlockSpec((None, 4 * RW_HEAD, SEG_CHUNK), lambda i, j: (i, 0, 0))
    cst = pl.BlockSpec((SEG_CHUNK, SEG_CHUNK), lambda i, j: (0, 0))
    return pl.pallas_call(
        functools.partial(_rwkv_scan_kernel, tb=tb, nblk=nblk),
        grid=(n, nblk),
        in_specs=[seq] * 6 + [st, cst, cst],
        out_specs=[seq, st],
        out_shape=[jax.ShapeDtypeStruct((n, t, RW_WIDTH), F32),
                   jax.ShapeDtypeStruct((n, 4 * RW_HEAD, SEG_CHUNK), F32)],
        scratch_shapes=[pltpu.VMEM((4 * RW_HEAD, SEG_CHUNK), F32)],
        compiler_params=_cparams(("parallel", "arbitrary")),
        name="rwkv_scan",
    )(r, w, k, v, kk, b, s0, prm["bones"], prm["eye4"])


def _rwkv_chunk_kernel(r_ref, lw_ref, k_ref, v_ref, kk_ref, b_ref, y_ref, hout_ref, h_ref, *, nchunks):
    t = pl.program_id(1)
    cs = RW_CHUNK
    n = SEG_CHUNK

    @pl.when(t == 0)
    def _():
        h_ref[...] = jnp.zeros(h_ref.shape, F32)

    row = lax.broadcasted_iota(jnp.int32, (n, n), 0)
    lane = lax.broadcasted_iota(jnp.int32, (n, n), 1)
    same_head = (row // cs) == (lane // cs)
    strict = same_head & ((lane % cs) < (row % cs))
    incl = same_head & ((lane % cs) <= (row % cs))
    eye = row == lane
    lane_head = lax.broadcasted_iota(jnp.int32, (cs, n), 1) // RW_HEAD
    row_in = lax.broadcasted_iota(jnp.int32, (cs, n), 0)

    def stack(x):
        return jnp.concatenate([jnp.where(lane_head == hl, x, 0.0) for hl in range(4)], axis=0)

    def tile4(x):
        return jnp.concatenate([x] * 4, axis=1)

    def cumsum_rows(x):
        for sh in (1, 2, 4, 8, 16, 32):
            x = x + jnp.where(row_in >= sh, pltpu.roll(x, sh, 0), 0.0)
        return x

    for c in range(4):
        sl = slice(c * n, (c + 1) * n)
        lw = lw_ref[:, sl]
        l_in = cumsum_rows(lw)
        l_end = l_in[cs - 1:cs, :]
        e_neg = jnp.exp(-l_in)
        e_tail = jnp.exp(l_end - l_in)
        kq = stack(kk_ref[:, sl] * jnp.exp(l_in - lw)).astype(BF16)
        rq = stack(r_ref[:, sl] * jnp.exp(l_in)).astype(BF16)
        k_c = k_ref[:, sl]
        b_c = b_ref[:, sl]
        kh = (k_c * e_neg).astype(BF16)
        bh = (b_c * e_neg).astype(BF16)
        a_k = jnp.where(strict, tile4(_dot_nt(kq, kh)), 0.0)
        a_b = jnp.where(strict, tile4(_dot_nt(kq, bh)), 0.0)
        r_k = jnp.where(incl, tile4(_dot_nt(rq, kh)), 0.0)
        r_b = jnp.where(incl, tile4(_dot_nt(rq, bh)), 0.0)
        pw = a_b
        inv = jnp.where(eye, 1.0, 0.0) - a_b
        for _ in range(5):
            pb = pw.astype(BF16)
            pw = _dot(pb, pb)
            inv = inv + _dot(inv.astype(BF16), pw.astype(BF16))
        h0 = h_ref[c]
        h0b = h0.astype(BF16)
        v_bd = stack(v_ref[:, sl]).astype(BF16)
        u = _dot(inv.astype(BF16), (_dot(kq, h0b) + _dot(a_k.astype(BF16), v_bd)).astype(BF16))
        ub = u.astype(BF16)
        y = _dot(rq, h0b) + _dot(r_k.astype(BF16), v_bd) - _dot(r_b.astype(BF16), ub)
        y_ref[:, sl] = y[0:cs] + y[cs:2 * cs] + y[2 * cs:3 * cs] + y[3 * cs:4 * cs]
        tail = jnp.concatenate([stack(k_c * e_tail), -stack(b_c * e_tail)], axis=0)
        upd = _dot(tail.T.astype(BF16), jnp.concatenate([v_bd, ub], axis=0))
        decay_col = jnp.sum(jnp.where(eye, jnp.broadcast_to(jnp.exp(l_end), (n, n)), 0.0), axis=1, keepdims=True)
        h_ref[c] = decay_col * h0 + upd

    @pl.when(t == nchunks - 1)
    def _():
        hout_ref[...] = h_ref[...]


def _rwkv_chunked(r, lw, k, v, kk, b):
    n, t, _ = r.shape
    assert t % RW_CHUNK == 0 and RW_CHUNK == RW_HEAD
    nchunks = t // RW_CHUNK
    seq = pl.BlockSpec((None, RW_CHUNK, RW_WIDTH), lambda i, j: (i, j, 0))
    st = pl.BlockSpec((None, 4, SEG_CHUNK, SEG_CHUNK), lambda i, j: (i, 0, 0, 0))
    y, h = pl.pallas_call(
        functools.partial(_rwkv_chunk_kernel, nchunks=nchunks),
        grid=(n, nchunks),
        in_specs=[seq] * 6,
        out_specs=[seq, st],
        out_shape=[jax.ShapeDtypeStruct((n, t, RW_WIDTH), F32),
                   jax.ShapeDtypeStruct((n, 4, SEG_CHUNK, SEG_CHUNK), F32)],
        scratch_shapes=[pltpu.VMEM((4, SEG_CHUNK, SEG_CHUNK), F32)],
        compiler_params=_cparams(("parallel", "arbitrary")),
        name="rwkv_chunked",
    )(r, lw, k, v, kk, b)
    h = h.reshape(n, 4, 4, RW_HEAD, 4, RW_HEAD)
    hd = jnp.stack([h[:, :, hl, :, hl, :] for hl in range(4)], axis=2)
    return y, jnp.swapaxes(hd, 3, 4).reshape(n, RW_HEADS, RW_HEAD, RW_HEAD)


def _rwkv_post_kernel(y_ref, r_ref, k_ref, v_ref, g_ref, rk_ref, lg_ref, lb_ref, bones_ref, o_ref):
    bones = bones_ref[...]
    y = y_ref[...]
    d = y - _seg_sum(y, bones) * (1.0 / RW_HEAD)
    var = _seg_sum(d * d, bones) * (1.0 / RW_HEAD)
    yn = d * lax.rsqrt(var + GN_EPS) * lg_ref[...] + lb_ref[...]
    bonus = _seg_sum(r_ref[...] * k_ref[...] * rk_ref[...], bones) * v_ref[...]
    o_ref[...] = ((yn + bonus) * g_ref[...]).astype(o_ref.dtype)


def _rwkv_post(y, r, k, v, g, prm, tm=256):
    m = y.shape[0]
    tm = _tile(m, tm)
    big = pl.BlockSpec((tm, RW_WIDTH), lambda i: (i, 0))
    row = pl.BlockSpec((1, RW_WIDTH), lambda i: (0, 0))
    return pl.pallas_call(
        _rwkv_post_kernel,
        grid=(m // tm,),
        in_specs=[big] * 5 + [row] * 3 + [pl.BlockSpec((SEG_CHUNK, SEG_CHUNK), lambda i: (0, 0))],
        out_specs=big,
        out_shape=jax.ShapeDtypeStruct((m, RW_WIDTH), BF16),
        compiler_params=_cparams(("parallel",)),
        name="rwkv_post",
    )(y, r, k, v, g, prm["r_k"], prm["ln_g"], prm["ln_b"], prm["bones"])


def _state_to_stacked(s):
    n = s.shape[0]
    s = s.reshape(n, 4, 4, RW_HEAD, RW_HEAD)
    return jnp.transpose(s, (0, 1, 3, 2, 4)).reshape(n, 4 * RW_HEAD, SEG_CHUNK)


def _state_from_stacked(s):
    n = s.shape[0]
    s = s.reshape(n, 4, RW_HEAD, 4, RW_HEAD)
    return jnp.transpose(s, (0, 1, 3, 2, 4)).reshape(n, RW_HEADS, RW_HEAD, RW_HEAD)


def _top3_rows(sc, idx, nvalid_mask):
    big = float(sc.shape[0])
    sc = jnp.where(nvalid_mask, sc, NEG)
    sel = jnp.zeros(sc.shape, F32)
    for _ in range(MOBA_TOPK):
        mx = jnp.max(sc, axis=0, keepdims=True)
        first = jnp.min(jnp.where(sc == mx, idx, big), axis=0, keepdims=True)
        hit = (idx == first) & (mx > 0.5 * NEG)
        sel = jnp.where(hit, 1.0, sel)
        sc = jnp.where(hit, NEG, sc)
    return sel


def _moba_prompt_kernel(q_ref, k_ref, v_ref, bown_ref, bprev_ref, bfar_ref, o_ref,
                        kb_ref, vt_ref, kmh_ref, kml_ref, sel_ref, m_ref, l_ref, acc_ref, *, nb):
    i = pl.program_id(1)
    blk = MOBA_BLOCK

    @pl.when(i == 0)
    def _():
        def prep(j, c):
            rows = pl.ds(pl.multiple_of(j * blk, blk), blk)
            kj = k_ref[rows, :]
            kb_ref[rows, :] = kj.astype(BF16)
            km = jnp.sum(kj, axis=0, keepdims=True) * (1.0 / blk)
            hi, lo = _split(km)
            kmh_ref[pl.ds(j, 1), :] = hi.astype(F32)
            kml_ref[pl.ds(j, 1), :] = lo.astype(F32)
            vt_ref[j] = v_ref[rows, :].T.astype(BF16)
            return c
        lax.fori_loop(0, nb, prep, 0)

    qt = q_ref[...].T
    qh, ql = _split(qt)
    kmh = kmh_ref[...].astype(BF16)
    kml = kml_ref[...].astype(BF16)
    sc = _dot(kmh, qh) + _dot(kmh, ql) + _dot(kml, qh)
    bidx = lax.broadcasted_iota(jnp.int32, sc.shape, 0)
    sel_ref[...] = _top3_rows(sc, bidx.astype(F32), bidx < i)

    qs = (qt * (HEAD_DIM ** -0.5)).astype(BF16)

    def keys(j, n):
        return kb_ref[pl.ds(pl.multiple_of(j * blk, blk), n * blk), :]

    def values_t(j, n):
        return vt_ref[j] if n == 1 else jnp.concatenate([vt_ref[j], vt_ref[j + 1]], axis=1)

    def picked(j):
        return jnp.broadcast_to(sel_ref[pl.ds(j, 1), :], (blk, blk)) > 0.5

    kidx = lax.broadcasted_iota(jnp.int32, (blk, blk), 0)
    qidx = lax.broadcasted_iota(jnp.int32, (blk, blk), 1)
    causal = kidx <= qidx

    def start(s, vt):
        m0 = jnp.max(s, axis=0, keepdims=True)
        p = jnp.exp(s - m0)
        m_ref[...] = m0
        l_ref[...] = jnp.sum(p, axis=0, keepdims=True)
        acc_ref[...] = _dot(vt, p.astype(BF16))

    def update(s, vt):
        m_old = m_ref[...]
        m_new = jnp.maximum(m_old, jnp.max(s, axis=0, keepdims=True))
        a = jnp.exp(m_old - m_new)
        p = jnp.exp(s - m_new)
        l_ref[...] = a * l_ref[...] + jnp.sum(p, axis=0, keepdims=True)
        acc_ref[...] = a * acc_ref[...] + _dot(vt, p.astype(BF16))
        m_ref[...] = m_new

    @pl.when(i == 0)
    def _():
        start(jnp.where(causal, _dot(keys(i, 1), qs) + bown_ref[...], NEG), vt_ref[i])

    @pl.when(i >= 1)
    def _():
        s = _dot(keys(i - 1, 2), qs) + jnp.concatenate([bprev_ref[...], bown_ref[...]], axis=0)
        keep = jnp.concatenate([picked(i - 1), causal], axis=0)
        start(jnp.where(keep, s, NEG), values_t(i - 1, 2))

    n_far = jnp.maximum(i - 1, 0)
    bfar = bfar_ref[0:1, 0:1]

    def far_pair(jj, c):
        j = 2 * jj
        keep = jnp.concatenate([picked(j), picked(j + 1)], axis=0)
        update(jnp.where(keep, _dot(keys(j, 2), qs) + bfar, NEG), values_t(j, 2))
        return c
    lax.fori_loop(0, n_far // 2, far_pair, 0)

    @pl.when(n_far % 2 == 1)
    def _():
        j = n_far - 1
        update(jnp.where(picked(j), _dot(keys(j, 1), qs) + bfar, NEG), vt_ref[j])

    o_ref[...] = (acc_ref[...] / l_ref[...]).T.astype(o_ref.dtype)


def _moba_prompt(z1, bias):
    t = z1.shape[0]
    assert t % MOBA_BLOCK == 0
    nb = t // MOBA_BLOCK
    blk = MOBA_BLOCK
    h8 = ATTN_HEADS
    tile = pl.BlockSpec((None, blk, blk), lambda h, i: (h, 0, 0))
    return pl.pallas_call(
        functools.partial(_moba_prompt_kernel, nb=nb),
        grid=(h8, nb),
        in_specs=[pl.BlockSpec((blk, HEAD_DIM), lambda h, i: (i, h)),
                  pl.BlockSpec((t, HEAD_DIM), lambda h, i: (0, h8 + h)),
                  pl.BlockSpec((t, HEAD_DIM), lambda h, i: (0, 2 * h8 + h)),
                  tile, tile,
                  pl.BlockSpec((None, 1, HEAD_DIM), lambda h, i: (h, 0, 0))],
        out_specs=pl.BlockSpec((blk, HEAD_DIM), lambda h, i: (i, h)),
        out_shape=jax.ShapeDtypeStruct((t, ATTN_WIDTH), BF16),
        scratch_shapes=[pltpu.VMEM((t, HEAD_DIM), BF16),
                        pltpu.VMEM((nb, HEAD_DIM, blk), BF16),
                        pltpu.VMEM((nb, HEAD_DIM), F32),
                        pltpu.VMEM((nb, HEAD_DIM), F32),
                        pltpu.VMEM((nb, blk), F32),
                        pltpu.VMEM((1, blk), F32),
                        pltpu.VMEM((1, blk), F32),
                        pltpu.VMEM((HEAD_DIM, blk), F32)],
        compiler_params=_cparams(("arbitrary", "arbitrary")),
        name="moba_prompt",
    )(z1, z1, z1, bias["own_t"], bias["prev_t"], bias["far"])


def _diag_extract(s):
    lane = lax.broadcasted_iota(jnp.int32, (ATTN_HEADS, s.shape[1]), 1) % ATTN_HEADS
    out = jnp.zeros((ATTN_HEADS, s.shape[1]), F32)
    for hp in range(ATTN_HEADS):
        out = out + jnp.where(lane == hp, s[hp * 8:(hp + 1) * 8, :], 0.0)
    return out


def _diag_expand(p):
    lane = lax.broadcasted_iota(jnp.int32, p.shape, 1) % ATTN_HEADS
    return jnp.concatenate([jnp.where(lane == hp, p, 0.0) for hp in range(ATTN_HEADS)], axis=0)


def _class_allreduce(x, op):
    for sh in (8, 16, 32, 64):
        x = op(x, pltpu.roll(x, sh, 1))
    return x


def _fold_tiles(x, op):
    out = x[:, 0:128]
    for c in range(1, x.shape[1] // 128):
        out = op(out, x[:, c * 128:(c + 1) * 128])
    return out


def _moba_sample_kernel(pt_ref, q_ref, *refs, nblk, grp):
    k_refs = refs[:2 * grp]
    v_refs = refs[2 * grp:4 * grp]
    (kn_ref, vn_ref, blast_ref, bown_ref, bfar_ref, o_ref,
     sc_ref, bsum_ref, sel_ref, m_ref, li_ref, acc_ref) = refs[4 * grp:]
    ph = pl.program_id(1)
    step = pl.program_id(2)
    nsteps = nblk // grp
    ntok = 8
    rows = PAGE_SIZE * ATTN_HEADS
    nkeys = 2 * rows
    far16 = jnp.concatenate([bfar_ref[...]] * (nkeys // 128), axis=1)

    def block_bias(j):
        return jnp.where(j == nblk - 1, blast_ref[...], far16)

    def tile16(x):
        return jnp.concatenate([x] * (nkeys // 128), axis=1)

    @pl.when(ph == 0)
    def _():
        qs = (q_ref[...] * (HEAD_DIM ** -0.5)).astype(BF16)
        for g in range(grp):
            j = step * grp + g
            k0 = k_refs[2 * g][...]
            k1 = k_refs[2 * g + 1][...]
            bsum_ref[pl.ds(pl.multiple_of(j * 8, 8), 8), :] = jnp.sum(k0, axis=0) + jnp.sum(k1, axis=0)
            k2 = jnp.concatenate([k0.reshape(rows, HEAD_DIM), k1.reshape(rows, HEAD_DIM)], axis=0).astype(BF16)
            sc_ref[j] = _diag_extract(_dot_nt(qs, k2))

    @pl.when((ph == 0) & (step == nsteps - 1))
    def _():
        q = q_ref[...]
        qh, ql = _split(q)
        bh, bl = _split(bsum_ref[...] * (1.0 / MOBA_BLOCK))
        scx = _diag_extract(_dot_nt(qh, bh) + _dot_nt(qh, bl) + _dot_nt(ql, bh))
        width = nblk * 8
        jidx = (lax.broadcasted_iota(jnp.int32, (ntok, width), 1) // 8).astype(F32)

        def creduce(x, op):
            y = _class_allreduce(_fold_tiles(x, op), op)
            return jnp.concatenate([y] * (width // 128), axis=1)

        selx = jnp.zeros((ntok, width), F32)
        for _ in range(MOBA_TOPK):
            mx = creduce(scx, jnp.maximum)
            first = creduce(jnp.where(scx == mx, jidx, float(nblk)), jnp.minimum)
            hit = jidx == first
            selx = jnp.where(hit, 1.0, selx)
            scx = jnp.where(hit, NEG, scx)
        selb = selx.astype(BF16)
        erow = lax.broadcasted_iota(jnp.int32, (width, 128), 0)
        ecls = lax.broadcasted_iota(jnp.int32, (width, 128), 1) % 8

        def spread(j, c):
            e = jnp.where(erow == j * 8 + ecls, 1.0, 0.0).astype(BF16)
            sel_ref[j] = _dot(selb, e)
            return c
        lax.fori_loop(0, nblk, spread, 0)

        qs = (q * (HEAD_DIM ** -0.5)).astype(BF16)
        so = _diag_extract(_dot_nt(qs, kn_ref[...].astype(BF16))) + bown_ref[...]
        tq = lax.broadcasted_iota(jnp.int32, so.shape, 0)
        tk = lax.broadcasted_iota(jnp.int32, so.shape, 1) // 8
        so = jnp.concatenate([jnp.where(tk <= tq, so, NEG), jnp.full((ntok, 64), NEG, F32)], axis=1)

        def masked(j):
            x = sc_ref[j] + block_bias(j)
            return jnp.where(tile16(sel_ref[j]) > 0.5, x, NEG)

        def mx_body(j, m):
            return jnp.maximum(m, _fold_tiles(masked(j), jnp.maximum))
        m = lax.fori_loop(0, nblk, mx_body, so)
        m = _class_allreduce(m, jnp.maximum)
        m16 = tile16(m)

        def sum_body(j, l):
            return l + _fold_tiles(jnp.exp(masked(j) - m16), jnp.add)
        po = jnp.exp(so - m)
        l = lax.fori_loop(0, nblk, sum_body, po)
        li = 1.0 / _class_allreduce(l, jnp.add)
        m_ref[...] = m
        li_ref[...] = li
        pfull = _diag_expand((po * li)[:, 0:64]).astype(BF16)
        acc_ref[...] = _dot(pfull, vn_ref[...].astype(BF16))

    @pl.when(ph == 1)
    def _():
        m16 = tile16(m_ref[...])
        li16 = tile16(li_ref[...])
        acc = acc_ref[...]
        for g in range(grp):
            j = step * grp + g
            p = jnp.exp(sc_ref[j] + block_bias(j) - m16) * li16
            p = jnp.where(tile16(sel_ref[j]) > 0.5, p, 0.0)
            v2 = jnp.concatenate([v_refs[2 * g][...].reshape(rows, HEAD_DIM),
                                  v_refs[2 * g + 1][...].reshape(rows, HEAD_DIM)], axis=0).astype(BF16)
            acc = acc + _dot(_diag_expand(p).astype(BF16), v2)
        acc_ref[...] = acc

    @pl.when((ph == 1) & (step == nsteps - 1))
    def _():
        o_ref[...] = acc_ref[...]


def _moba_sample(qht, knew, vnew, cache_k, cache_v, page_table, bias):
    nb_, n_pages = page_table.shape
    assert n_pages % PAGES_PER_BLOCK == 0 and PAGES_PER_BLOCK == 2
    nblk = n_pages // 2
    assert nblk >= MOBA_TOPK and (nblk * 8) % 128 == 0
    nkeys = 2 * PAGE_SIZE * ATTN_HEADS
    grp = SAMPLE_BLOCKS_PER_STEP
    assert nblk % grp == 0
    nsteps = nblk // grp

    def kmap(u):
        return lambda b, p, j, pt: (0, pt[b, 2 * grp * jnp.where(p == 0, j, nsteps - 1) + u], 0, 0, 0)

    def vmap_(u):
        return lambda b, p, j, pt: (0, pt[b, 2 * grp * jnp.where(p == 0, 0, j) + u], 0, 0, 0)

    page = (None, None, PAGE_SIZE, ATTN_HEADS, HEAD_DIM)
    per_b = pl.BlockSpec((None, 64, HEAD_DIM), lambda b, p, j, pt: (b, 0, 0))
    cst = lambda r, c: pl.BlockSpec((r, c), lambda b, p, j, pt: (0, 0))
    grid_spec = pltpu.PrefetchScalarGridSpec(
        num_scalar_prefetch=1,
        grid=(nb_, 2, nsteps),
        in_specs=[per_b]
                 + [pl.BlockSpec(page, kmap(u)) for u in range(2 * grp)]
                 + [pl.BlockSpec(page, vmap_(u)) for u in range(2 * grp)]
                 + [per_b, per_b, cst(8, nkeys), cst(8, 64), cst(1, 128)],
        out_specs=per_b,
        scratch_shapes=[pltpu.VMEM((nblk, 8, nkeys), F32),
                        pltpu.VMEM((nblk * 8, HEAD_DIM), F32),
                        pltpu.VMEM((nblk, 8, 128), F32),
                        pltpu.VMEM((8, 128), F32),
                        pltpu.VMEM((8, 128), F32),
                        pltpu.VMEM((64, HEAD_DIM), F32)])
    return pl.pallas_call(
        functools.partial(_moba_sample_kernel, nblk=nblk, grp=grp),
        grid_spec=grid_spec,
        out_shape=jax.ShapeDtypeStruct((nb_, 64, HEAD_DIM), F32),
        compiler_params=_cparams(("arbitrary", "arbitrary", "arbitrary")),
        name="moba_sample",
    )(page_table, qht, *([cache_k] * (2 * grp)), *([cache_v] * (2 * grp)), knew, vnew,
      bias["last_s"], bias["own_s"], bias["far_s"])


def _bias_of_distance(dist, rel_bias):
    dist = jnp.maximum(dist, 0)
    exact = N_BUCKETS // 2
    log_ratio = jnp.log(jnp.maximum(dist, 1).astype(F32) / exact) / math.log(MAX_DISTANCE / exact)
    large = jnp.minimum(exact + (log_ratio * (N_BUCKETS - exact)).astype(jnp.int32), N_BUCKETS - 1)
    return rel_bias[jnp.where(dist < exact, dist, large)]


def _bias_tiles(rel_bias, past_len, dec_seq):
    blk = MOBA_BLOCK
    val = _bias_of_distance(jnp.arange(2 * blk), rel_bias)
    val_t = val.T

    def toeplitz(ext):
        n2 = ext.shape[1]
        n = n2 // 2
        return jnp.tile(ext, (1, n))[:, :n * (n2 - 1)].reshape(ext.shape[0], n, n2 - 1)[:, :, :n]

    own_t = toeplitz(val_t)
    prev_t = toeplitz(jnp.roll(val_t, -blk, axis=1))
    far_row = val[2 * blk - 1]
    far = jnp.broadcast_to(far_row[:, None, None], (ATTN_HEADS, 1, HEAD_DIM))
    assert past_len % blk == 0 and 2 * blk - 1 >= MAX_DISTANCE and dec_seq < blk
    last = jnp.stack([val[t + 1:t + 1 + blk][::-1] for t in range(dec_seq)])
    last_s = last.reshape(dec_seq, blk * ATTN_HEADS)
    own = jnp.stack([jnp.concatenate([val[:t + 1][::-1], jnp.zeros((dec_seq - 1 - t, ATTN_HEADS), F32)])
                     for t in range(dec_seq)])
    own_s = own.reshape(dec_seq, dec_seq * ATTN_HEADS)
    far_s = jnp.tile(far_row, 128 // ATTN_HEADS)[None, :]
    return dict(own_t=own_t, prev_t=prev_t, far=far, last_s=last_s, own_s=own_s, far_s=far_s)


def _np_consts():
    seg = np.arange(SEG_CHUNK) // RW_HEAD
    bones = (seg[:, None] == seg[None, :]).astype(np.float32)
    idx = np.arange(SEG_CHUNK) % RW_HEAD
    eye4 = (idx[:, None] == idx[None, :]).astype(np.float32)
    return bones, eye4


def _pad_rows(w, row0, total):
    return jnp.zeros((total, w.shape[1]), BF16).at[row0:row0 + w.shape[0]].set(w.astype(BF16))


def kernel(x_prompt, x_sample, cache_k, cache_v, page_table, state_wkv, state_shift, w_in, rel_bias, rw_mu, rw_w0, rw_w2, rw_a0, rw_a2, rw_g2, rw_k_k, rw_k_a, rw_r_k, rw_ln_g, rw_ln_b, w_up_attn, w_up_rwkv, w_o, ln1_g, ln1_b, w_ffn_gate, w_ffn_up, w_ffn_down, ln2_g, ln2_b):
    assert x_prompt.shape[0] == 1 and w_in.shape[0] == DEPTH == 1
    t_p = x_prompt.shape[1]
    nb_s, t_s, _ = x_sample.shape
    assert t_s == 8
    past_len = page_table.shape[1] * PAGE_SIZE

    w_in2 = w_in[0]
    w_lora = jnp.pad(w_in2[:, LORA_COL:GATE_COL], ((0, 0), (0, LORA_PAD - LORA_COLS)))
    w_gate = w_in2[:, GATE_COL:]
    bones_np, eye_np = _np_consts()
    mu = rw_mu[0]
    prm = dict(
        mu=mu[None, :3 * RW_WIDTH],
        mul=jnp.pad(mu[3 * RW_WIDTH:], (0, LORA_PAD - LORA_COLS))[None, :],
        w0=rw_w0, a0=rw_a0,
        w2=_pad_rows(rw_w2[0], 0, LORA_PAD),
        a2=_pad_rows(rw_a2[0], DECAY_LORA, LORA_PAD),
        g2=_pad_rows(rw_g2[0], DECAY_LORA + AAA_LORA, LORA_PAD),
        k_k=rw_k_k, k_a=rw_k_a, r_k=rw_r_k.reshape(1, RW_WIDTH), ln_g=rw_ln_g, ln_b=rw_ln_b,
        bones=jnp.asarray(bones_np, BF16), eye4=jnp.asarray(eye_np, F32))
    bias = _bias_tiles(rel_bias, past_len, t_s)

    def group(x2d, nseq, tseq, zprev, zlprev, s0, attend, tt, tb):
        z1 = _proj(x2d, w_in2, 0, MAIN_COLS)
        zl = _proj(x2d, w_lora, 0, LORA_PAD)
        zg = _proj(x2d, w_gate, 0, 2 * D_MODEL)
        o_attn = attend(z1)
        r, w, k, v, kk, b, g = _rwkv_prep(z1.reshape(nseq, tseq, MAIN_COLS), zl.reshape(nseq, tseq, LORA_PAD),
                                          zprev, zlprev, prm, tt)
        if s0 is None:
            y, s_fin = _rwkv_chunked(r, w, k, v, kk, b)
        else:
            y, s_fin = _rwkv_scan(r, w, k, v, kk, b, s0, prm, tb)
            s_fin = _state_from_stacked(s_fin)
        flat = lambda u: u.reshape(nseq * tseq, RW_WIDTH)
        o_rwkv = _rwkv_post(flat(y), flat(r), flat(k), flat(v), flat(g), prm)
        mixed = _merge(o_attn, o_rwkv, zg, w_up_attn[0], w_up_rwkv[0])
        h = _proj_ln(mixed, w_o[0], x2d, ln1_g, ln1_b)
        act = _ffn_up(h, w_ffn_gate[0], w_ffn_up[0])
        out = _proj_ln(act, w_ffn_down[0], h, ln2_g, ln2_b)
        new_k = z1[:, ATTN_WIDTH:2 * ATTN_WIDTH]
        new_v = z1[:, 2 * ATTN_WIDTH:3 * ATTN_WIDTH]
        return out, new_k, new_v, s_fin

    xp = x_prompt[0]
    yp, kp, vp, sp = group(
        xp, 1, t_p,
        jnp.zeros((1, 1, 3 * RW_WIDTH), F32), jnp.zeros((1, 1, LORA_PAD), F32),
        None, lambda z1: _moba_prompt(z1, bias), 256, 8)

    xs = x_sample.reshape(nb_s * t_s, D_MODEL)
    sh = state_shift[0]
    zprev = _proj(sh, w_in2, RKV_COL, 3 * RW_WIDTH)[:, None, :]
    zlprev = _proj(sh, w_lora, 0, LORA_PAD)[:, None, :]

    def attend_sample(z1):
        q = z1[:, 0:ATTN_WIDTH].reshape(nb_s, t_s, ATTN_HEADS, HEAD_DIM)
        qht = jnp.transpose(q, (0, 2, 1, 3)).reshape(nb_s, ATTN_HEADS * t_s, HEAD_DIM)
        kn = z1[:, ATTN_WIDTH:2 * ATTN_WIDTH].reshape(nb_s, t_s * ATTN_HEADS, HEAD_DIM)
        vn = z1[:, 2 * ATTN_WIDTH:3 * ATTN_WIDTH].reshape(nb_s, t_s * ATTN_HEADS, HEAD_DIM)
        o = _moba_sample(qht, kn, vn, cache_k, cache_v, page_table, bias)
        o = jnp.transpose(o.reshape(nb_s, ATTN_HEADS, t_s, HEAD_DIM), (0, 2, 1, 3))
        return o.reshape(nb_s * t_s, ATTN_WIDTH).astype(BF16)

    ys, ks, vs, ss = group(xs, nb_s, t_s, zprev, zlprev, _state_to_stacked(state_wkv[0]),
                           attend_sample, 8, 8)

    return (yp[None], ys.reshape(nb_s, t_s, D_MODEL),
            kp.reshape(1, 1, t_p, ATTN_HEADS, HEAD_DIM), vp.reshape(1, 1, t_p, ATTN_HEADS, HEAD_DIM),
            sp[None], xp[None, -1:, :],
            ks.reshape(1, nb_s, t_s, ATTN_HEADS, HEAD_DIM), vs.reshape(1, nb_s, t_s, ATTN_HEADS, HEAD_DIM),
            ss[None], x_sample[None, :, -1, :])
```

```python
import functools
import math

import numpy as np
import jax
import jax.numpy as jnp
from jax import lax
from jax.experimental import pallas as pl
from jax.experimental.pallas import tpu as pltpu

F32 = jnp.float32
BF16 = jnp.bfloat16

D_MODEL = 2048
HEAD_DIM = 128
ATTN_HEADS = D_MODEL // 256
ATTN_WIDTH = ATTN_HEADS * HEAD_DIM
MOBA_BLOCK = 256
MOBA_TOPK = 3
N_BUCKETS = 32
MAX_DISTANCE = 128
PAGE_SIZE = 128
RW_HEAD = 64
RW_HEADS = D_MODEL // 128
RW_WIDTH = RW_HEADS * RW_HEAD
DECAY_LORA = 96
AAA_LORA = 96
GATE_LORA = 256
LORA_COLS = DECAY_LORA + AAA_LORA + GATE_LORA
LORA_PAD = 512
GN_EPS = RW_HEAD * 1e-5
D_FF = 5632
LN_EPS = 1e-5
LN_ROWS = 128
DEPTH = 1
DEEPNORM_ALPHA = (2 * DEPTH) ** 0.25

RKV_COL = 3 * ATTN_WIDTH
LORA_COL = RKV_COL + 3 * RW_WIDTH
GATE_COL = LORA_COL + LORA_COLS
MAIN_COLS = LORA_COL

NEG = -1e30
SEG_CHUNK = 256
PAGES_PER_BLOCK = MOBA_BLOCK // PAGE_SIZE
SAMPLE_BLOCKS_PER_STEP = 4
RW_CHUNK = 64
VMEM_LIMIT = 56 * 1024 * 1024


def _cparams(sem):
    return pltpu.CompilerParams(dimension_semantics=sem, vmem_limit_bytes=VMEM_LIMIT)


def _dot(a, b):
    return jnp.dot(a, b, preferred_element_type=F32)


def _dot_nt(a, b):
    return lax.dot_general(a, b, (((1,), (1,)), ((), ())), preferred_element_type=F32)


def _split(x):
    hi = x.astype(BF16)
    lo = (x - hi.astype(F32)).astype(BF16)
    return hi, lo


def _sigmoid(x):
    return 1.0 / (1.0 + jnp.exp(-x))


def _tile(m, pref):
    t = min(m, pref)
    assert m % t == 0, (m, pref)
    return t


def _proj_kernel(x_ref, w_ref, o_ref, xb_ref):
    @pl.when(pl.program_id(1) == 0)
    def _():
        xb_ref[...] = x_ref[...].astype(BF16)

    o_ref[...] = _dot(xb_ref[...], w_ref[...].astype(BF16)).astype(o_ref.dtype)


def _proj(x, w, col0, ncols, tm=1024, tn=512):
    m, k = x.shape
    tm = _tile(m, tm)
    tn = _tile(ncols, tn)
    assert col0 % tn == 0
    c0 = col0 // tn
    return pl.pallas_call(
        _proj_kernel,
        grid=(m // tm, ncols // tn),
        in_specs=[pl.BlockSpec((tm, k), lambda i, j: (i, 0)),
                  pl.BlockSpec((k, tn), lambda i, j: (0, c0 + j))],
        out_specs=pl.BlockSpec((tm, tn), lambda i, j: (i, j)),
        out_shape=jax.ShapeDtypeStruct((m, ncols), F32),
        scratch_shapes=[pltpu.VMEM((tm, k), BF16)],
        compiler_params=_cparams(("parallel", "arbitrary")),
        name="proj",
    )(x, w)


def _merge_kernel(oa_ref, or_ref, wa_ref, wr_ref, ga_ref, gr_ref, o_ref):
    a = _dot(oa_ref[...], wa_ref[...].astype(BF16))
    r = _dot(or_ref[...], wr_ref[...].astype(BF16))
    o_ref[...] = (_sigmoid(ga_ref[...]) * a + _sigmoid(gr_ref[...]) * r).astype(o_ref.dtype)


def _merge(o_attn, o_rwkv, zg, w_up_attn, w_up_rwkv, tm=1024, tn=512):
    m = o_attn.shape[0]
    tm = _tile(m, tm)
    nj = D_MODEL // tn
    return pl.pallas_call(
        _merge_kernel,
        grid=(m // tm, nj),
        in_specs=[pl.BlockSpec((tm, ATTN_WIDTH), lambda i, j: (i, 0)),
                  pl.BlockSpec((tm, RW_WIDTH), lambda i, j: (i, 0)),
                  pl.BlockSpec((ATTN_WIDTH, tn), lambda i, j: (0, j)),
                  pl.BlockSpec((RW_WIDTH, tn), lambda i, j: (0, j)),
                  pl.BlockSpec((tm, tn), lambda i, j: (i, j)),
                  pl.BlockSpec((tm, tn), lambda i, j: (i, j + nj))],
        out_specs=pl.BlockSpec((tm, tn), lambda i, j: (i, j)),
        out_shape=jax.ShapeDtypeStruct((m, D_MODEL), BF16),
        compiler_params=_cparams(("parallel", "arbitrary")),
        name="merge",
    )(o_attn, o_rwkv, w_up_attn, w_up_rwkv, zg, zg)


def _proj_ln_kernel(m_ref, w_ref, x_ref, g_ref, b_ref, o_ref, *, tn, nj):
    j = pl.program_id(1)
    col = pl.multiple_of(j * tn, tn)
    mb = m_ref[...].astype(BF16)
    o_ref[:, pl.ds(col, tn)] = DEEPNORM_ALPHA * x_ref[...] + _dot(mb, w_ref[...].astype(BF16))

    @pl.when(j == nj - 1)
    def _():
        ln_rows = min(LN_ROWS, o_ref.shape[0])
        assert o_ref.shape[0] % ln_rows == 0

        def norm_rows(c, carry):
            rows = pl.ds(pl.multiple_of(c * ln_rows, ln_rows), ln_rows)
            y = o_ref[rows, :]
            mu = jnp.mean(y, axis=-1, keepdims=True)
            d = y - mu
            var = jnp.mean(d * d, axis=-1, keepdims=True)
            o_ref[rows, :] = d * lax.rsqrt(var + LN_EPS) * g_ref[...] + b_ref[...]
            return carry
        lax.fori_loop(0, o_ref.shape[0] // ln_rows, norm_rows, 0)


def _proj_ln(mat, w, x, g, b, tm=512, tn=256):
    m, k = mat.shape
    tm = _tile(m, tm)
    nj = D_MODEL // tn
    return pl.pallas_call(
        functools.partial(_proj_ln_kernel, tn=tn, nj=nj),
        grid=(m // tm, nj),
        in_specs=[pl.BlockSpec((tm, k), lambda i, j: (i, 0)),
                  pl.BlockSpec((k, tn), lambda i, j: (0, j)),
                  pl.BlockSpec((tm, tn), lambda i, j: (i, j)),
                  pl.BlockSpec((1, D_MODEL), lambda i, j: (0, 0)),
                  pl.BlockSpec((1, D_MODEL), lambda i, j: (0, 0))],
        out_specs=pl.BlockSpec((tm, D_MODEL), lambda i, j: (i, 0)),
        out_shape=jax.ShapeDtypeStruct((m, D_MODEL), F32),
        compiler_params=_cparams(("parallel", "arbitrary")),
        name="proj_ln",
    )(mat, w, x, g, b)


def _ffn_up_kernel(h_ref, wg_ref, wu_ref, o_ref, hb_ref):
    @pl.when(pl.program_id(1) == 0)
    def _():
        hb_ref[...] = h_ref[...].astype(BF16)

    hb = hb_ref[...]
    a = _dot(hb, wg_ref[...].astype(BF16))
    u = _dot(hb, wu_ref[...].astype(BF16))
    o_ref[...] = (a * _sigmoid(a) * u).astype(o_ref.dtype)


def _ffn_up(h, wg, wu, tm=1024, tn=256):
    m = h.shape[0]
    tm = _tile(m, tm)
    return pl.pallas_call(
        _ffn_up_kernel,
        grid=(m // tm, D_FF // tn),
        in_specs=[pl.BlockSpec((tm, D_MODEL), lambda i, j: (i, 0)),
                  pl.BlockSpec((D_MODEL, tn), lambda i, j: (0, j)),
                  pl.BlockSpec((D_MODEL, tn), lambda i, j: (0, j))],
        out_specs=pl.BlockSpec((tm, tn), lambda i, j: (i, j)),
        out_shape=jax.ShapeDtypeStruct((m, D_FF), BF16),
        scratch_shapes=[pltpu.VMEM((tm, D_MODEL), BF16)],
        compiler_params=_cparams(("parallel", "arbitrary")),
        name="ffn_up",
    )(h, wg, wu)


def _seg_sum(x, bones):
    outs = []
    for c in range(x.shape[1] // SEG_CHUNK):
        hi, lo = _split(x[:, c * SEG_CHUNK:(c + 1) * SEG_CHUNK])
        outs.append(_dot(hi, bones) + _dot(lo, bones))
    return jnp.concatenate(outs, axis=1)


def _shifted(z, prev):
    zs = pltpu.roll(z, 1, 0)
    row = lax.broadcasted_iota(jnp.int32, z.shape, 0)
    return jnp.where(row == 0, prev, zs)


def _rwkv_prep_kernel(z_ref, zl_ref, zp_ref, zlp_ref, mu_ref, mul_ref, w0_ref, a0_ref, w2_ref, a2_ref,
                      g2_ref, kkw_ref, kaw_ref, bones_ref,
                      r_o, w_o, k_o, v_o, kk_o, b_o, g_o, prev_ref, prevl_ref):
    @pl.when(pl.program_id(1) == 0)
    def _():
        prev_ref[...] = zp_ref[...]
        prevl_ref[...] = zlp_ref[...]

    z = z_ref[...]
    zl = zl_ref[...]
    tt = z.shape[0]
    zm = z + (_shifted(z, prev_ref[...]) - z) * mu_ref[...]
    zlm = zl + (_shifted(zl, prevl_ref[...]) - zl) * mul_ref[...]
    prev_ref[...] = z[tt - 1:tt, :]
    prevl_ref[...] = zl[tt - 1:tt, :]

    r = zm[:, 0:RW_WIDTH]
    k = zm[:, RW_WIDTH:2 * RW_WIDTH]
    v = zm[:, 2 * RW_WIDTH:3 * RW_WIDTH]
    xw = w0_ref[...] + _dot(jnp.tanh(zlm).astype(BF16), w2_ref[...])
    nx = -xw
    softplus = jnp.maximum(nx, 0.0) + jnp.log(1.0 + jnp.exp(-jnp.abs(nx)))
    log_decay = -jnp.exp(-softplus - 0.5)
    a = _sigmoid(a0_ref[...] + _dot(zlm.astype(BF16), a2_ref[...]))
    g = _dot(_sigmoid(zlm).astype(BF16), g2_ref[...])
    kk = k * kkw_ref[...]
    ssq = _seg_sum(kk * kk, bones_ref[...])
    kk = kk / jnp.maximum(jnp.sqrt(ssq), 1e-12)
    r_o[...] = r
    w_o[...] = log_decay
    k_o[...] = k * (1.0 + (a - 1.0) * kaw_ref[...])
    v_o[...] = v
    kk_o[...] = kk
    b_o[...] = kk * a
    g_o[...] = g


def _rwkv_prep(z1, zl, zprev, zlprev, prm, tt):
    n, t, _ = z1.shape
    tt = _tile(t, tt)
    row = lambda c: pl.BlockSpec((1, c), lambda i, j: (0, 0))
    mat = lambda r, c: pl.BlockSpec((r, c), lambda i, j: (0, 0))
    seq = lambda c: pl.BlockSpec((None, tt, c), lambda i, j: (i, j, 0))
    outs = pl.pallas_call(
        _rwkv_prep_kernel,
        grid=(n, t // tt),
        in_specs=[pl.BlockSpec((None, tt, 3 * RW_WIDTH), lambda i, j: (i, j, 1)),
                  seq(LORA_PAD),
                  pl.BlockSpec((None, 1, 3 * RW_WIDTH), lambda i, j: (i, 0, 0)),
                  pl.BlockSpec((None, 1, LORA_PAD), lambda i, j: (i, 0, 0)),
                  row(3 * RW_WIDTH), row(LORA_PAD), row(RW_WIDTH), row(RW_WIDTH),
                  mat(LORA_PAD, RW_WIDTH), mat(LORA_PAD, RW_WIDTH), mat(LORA_PAD, RW_WIDTH),
                  row(RW_WIDTH), row(RW_WIDTH), mat(SEG_CHUNK, SEG_CHUNK)],
        out_specs=[seq(RW_WIDTH)] * 7,
        out_shape=[jax.ShapeDtypeStruct((n, t, RW_WIDTH), F32)] * 7,
        scratch_shapes=[pltpu.VMEM((1, 3 * RW_WIDTH), F32), pltpu.VMEM((1, LORA_PAD), F32)],
        compiler_params=_cparams(("parallel", "arbitrary")),
        name="rwkv_prep",
    )(z1, zl, zprev, zlprev, prm["mu"], prm["mul"], prm["w0"], prm["a0"], prm["w2"], prm["a2"],
      prm["g2"], prm["k_k"], prm["k_a"], prm["bones"])
    return outs


def _rwkv_scan_kernel(r_ref, w_ref, k_ref, v_ref, kk_ref, b_ref, s0_ref, bones_ref, eye_ref,
                      y_ref, sout_ref, s_ref, *, tb, nblk):
    t = pl.program_id(1)

    @pl.when(t == 0)
    def _():
        s_ref[...] = s0_ref[...]

    bones = bones_ref[...]
    eye = eye_ref[...]

    def bc(ref, s):
        return jnp.concatenate(
            [jnp.broadcast_to(ref[pl.ds(s, 1), c * SEG_CHUNK:(c + 1) * SEG_CHUNK], (RW_HEAD, SEG_CHUNK))
             for c in range(4)], axis=0)

    def step(s, carry):
        st = s_ref[...]
        hi, lo = _split(st * bc(kk_ref, s))
        sa = _dot(hi, bones) + _dot(lo, bones)
        vcol = _dot((bc(v_ref, s) * eye).astype(BF16), bones)
        sn = st * jnp.exp(bc(w_ref, s)) - sa * bc(b_ref, s) + vcol * bc(k_ref, s)
        s_ref[...] = sn
        yb = _dot((sn * bc(r_ref, s)).astype(BF16), bones)
        y4 = jnp.sum((yb * eye).reshape(4, RW_HEAD, SEG_CHUNK), axis=1)
        for c in range(4):
            y_ref[pl.ds(s, 1), c * SEG_CHUNK:(c + 1) * SEG_CHUNK] = y4[c:c + 1, :]
        return carry

    lax.fori_loop(0, tb, step, 0)

    @pl.when(t == nblk - 1)
    def _():
        sout_ref[...] = s_ref[...]


def _rwkv_scan(r, w, k, v, kk, b, s0, prm, tb=8):
    n, t, _ = r.shape
    tb = _tile(t, tb)
    nblk = t // tb
    seq = pl.BlockSpec((None, tb, RW_WIDTH), lambda i, j: (i, j, 0))
    st = pl.BlockSpec((None, 4 * RW_HEAD, SEG_CHUNK), lambda i, j: (i, 0, 0))
    cst = pl.BlockSpec((SEG_CHUNK, SEG_CHUNK), lambda i, j: (0, 0))
    return pl.pallas_call(
        functools.partial(_rwkv_scan_kernel, tb=tb, nblk=nblk),
        grid=(n, nblk),
        in_specs=[seq] * 6 + [st, cst, cst],
        out_specs=[seq, st],
        out_shape=[jax.ShapeDtypeStruct((n, t, RW_WIDTH), F32),
                   jax.ShapeDtypeStruct((n, 4 * RW_HEAD, SEG_CHUNK), F32)],
        scratch_shapes=[pltpu.VMEM((4 * RW_HEAD, SEG_CHUNK), F32)],
        compiler_params=_cparams(("parallel", "arbitrary")),
        name="rwkv_scan",
    )(r, w, k, v, kk, b, s0, prm["bones"], prm["eye4"])


def _rwkv_chunk_kernel(r_ref, lw_ref, k_ref, v_ref, kk_ref, b_ref, y_ref, hout_ref, h_ref, *, nchunks):
    t = pl.program_id(1)
    cs = RW_CHUNK
    n = SEG_CHUNK

    @pl.when(t == 0)
    def _():
        h_ref[...] = jnp.zeros(h_ref.shape, F32)

    row = lax.broadcasted_iota(jnp.int32, (n, n), 0)
    lane = lax.broadcasted_iota(jnp.int32, (n, n), 1)
    same_head = (row // cs) == (lane // cs)
    strict = same_head & ((lane % cs) < (row % cs))
    incl = same_head & ((lane % cs) <= (row % cs))
    eye = row == lane
    lane_head = lax.broadcasted_iota(jnp.int32, (cs, n), 1) // RW_HEAD
    row_in = lax.broadcasted_iota(jnp.int32, (cs, n), 0)

    def stack(x):
        return jnp.concatenate([jnp.where(lane_head == hl, x, 0.0) for hl in range(4)], axis=0)

    def tile4(x):
        return jnp.concatenate([x] * 4, axis=1)

    def cumsum_rows(x):
        for sh in (1, 2, 4, 8, 16, 32):
            x = x + jnp.where(row_in >= sh, pltpu.roll(x, sh, 0), 0.0)
        return x

    for c in range(4):
        sl = slice(c * n, (c + 1) * n)
        lw = lw_ref[:, sl]
        l_in = cumsum_rows(lw)
        l_end = l_in[cs - 1:cs, :]
        e_neg = jnp.exp(-l_in)
        e_tail = jnp.exp(l_end - l_in)
        kq = stack(kk_ref[:, sl] * jnp.exp(l_in - lw)).astype(BF16)
        rq = stack(r_ref[:, sl] * jnp.exp(l_in)).astype(BF16)
        k_c = k_ref[:, sl]
        b_c = b_ref[:, sl]
        kh = (k_c * e_neg).astype(BF16)
        bh = (b_c * e_neg).astype(BF16)
        a_k = jnp.where(strict, tile4(_dot_nt(kq, kh)), 0.0)
        a_b = jnp.where(strict, tile4(_dot_nt(kq, bh)), 0.0)
        r_k = jnp.where(incl, tile4(_dot_nt(rq, kh)), 0.0)
        r_b = jnp.where(incl, tile4(_dot_nt(rq, bh)), 0.0)
        pw = a_b
        inv = jnp.where(eye, 1.0, 0.0) - a_b
        for _ in range(5):
            pb = pw.astype(BF16)
            pw = _dot(pb, pb)
            inv = inv + _dot(inv.astype(BF16), pw.astype(BF16))
        h0 = h_ref[c]
        h0b = h0.astype(BF16)
        v_bd = stack(v_ref[:, sl]).astype(BF16)
        u = _dot(inv.astype(BF16), (_dot(kq, h0b) + _dot(a_k.astype(BF16), v_bd)).astype(BF16))
        ub = u.astype(BF16)
        y = _dot(rq, h0b) + _dot(r_k.astype(BF16), v_bd) - _dot(r_b.astype(BF16), ub)
        y_ref[:, sl] = y[0:cs] + y[cs:2 * cs] + y[2 * cs:3 * cs] + y[3 * cs:4 * cs]
        tail = jnp.concatenate([stack(k_c * e_tail), -stack(b_c * e_tail)], axis=0)
        upd = _dot(tail.T.astype(BF16), jnp.concatenate([v_bd, ub], axis=0))
        decay_col = jnp.sum(jnp.where(eye, jnp.broadcast_to(jnp.exp(l_end), (n, n)), 0.0), axis=1, keepdims=True)
        h_ref[c] = decay_col * h0 + upd

    @pl.when(t == nchunks - 1)
    def _():
        hout_ref[...] = h_ref[...]


def _rwkv_chunked(r, lw, k, v, kk, b):
    n, t, _ = r.shape
    assert t % RW_CHUNK == 0 and RW_CHUNK == RW_HEAD
    nchunks = t // RW_CHUNK
    seq = pl.BlockSpec((None, RW_CHUNK, RW_WIDTH), lambda i, j: (i, j, 0))
    st = pl.BlockSpec((None, 4, SEG_CHUNK, SEG_CHUNK), lambda i, j: (i, 0, 0, 0))
    y, h = pl.pallas_call(
        functools.partial(_rwkv_chunk_kernel, nchunks=nchunks),
        grid=(n, nchunks),
        in_specs=[seq] * 6,
        out_specs=[seq, st],
        out_shape=[jax.ShapeDtypeStruct((n, t, RW_WIDTH), F32),
                   jax.ShapeDtypeStruct((n, 4, SEG_CHUNK, SEG_CHUNK), F32)],
        scratch_shapes=[pltpu.VMEM((4, SEG_CHUNK, SEG_CHUNK), F32)],
        compiler_params=_cparams(("parallel", "arbitrary")),
        name="rwkv_chunked",
    )(r, lw, k, v, kk, b)
    h = h.reshape(n, 4, 4, RW_HEAD, 4, RW_HEAD)
    hd = jnp.stack([h[:, :, hl, :, hl, :] for hl in range(4)], axis=2)
    return y, jnp.swapaxes(hd, 3, 4).reshape(n, RW_HEADS, RW_HEAD, RW_HEAD)


def _rwkv_post_kernel(y_ref, r_ref, k_ref, v_ref, g_ref, rk_ref, lg_ref, lb_ref, bones_ref, o_ref):
    bones = bones_ref[...]
    y = y_ref[...]
    d = y - _seg_sum(y, bones) * (1.0 / RW_HEAD)
    var = _seg_sum(d * d, bones) * (1.0 / RW_HEAD)
    yn = d * lax.rsqrt(var + GN_EPS) * lg_ref[...] + lb_ref[...]
    bonus = _seg_sum(r_ref[...] * k_ref[...] * rk_ref[...], bones) * v_ref[...]
    o_ref[...] = ((yn + bonus) * g_ref[...]).astype(o_ref.dtype)


def _rwkv_post(y, r, k, v, g, prm, tm=256):
    m = y.shape[0]
    tm = _tile(m, tm)
    big = pl.BlockSpec((tm, RW_WIDTH), lambda i: (i, 0))
    row = pl.BlockSpec((1, RW_WIDTH), lambda i: (0, 0))
    return pl.pallas_call(
        _rwkv_post_kernel,
        grid=(m // tm,),
        in_specs=[big] * 5 + [row] * 3 + [pl.BlockSpec((SEG_CHUNK, SEG_CHUNK), lambda i: (0, 0))],
        out_specs=big,
        out_shape=jax.ShapeDtypeStruct((m, RW_WIDTH), BF16),
        compiler_params=_cparams(("parallel",)),
        name="rwkv_post",
    )(y, r, k, v, g, prm["r_k"], prm["ln_g"], prm["ln_b"], prm["bones"])


def _state_to_stacked(s):
    n = s.shape[0]
    s = s.reshape(n, 4, 4, RW_HEAD, RW_HEAD)
    return jnp.transpose(s, (0, 1, 3, 2, 4)).reshape(n, 4 * RW_HEAD, SEG_CHUNK)


def _state_from_stacked(s):
    n = s.shape[0]
    s = s.reshape(n, 4, RW_HEAD, 4, RW_HEAD)
    return jnp.transpose(s, (0, 1, 3, 2, 4)).reshape(n, RW_HEADS, RW_HEAD, RW_HEAD)


def _top3_rows(sc, idx, nvalid_mask):
    big = float(sc.shape[0])
    sc = jnp.where(nvalid_mask, sc, NEG)
    sel = jnp.zeros(sc.shape, F32)
    for _ in range(MOBA_TOPK):
        mx = jnp.max(sc, axis=0, keepdims=True)
        first = jnp.min(jnp.where(sc == mx, idx, big), axis=0, keepdims=True)
        hit = (idx == first) & (mx > 0.5 * NEG)
        sel = jnp.where(hit, 1.0, sel)
        sc = jnp.where(hit, NEG, sc)
    return sel


def _moba_prompt_kernel(q_ref, k_ref, v_ref, bown_ref, bprev_ref, bfar_ref, o_ref,
                        kb_ref, vt_ref, kmh_ref, kml_ref, sel_ref, m_ref, l_ref, acc_ref, *, nb):
    i = pl.program_id(1)
    blk = MOBA_BLOCK

    @pl.when(i == 0)
    def _():
        def prep(j, c):
            rows = pl.ds(pl.multiple_of(j * blk, blk), blk)
            kj = k_ref[rows, :]
            kb_ref[rows, :] = kj.astype(BF16)
            km = jnp.sum(kj, axis=0, keepdims=True) * (1.0 / blk)
            hi, lo = _split(km)
            kmh_ref[pl.ds(j, 1), :] = hi.astype(F32)
            kml_ref[pl.ds(j, 1), :] = lo.astype(F32)
            vt_ref[j] = v_ref[rows, :].T.astype(BF16)
            return c
        lax.fori_loop(0, nb, prep, 0)

    qt = q_ref[...].T
    qh, ql = _split(qt)
    kmh = kmh_ref[...].astype(BF16)
    kml = kml_ref[...].astype(BF16)
    sc = _dot(kmh, qh) + _dot(kmh, ql) + _dot(kml, qh)
    bidx = lax.broadcasted_iota(jnp.int32, sc.shape, 0)
    sel_ref[...] = _top3_rows(sc, bidx.astype(F32), bidx < i)

    qs = (qt * (HEAD_DIM ** -0.5)).astype(BF16)

    def keys(j, n):
        return kb_ref[pl.ds(pl.multiple_of(j * blk, blk), n * blk), :]

    def values_t(j, n):
        return vt_ref[j] if n == 1 else jnp.concatenate([vt_ref[j + g] for g in range(n)], axis=1)

    half = blk // 2
    kidx = lax.broadcasted_iota(jnp.int32, (blk, half), 0)
    qidx = lax.broadcasted_iota(jnp.int32, (blk, half), 1)
    bfar = bfar_ref[0:1, 0:1]

    def attend(j, n, kind, first):
        k2 = keys(j, n)
        vt = values_t(j, n)
        for hf in range(2):
            qsl = slice(hf * half, (hf + 1) * half)
            picked = lambda g: jnp.broadcast_to(sel_ref[pl.ds(j + g, 1), :][:, qsl], (blk, half)) > 0.5
            causal = kidx <= qidx + hf * half
            s = _dot(k2, qs[:, qsl])
            if kind == "far":
                s = s + bfar
                keep = picked(0) if n == 1 else jnp.concatenate([picked(g) for g in range(n)], axis=0)
            elif kind == "prev_own":
                s = s + jnp.concatenate([bprev_ref[:, qsl], bown_ref[:, qsl]], axis=0)
                keep = jnp.concatenate([picked(0), causal], axis=0)
            else:
                s = s + bown_ref[:, qsl]
                keep = causal
            s = jnp.where(keep, s, NEG)
            if first:
                m_new = jnp.max(s, axis=0, keepdims=True)
                p = jnp.exp(s - m_new)
                l_ref[:, qsl] = jnp.sum(p, axis=0, keepdims=True)
                acc_ref[:, qsl] = _dot(vt, p.astype(BF16))
            else:
                m_old = m_ref[:, qsl]
                m_new = jnp.maximum(m_old, jnp.max(s, axis=0, keepdims=True))
                a = jnp.exp(m_old - m_new)
                p = jnp.exp(s - m_new)
                l_ref[:, qsl] = a * l_ref[:, qsl] + jnp.sum(p, axis=0, keepdims=True)
                acc_ref[:, qsl] = a * acc_ref[:, qsl] + _dot(vt, p.astype(BF16))
            m_ref[:, qsl] = m_new

    @pl.when(i == 0)
    def _():
        attend(i, 1, "own", True)

    @pl.when(i >= 1)
    def _():
        attend(i - 1, 2, "prev_own", True)

    n_far = jnp.maximum(i - 1, 0)

    def far_quad(jj, c):
        attend(4 * jj, 4, "far", False)
        return c
    lax.fori_loop(0, n_far // 4, far_quad, 0)

    @pl.when((n_far & 2) != 0)
    def _():
        attend((n_far // 4) * 4, 2, "far", False)

    @pl.when((n_far & 1) != 0)
    def _():
        attend(n_far - 1, 1, "far", False)

    o_ref[...] = (acc_ref[...] / l_ref[...]).T.astype(o_ref.dtype)


def _moba_prompt(z1, bias):
    t = z1.shape[0]
    assert t % MOBA_BLOCK == 0
    nb = t // MOBA_BLOCK
    blk = MOBA_BLOCK
    h8 = ATTN_HEADS
    tile = pl.BlockSpec((None, blk, blk), lambda h, i: (h, 0, 0))
    return pl.pallas_call(
        functools.partial(_moba_prompt_kernel, nb=nb),
        grid=(h8, nb),
        in_specs=[pl.BlockSpec((blk, HEAD_DIM), lambda h, i: (i, h)),
                  pl.BlockSpec((t, HEAD_DIM), lambda h, i: (0, h8 + h)),
                  pl.BlockSpec((t, HEAD_DIM), lambda h, i: (0, 2 * h8 + h)),
                  tile, tile,
                  pl.BlockSpec((None, 1, HEAD_DIM), lambda h, i: (h, 0, 0))],
        out_specs=pl.BlockSpec((blk, HEAD_DIM), lambda h, i: (i, h)),
        out_shape=jax.ShapeDtypeStruct((t, ATTN_WIDTH), BF16),
        scratch_shapes=[pltpu.VMEM((t, HEAD_DIM), BF16),
                        pltpu.VMEM((nb, HEAD_DIM, blk), BF16),
                        pltpu.VMEM((nb, HEAD_DIM), F32),
                        pltpu.VMEM((nb, HEAD_DIM), F32),
                        pltpu.VMEM((nb, blk), F32),
                        pltpu.VMEM((1, blk), F32),
                        pltpu.VMEM((1, blk), F32),
                        pltpu.VMEM((HEAD_DIM, blk), F32)],
        compiler_params=_cparams(("arbitrary", "arbitrary")),
        name="moba_prompt",
    )(z1, z1, z1, bias["own_t"], bias["prev_t"], bias["far"])


def _diag_extract(s):
    lane = lax.broadcasted_iota(jnp.int32, (ATTN_HEADS, s.shape[1]), 1) % ATTN_HEADS
    out = jnp.zeros((ATTN_HEADS, s.shape[1]), F32)
    for hp in range(ATTN_HEADS):
        out = out + jnp.where(lane == hp, s[hp * 8:(hp + 1) * 8, :], 0.0)
    return out


def _diag_expand(p):
    lane = lax.broadcasted_iota(jnp.int32, p.shape, 1) % ATTN_HEADS
    return jnp.concatenate([jnp.where(lane == hp, p, 0.0) for hp in range(ATTN_HEADS)], axis=0)


def _class_allreduce(x, op):
    for sh in (8, 16, 32, 64):
        x = op(x, pltpu.roll(x, sh, 1))
    return x


def _fold_tiles(x, op):
    out = x[:, 0:128]
    for c in range(1, x.shape[1] // 128):
        out = op(out, x[:, c * 128:(c + 1) * 128])
    return out


def _moba_sample_kernel(pt_ref, q_ref, *refs, nblk, grp):
    k_refs = refs[:2 * grp]
    v_refs = refs[2 * grp:4 * grp]
    (kn_ref, vn_ref, blast_ref, bown_ref, bfar_ref, o_ref,
     sc_ref, bsum_ref, bmax_ref, bexp_ref, selx_ref, m_ref, li_ref, acc_ref) = refs[4 * grp:]
    ph = pl.program_id(1)
    step = pl.program_id(2)
    nsteps = nblk // grp
    ntok = 8
    rows = PAGE_SIZE * ATTN_HEADS
    nkeys = 2 * rows
    far16 = jnp.concatenate([bfar_ref[...]] * (nkeys // 128), axis=1)
    lane_blk = lax.broadcasted_iota(jnp.int32, (ntok, 128), 1) // 8

    def block_bias(j):
        return jnp.where(j == nblk - 1, blast_ref[...], far16)

    def tile16(x):
        return jnp.concatenate([x] * (nkeys // 128), axis=1)

    def compact_tile(j):
        return pl.ds(pl.multiple_of((j // 16) * 128, 128), 128)

    def put_compact(ref, j, x):
        ref[:, compact_tile(j)] = jnp.where(lane_blk == j % 16, x, ref[:, compact_tile(j)])

    def get_compact(ref, j):
        mine = jnp.where(lane_blk == j % 16, ref[:, compact_tile(j)], 0.0)
        return _class_allreduce(mine, jnp.add)

    @pl.when((ph == 0) & (step == 0))
    def _():
        bmax_ref[...] = jnp.zeros(bmax_ref.shape, F32)
        bexp_ref[...] = jnp.zeros(bexp_ref.shape, F32)

    @pl.when(ph == 0)
    def _():
        qs = (q_ref[...] * (HEAD_DIM ** -0.5)).astype(BF16)
        for g in range(grp):
            j = step * grp + g
            k0 = k_refs[2 * g][...]
            k1 = k_refs[2 * g + 1][...]
            bsum_ref[pl.ds(pl.multiple_of(j * 8, 8), 8), :] = jnp.sum(k0, axis=0) + jnp.sum(k1, axis=0)
            k2 = jnp.concatenate([k0.reshape(rows, HEAD_DIM), k1.reshape(rows, HEAD_DIM)], axis=0).astype(BF16)
            sc = _diag_extract(_dot_nt(qs, k2)) + block_bias(j)
            sc_ref[j] = sc
            bm = _class_allreduce(_fold_tiles(sc, jnp.maximum), jnp.maximum)
            be = _class_allreduce(_fold_tiles(jnp.exp(sc - tile16(bm)), jnp.add), jnp.add)
            put_compact(bmax_ref, j, bm)
            put_compact(bexp_ref, j, be)

    @pl.when((ph == 0) & (step == nsteps - 1))
    def _():
        q = q_ref[...]
        qh, ql = _split(q)
        bh, bl = _split(bsum_ref[...] * (1.0 / MOBA_BLOCK))
        scx = _diag_extract(_dot_nt(qh, bh) + _dot_nt(qh, bl) + _dot_nt(ql, bh))
        width = nblk * 8
        jidx = (lax.broadcasted_iota(jnp.int32, (ntok, width), 1) // 8).astype(F32)

        def creduce(x, op):
            y = _class_allreduce(_fold_tiles(x, op), op)
            return jnp.concatenate([y] * (width // 128), axis=1)

        selx = jnp.zeros((ntok, width), F32)
        for _ in range(MOBA_TOPK):
            mx = creduce(scx, jnp.maximum)
            first = creduce(jnp.where(scx == mx, jidx, float(nblk)), jnp.minimum)
            hit = jidx == first
            selx = jnp.where(hit, 1.0, selx)
            scx = jnp.where(hit, NEG, scx)
        selx_ref[...] = selx
        picked = selx > 0.5

        qs = (q * (HEAD_DIM ** -0.5)).astype(BF16)
        so = _diag_extract(_dot_nt(qs, kn_ref[...].astype(BF16))) + bown_ref[...]
        tq = lax.broadcasted_iota(jnp.int32, so.shape, 0)
        tk = lax.broadcasted_iota(jnp.int32, so.shape, 1) // 8
        so = jnp.concatenate([jnp.where(tk <= tq, so, NEG), jnp.full((ntok, 64), NEG, F32)], axis=1)

        bmax = bmax_ref[...]
        m = jnp.maximum(_class_allreduce(so, jnp.maximum),
                        creduce(jnp.where(picked, bmax, NEG), jnp.maximum)[:, 0:128])
        mw = jnp.concatenate([m] * (width // 128), axis=1)
        l_blocks = creduce(jnp.where(picked, bexp_ref[...] * jnp.exp(bmax - mw), 0.0), jnp.add)[:, 0:128]
        po = jnp.exp(so - m)
        li = 1.0 / (_class_allreduce(po, jnp.add) + l_blocks)
        m_ref[...] = m
        li_ref[...] = li
        pfull = _diag_expand((po * li)[:, 0:64]).astype(BF16)
        acc_ref[...] = _dot(pfull, vn_ref[...].astype(BF16))

    @pl.when(ph == 1)
    def _():
        m16 = tile16(m_ref[...])
        li16 = tile16(li_ref[...])
        acc = acc_ref[...]
        for g in range(grp):
            j = step * grp + g
            p = jnp.exp(sc_ref[j] - m16) * li16
            p = jnp.where(tile16(get_compact(selx_ref, j)) > 0.5, p, 0.0)
            v2 = jnp.concatenate([v_refs[2 * g][...].reshape(rows, HEAD_DIM),
                                  v_refs[2 * g + 1][...].reshape(rows, HEAD_DIM)], axis=0).astype(BF16)
            acc = acc + _dot(_diag_expand(p).astype(BF16), v2)
        acc_ref[...] = acc

    @pl.when((ph == 1) & (step == nsteps - 1))
    def _():
        o_ref[...] = acc_ref[...]


def _moba_sample(qht, knew, vnew, cache_k, cache_v, page_table, bias):
    nb_, n_pages = page_table.shape
    assert n_pages % PAGES_PER_BLOCK == 0 and PAGES_PER_BLOCK == 2
    nblk = n_pages // 2
    assert nblk >= MOBA_TOPK and (nblk * 8) % 128 == 0
    nkeys = 2 * PAGE_SIZE * ATTN_HEADS
    grp = SAMPLE_BLOCKS_PER_STEP
    assert nblk % grp == 0
    nsteps = nblk // grp

    def kmap(u):
        return lambda b, p, j, pt: (0, pt[b, 2 * grp * jnp.where(p == 0, j, nsteps - 1) + u], 0, 0, 0)

    def vmap_(u):
        return lambda b, p, j, pt: (0, pt[b, 2 * grp * jnp.where(p == 0, 0, j) + u], 0, 0, 0)

    page = (None, None, PAGE_SIZE, ATTN_HEADS, HEAD_DIM)
    per_b = pl.BlockSpec((None, 64, HEAD_DIM), lambda b, p, j, pt: (b, 0, 0))
    cst = lambda r, c: pl.BlockSpec((r, c), lambda b, p, j, pt: (0, 0))
    grid_spec = pltpu.PrefetchScalarGridSpec(
        num_scalar_prefetch=1,
        grid=(nb_, 2, nsteps),
        in_specs=[per_b]
                 + [pl.BlockSpec(page, kmap(u)) for u in range(2 * grp)]
                 + [pl.BlockSpec(page, vmap_(u)) for u in range(2 * grp)]
                 + [per_b, per_b, cst(8, nkeys), cst(8, 64), cst(1, 128)],
        out_specs=per_b,
        scratch_shapes=[pltpu.VMEM((nblk, 8, nkeys), F32),
                        pltpu.VMEM((nblk * 8, HEAD_DIM), F32),
                        pltpu.VMEM((8, nblk * 8), F32),
                        pltpu.VMEM((8, nblk * 8), F32),
                        pltpu.VMEM((8, nblk * 8), F32),
                        pltpu.VMEM((8, 128), F32),
                        pltpu.VMEM((8, 128), F32),
                        pltpu.VMEM((64, HEAD_DIM), F32)])
    return pl.pallas_call(
        functools.partial(_moba_sample_kernel, nblk=nblk, grp=grp),
        grid_spec=grid_spec,
        out_shape=jax.ShapeDtypeStruct((nb_, 64, HEAD_DIM), F32),
        compiler_params=_cparams(("arbitrary", "arbitrary", "arbitrary")),
        name="moba_sample",
    )(page_table, qht, *([cache_k] * (2 * grp)), *([cache_v] * (2 * grp)), knew, vnew,
      bias["last_s"], bias["own_s"], bias["far_s"])


def _bias_of_distance(dist, rel_bias):
    dist = jnp.maximum(dist, 0)
    exact = N_BUCKETS // 2
    log_ratio = jnp.log(jnp.maximum(dist, 1).astype(F32) / exact) / math.log(MAX_DISTANCE / exact)
    large = jnp.minimum(exact + (log_ratio * (N_BUCKETS - exact)).astype(jnp.int32), N_BUCKETS - 1)
    return rel_bias[jnp.where(dist < exact, dist, large)]


def _bias_tiles(rel_bias, past_len, dec_seq):
    blk = MOBA_BLOCK
    val = _bias_of_distance(jnp.arange(2 * blk), rel_bias)
    val_t = val.T

    def toeplitz(ext):
        n2 = ext.shape[1]
        n = n2 // 2
        return jnp.tile(ext, (1, n))[:, :n * (n2 - 1)].reshape(ext.shape[0], n, n2 - 1)[:, :, :n]

    own_t = toeplitz(val_t)
    prev_t = toeplitz(jnp.roll(val_t, -blk, axis=1))
    far_row = val[2 * blk - 1]
    far = jnp.broadcast_to(far_row[:, None, None], (ATTN_HEADS, 1, HEAD_DIM))
    assert past_len % blk == 0 and 2 * blk - 1 >= MAX_DISTANCE and dec_seq < blk
    last = jnp.stack([val[t + 1:t + 1 + blk][::-1] for t in range(dec_seq)])
    last_s = last.reshape(dec_seq, blk * ATTN_HEADS)
    own = jnp.stack([jnp.concatenate([val[:t + 1][::-1], jnp.zeros((dec_seq - 1 - t, ATTN_HEADS), F32)])
                     for t in range(dec_seq)])
    own_s = own.reshape(dec_seq, dec_seq * ATTN_HEADS)
    far_s = jnp.tile(far_row, 128 // ATTN_HEADS)[None, :]
    return dict(own_t=own_t, prev_t=prev_t, far=far, last_s=last_s, own_s=own_s, far_s=far_s)


def _np_consts():
    seg = np.arange(SEG_CHUNK) // RW_HEAD
    bones = (seg[:, None] == seg[None, :]).astype(np.float32)
    idx = np.arange(SEG_CHUNK) % RW_HEAD
    eye4 = (idx[:, None] == idx[None, :]).astype(np.float32)
    return bones, eye4


def _pad_rows(w, row0, total):
    return jnp.zeros((total, w.shape[1]), BF16).at[row0:row0 + w.shape[0]].set(w.astype(BF16))


def kernel(x_prompt, x_sample, cache_k, cache_v, page_table, state_wkv, state_shift, w_in, rel_bias, rw_mu, rw_w0, rw_w2, rw_a0, rw_a2, rw_g2, rw_k_k, rw_k_a, rw_r_k, rw_ln_g, rw_ln_b, w_up_attn, w_up_rwkv, w_o, ln1_g, ln1_b, w_ffn_gate, w_ffn_up, w_ffn_down, ln2_g, ln2_b):
    assert x_prompt.shape[0] == 1 and w_in.shape[0] == DEPTH == 1
    t_p = x_prompt.shape[1]
    nb_s, t_s, _ = x_sample.shape
    assert t_s == 8
    past_len = page_table.shape[1] * PAGE_SIZE

    w_in2 = w_in[0]
    w_main = w_in2[:, :MAIN_COLS].astype(BF16)
    w_lora = jnp.pad(w_in2[:, LORA_COL:GATE_COL], ((0, 0), (0, LORA_PAD - LORA_COLS))).astype(BF16)
    w_gate = w_in2[:, GATE_COL:].astype(BF16)
    wb_up_attn, wb_up_rwkv, wb_o = w_up_attn[0].astype(BF16), w_up_rwkv[0].astype(BF16), w_o[0].astype(BF16)
    wb_gate, wb_up, wb_down = w_ffn_gate[0].astype(BF16), w_ffn_up[0].astype(BF16), w_ffn_down[0].astype(BF16)
    bones_np, eye_np = _np_consts()
    mu = rw_mu[0]
    prm = dict(
        mu=mu[None, :3 * RW_WIDTH],
        mul=jnp.pad(mu[3 * RW_WIDTH:], (0, LORA_PAD - LORA_COLS))[None, :],
        w0=rw_w0, a0=rw_a0,
        w2=_pad_rows(rw_w2[0], 0, LORA_PAD),
        a2=_pad_rows(rw_a2[0], DECAY_LORA, LORA_PAD),
        g2=_pad_rows(rw_g2[0], DECAY_LORA + AAA_LORA, LORA_PAD),
        k_k=rw_k_k, k_a=rw_k_a, r_k=rw_r_k.reshape(1, RW_WIDTH), ln_g=rw_ln_g, ln_b=rw_ln_b,
        bones=jnp.asarray(bones_np, BF16), eye4=jnp.asarray(eye_np, F32))
    bias = _bias_tiles(rel_bias, past_len, t_s)

    def group(x2d, nseq, tseq, zprev, zlprev, s0, attend, tt, tb):
        z1 = _proj(x2d, w_main, 0, MAIN_COLS)
        zl = _proj(x2d, w_lora, 0, LORA_PAD)
        zg = _proj(x2d, w_gate, 0, 2 * D_MODEL)
        o_attn = attend(z1)
        r, w, k, v, kk, b, g = _rwkv_prep(z1.reshape(nseq, tseq, MAIN_COLS), zl.reshape(nseq, tseq, LORA_PAD),
                                          zprev, zlprev, prm, tt)
        if s0 is None:
            y, s_fin = _rwkv_chunked(r, w, k, v, kk, b)
        else:
            y, s_fin = _rwkv_scan(r, w, k, v, kk, b, s0, prm, tb)
            s_fin = _state_from_stacked(s_fin)
        flat = lambda u: u.reshape(nseq * tseq, RW_WIDTH)
        o_rwkv = _rwkv_post(flat(y), flat(r), flat(k), flat(v), flat(g), prm)
        mixed = _merge(o_attn, o_rwkv, zg, wb_up_attn, wb_up_rwkv)
        h = _proj_ln(mixed, wb_o, x2d, ln1_g, ln1_b, tm=1024)
        act = _ffn_up(h, wb_gate, wb_up)
        out = _proj_ln(act, wb_down, h, ln2_g, ln2_b, tm=512)
        new_k = z1[:, ATTN_WIDTH:2 * ATTN_WIDTH]
        new_v = z1[:, 2 * ATTN_WIDTH:3 * ATTN_WIDTH]
        return out, new_k, new_v, s_fin

    xp = x_prompt[0]
    yp, kp, vp, sp = group(
        xp, 1, t_p,
        jnp.zeros((1, 1, 3 * RW_WIDTH), F32), jnp.zeros((1, 1, LORA_PAD), F32),
        None, lambda z1: _moba_prompt(z1, bias), 256, 8)

    xs = x_sample.reshape(nb_s * t_s, D_MODEL)
    sh = state_shift[0]
    zprev = _proj(sh, w_main, RKV_COL, 3 * RW_WIDTH)[:, None, :]
    zlprev = _proj(sh, w_lora, 0, LORA_PAD)[:, None, :]

    def attend_sample(z1):
        q = z1[:, 0:ATTN_WIDTH].reshape(nb_s, t_s, ATTN_HEADS, HEAD_DIM)
        qht = jnp.transpose(q, (0, 2, 1, 3)).reshape(nb_s, ATTN_HEADS * t_s, HEAD_DIM)
        kn = z1[:, ATTN_WIDTH:2 * ATTN_WIDTH].reshape(nb_s, t_s * ATTN_HEADS, HEAD_DIM)
        vn = z1[:, 2 * ATTN_WIDTH:3 * ATTN_WIDTH].reshape(nb_s, t_s * ATTN_HEADS, HEAD_DIM)
        o = _moba_sample(qht, kn, vn, cache_k, cache_v, page_table, bias)
        o = jnp.transpose(o.reshape(nb_s, ATTN_HEADS, t_s, HEAD_DIM), (0, 2, 1, 3))
        return o.reshape(nb_s * t_s, ATTN_WIDTH).astype(BF16)

    ys, ks, vs, ss = group(xs, nb_s, t_s, zprev, zlprev, _state_to_stacked(state_wkv[0]),
                           attend_sample, 8, 8)

    return (yp[None], ys.reshape(nb_s, t_s, D_MODEL),
            kp.reshape(1, 1, t_p, ATTN_HEADS, HEAD_DIM), vp.reshape(1, 1, t_p, ATTN_HEADS, HEAD_DIM),
            sp[None], xp[None, -1:, :],
            ks.reshape(1, nb_s, t_s, ATTN_HEADS, HEAD_DIM), vs.reshape(1, nb_s, t_s, ATTN_HEADS, HEAD_DIM),
            ss[None], x_sample[None, :, -1, :])
```

```python
import functools
import math

import numpy as np
import jax
import jax.numpy as jnp
from jax import lax
from jax.experimental import pallas as pl
from jax.experimental.pallas import tpu as pltpu

F32 = jnp.float32
BF16 = jnp.bfloat16

D_MODEL = 2048
HEAD_DIM = 128
ATTN_HEADS = D_MODEL // 256
ATTN_WIDTH = ATTN_HEADS * HEAD_DIM
MOBA_BLOCK = 256
MOBA_TOPK = 3
N_BUCKETS = 32
MAX_DISTANCE = 128
PAGE_SIZE = 128
RW_HEAD = 64
RW_HEADS = D_MODEL // 128
RW_WIDTH = RW_HEADS * RW_HEAD
DECAY_LORA = 96
AAA_LORA = 96
GATE_LORA = 256
LORA_COLS = DECAY_LORA + AAA_LORA + GATE_LORA
LORA_PAD = 512
GN_EPS = RW_HEAD * 1e-5
D_FF = 5632
LN_EPS = 1e-5
LN_ROWS = 128
DEPTH = 1
DEEPNORM_ALPHA = (2 * DEPTH) ** 0.25

RKV_COL = 3 * ATTN_WIDTH
LORA_COL = RKV_COL + 3 * RW_WIDTH
GATE_COL = LORA_COL + LORA_COLS
MAIN_COLS = LORA_COL

NEG = -1e30
LOG2E = math.log2(math.e)
FAR_PARTS = 4
ONES_ROWS = 8
SEG_CHUNK = 256
PAGES_PER_BLOCK = MOBA_BLOCK // PAGE_SIZE
SAMPLE_BLOCKS_PER_STEP = 8
RW_CHUNK = 64
VMEM_LIMIT = 56 * 1024 * 1024


def _cparams(sem):
    return pltpu.CompilerParams(dimension_semantics=sem, vmem_limit_bytes=VMEM_LIMIT)


def _dot(a, b):
    return jnp.dot(a, b, preferred_element_type=F32)


def _dot_nt(a, b):
    return lax.dot_general(a, b, (((1,), (1,)), ((), ())), preferred_element_type=F32)


def _split(x):
    hi = x.astype(BF16)
    lo = (x - hi.astype(F32)).astype(BF16)
    return hi, lo


def _sigmoid(x):
    return 1.0 / (1.0 + jnp.exp(-x))


def _tile(m, pref):
    t = min(m, pref)
    assert m % t == 0, (m, pref)
    return t


def _proj_kernel(x_ref, w_ref, o_ref, xb_ref):
    @pl.when(pl.program_id(1) == 0)
    def _():
        xb_ref[...] = x_ref[...].astype(BF16)

    o_ref[...] = _dot(xb_ref[...], w_ref[...].astype(BF16)).astype(o_ref.dtype)


def _proj(x, w, col0, ncols, tm=1024, tn=512):
    m, k = x.shape
    tm = _tile(m, tm)
    tn = _tile(ncols, tn)
    assert col0 % tn == 0
    c0 = col0 // tn
    return pl.pallas_call(
        _proj_kernel,
        grid=(m // tm, ncols // tn),
        in_specs=[pl.BlockSpec((tm, k), lambda i, j: (i, 0)),
                  pl.BlockSpec((k, tn), lambda i, j: (0, c0 + j))],
        out_specs=pl.BlockSpec((tm, tn), lambda i, j: (i, j)),
        out_shape=jax.ShapeDtypeStruct((m, ncols), F32),
        scratch_shapes=[pltpu.VMEM((tm, k), BF16)],
        compiler_params=_cparams(("parallel", "arbitrary")),
        name="proj",
    )(x, w)


def _merge_kernel(oa_ref, or_ref, wa_ref, wr_ref, ga_ref, gr_ref, o_ref):
    a = _dot(oa_ref[...], wa_ref[...].astype(BF16))
    r = _dot(or_ref[...], wr_ref[...].astype(BF16))
    o_ref[...] = (_sigmoid(ga_ref[...]) * a + _sigmoid(gr_ref[...]) * r).astype(o_ref.dtype)


def _merge(o_attn, o_rwkv, zg, w_up_attn, w_up_rwkv, tm=1024, tn=512):
    m = o_attn.shape[0]
    tm = _tile(m, tm)
    nj = D_MODEL // tn
    return pl.pallas_call(
        _merge_kernel,
        grid=(m // tm, nj),
        in_specs=[pl.BlockSpec((tm, ATTN_WIDTH), lambda i, j: (i, 0)),
                  pl.BlockSpec((tm, RW_WIDTH), lambda i, j: (i, 0)),
                  pl.BlockSpec((ATTN_WIDTH, tn), lambda i, j: (0, j)),
                  pl.BlockSpec((RW_WIDTH, tn), lambda i, j: (0, j)),
                  pl.BlockSpec((tm, tn), lambda i, j: (i, j)),
                  pl.BlockSpec((tm, tn), lambda i, j: (i, j + nj))],
        out_specs=pl.BlockSpec((tm, tn), lambda i, j: (i, j)),
        out_shape=jax.ShapeDtypeStruct((m, D_MODEL), BF16),
        compiler_params=_cparams(("parallel", "arbitrary")),
        name="merge",
    )(o_attn, o_rwkv, w_up_attn, w_up_rwkv, zg, zg)


def _proj_ln_kernel(m_ref, w_ref, x_ref, g_ref, b_ref, o_ref, *, tn, nj):
    j = pl.program_id(1)
    col = pl.multiple_of(j * tn, tn)
    mb = m_ref[...].astype(BF16)
    o_ref[:, pl.ds(col, tn)] = DEEPNORM_ALPHA * x_ref[...] + _dot(mb, w_ref[...].astype(BF16))

    @pl.when(j == nj - 1)
    def _():
        ln_rows = min(LN_ROWS, o_ref.shape[0])
        assert o_ref.shape[0] % ln_rows == 0

        def norm_rows(c, carry):
            rows = pl.ds(pl.multiple_of(c * ln_rows, ln_rows), ln_rows)
            y = o_ref[rows, :]
            mu = jnp.mean(y, axis=-1, keepdims=True)
            d = y - mu
            var = jnp.mean(d * d, axis=-1, keepdims=True)
            o_ref[rows, :] = d * lax.rsqrt(var + LN_EPS) * g_ref[...] + b_ref[...]
            return carry
        lax.fori_loop(0, o_ref.shape[0] // ln_rows, norm_rows, 0)


def _proj_ln(mat, w, x, g, b, tm=512, tn=256):
    m, k = mat.shape
    tm = _tile(m, tm)
    nj = D_MODEL // tn
    return pl.pallas_call(
        functools.partial(_proj_ln_kernel, tn=tn, nj=nj),
        grid=(m // tm, nj),
        in_specs=[pl.BlockSpec((tm, k), lambda i, j: (i, 0)),
                  pl.BlockSpec((k, tn), lambda i, j: (0, j)),
                  pl.BlockSpec((tm, tn), lambda i, j: (i, j)),
                  pl.BlockSpec((1, D_MODEL), lambda i, j: (0, 0)),
                  pl.BlockSpec((1, D_MODEL), lambda i, j: (0, 0))],
        out_specs=pl.BlockSpec((tm, D_MODEL), lambda i, j: (i, 0)),
        out_shape=jax.ShapeDtypeStruct((m, D_MODEL), F32),
        compiler_params=_cparams(("parallel", "arbitrary")),
        name="proj_ln",
    )(mat, w, x, g, b)


def _ffn_up_kernel(h_ref, wg_ref, wu_ref, o_ref, hb_ref):
    @pl.when(pl.program_id(1) == 0)
    def _():
        hb_ref[...] = h_ref[...].astype(BF16)

    hb = hb_ref[...]
    a = _dot(hb, wg_ref[...].astype(BF16))
    u = _dot(hb, wu_ref[...].astype(BF16))
    o_ref[...] = (a * _sigmoid(a) * u).astype(o_ref.dtype)


def _ffn_up(h, wg, wu, tm=1024, tn=256):
    m = h.shape[0]
    tm = _tile(m, tm)
    return pl.pallas_call(
        _ffn_up_kernel,
        grid=(m // tm, D_FF // tn),
        in_specs=[pl.BlockSpec((tm, D_MODEL), lambda i, j: (i, 0)),
                  pl.BlockSpec((D_MODEL, tn), lambda i, j: (0, j)),
                  pl.BlockSpec((D_MODEL, tn), lambda i, j: (0, j))],
        out_specs=pl.BlockSpec((tm, tn), lambda i, j: (i, j)),
        out_shape=jax.ShapeDtypeStruct((m, D_FF), BF16),
        scratch_shapes=[pltpu.VMEM((tm, D_MODEL), BF16)],
        compiler_params=_cparams(("parallel", "arbitrary")),
        name="ffn_up",
    )(h, wg, wu)


def _seg_sum(x, bones):
    outs = []
    for c in range(x.shape[1] // SEG_CHUNK):
        hi, lo = _split(x[:, c * SEG_CHUNK:(c + 1) * SEG_CHUNK])
        outs.append(_dot(hi, bones) + _dot(lo, bones))
    return jnp.concatenate(outs, axis=1)


def _shifted(z, prev):
    zs = pltpu.roll(z, 1, 0)
    row = lax.broadcasted_iota(jnp.int32, z.shape, 0)
    return jnp.where(row == 0, prev, zs)


def _rwkv_prep_kernel(z_ref, zl_ref, zp_ref, zlp_ref, mu_ref, mul_ref, w0_ref, a0_ref, w2_ref, a2_ref,
                      g2_ref, kkw_ref, kaw_ref, bones_ref,
                      r_o, w_o, k_o, v_o, kk_o, b_o, g_o, prev_ref, prevl_ref):
    @pl.when(pl.program_id(1) == 0)
    def _():
        prev_ref[...] = zp_ref[...]
        prevl_ref[...] = zlp_ref[...]

    z = z_ref[...]
    zl = zl_ref[...]
    tt = z.shape[0]
    zm = z + (_shifted(z, prev_ref[...]) - z) * mu_ref[...]
    zlm = zl + (_shifted(zl, prevl_ref[...]) - zl) * mul_ref[...]
    prev_ref[...] = z[tt - 1:tt, :]
    prevl_ref[...] = zl[tt - 1:tt, :]

    r = zm[:, 0:RW_WIDTH]
    k = zm[:, RW_WIDTH:2 * RW_WIDTH]
    v = zm[:, 2 * RW_WIDTH:3 * RW_WIDTH]
    xw = w0_ref[...] + _dot(jnp.tanh(zlm).astype(BF16), w2_ref[...])
    nx = -xw
    softplus = jnp.maximum(nx, 0.0) + jnp.log(1.0 + jnp.exp(-jnp.abs(nx)))
    log_decay = -jnp.exp(-softplus - 0.5)
    a = _sigmoid(a0_ref[...] + _dot(zlm.astype(BF16), a2_ref[...]))
    g = _dot(_sigmoid(zlm).astype(BF16), g2_ref[...])
    kk = k * kkw_ref[...]
    ssq = _seg_sum(kk * kk, bones_ref[...])
    kk = kk / jnp.maximum(jnp.sqrt(ssq), 1e-12)
    r_o[...] = r
    w_o[...] = log_decay
    k_o[...] = k * (1.0 + (a - 1.0) * kaw_ref[...])
    v_o[...] = v
    kk_o[...] = kk
    b_o[...] = kk * a
    g_o[...] = g


def _rwkv_prep(z1, zl, zprev, zlprev, prm, tt):
    n, t, _ = z1.shape
    tt = _tile(t, tt)
    row = lambda c: pl.BlockSpec((1, c), lambda i, j: (0, 0))
    mat = lambda r, c: pl.BlockSpec((r, c), lambda i, j: (0, 0))
    seq = lambda c: pl.BlockSpec((None, tt, c), lambda i, j: (i, j, 0))
    outs = pl.pallas_call(
        _rwkv_prep_kernel,
        grid=(n, t // tt),
        in_specs=[pl.BlockSpec((None, tt, 3 * RW_WIDTH), lambda i, j: (i, j, 1)),
                  seq(LORA_PAD),
                  pl.BlockSpec((None, 1, 3 * RW_WIDTH), lambda i, j: (i, 0, 0)),
                  pl.BlockSpec((None, 1, LORA_PAD), lambda i, j: (i, 0, 0)),
                  row(3 * RW_WIDTH), row(LORA_PAD), row(RW_WIDTH), row(RW_WIDTH),
                  mat(LORA_PAD, RW_WIDTH), mat(LORA_PAD, RW_WIDTH), mat(LORA_PAD, RW_WIDTH),
                  row(RW_WIDTH), row(RW_WIDTH), mat(SEG_CHUNK, SEG_CHUNK)],
        out_specs=[seq(RW_WIDTH)] * 7,
        out_shape=[jax.ShapeDtypeStruct((n, t, RW_WIDTH), F32)] * 7,
        scratch_shapes=[pltpu.VMEM((1, 3 * RW_WIDTH), F32), pltpu.VMEM((1, LORA_PAD), F32)],
        compiler_params=_cparams(("parallel", "arbitrary")),
        name="rwkv_prep",
    )(z1, zl, zprev, zlprev, prm["mu"], prm["mul"], prm["w0"], prm["a0"], prm["w2"], prm["a2"],
      prm["g2"], prm["k_k"], prm["k_a"], prm["bones"])
    return outs


def _rwkv_scan_kernel(r_ref, w_ref, k_ref, v_ref, kk_ref, b_ref, s0_ref, bones_ref, eye_ref,
                      y_ref, sout_ref, s_ref, *, tb, nblk):
    t = pl.program_id(1)

    @pl.when(t == 0)
    def _():
        s_ref[...] = s0_ref[...]

    bones = bones_ref[...]
    eye = eye_ref[...]

    def bc(ref, s):
        return jnp.concatenate(
            [jnp.broadcast_to(ref[pl.ds(s, 1), c * SEG_CHUNK:(c + 1) * SEG_CHUNK], (RW_HEAD, SEG_CHUNK))
             for c in range(4)], axis=0)

    def step(s, carry):
        st = s_ref[...]
        hi, lo = _split(st * bc(kk_ref, s))
        sa = _dot(hi, bones) + _dot(lo, bones)
        vcol = _dot((bc(v_ref, s) * eye).astype(BF16), bones)
        sn = st * jnp.exp(bc(w_ref, s)) - sa * bc(b_ref, s) + vcol * bc(k_ref, s)
        s_ref[...] = sn
        yb = _dot((sn * bc(r_ref, s)).astype(BF16), bones)
        y4 = jnp.sum((yb * eye).reshape(4, RW_HEAD, SEG_CHUNK), axis=1)
        for c in range(4):
            y_ref[pl.ds(s, 1), c * SEG_CHUNK:(c + 1) * SEG_CHUNK] = y4[c:c + 1, :]
        return carry

    lax.fori_loop(0, tb, step, 0)

    @pl.when(t == nblk - 1)
    def _():
        sout_ref[...] = s_ref[...]


def _rwkv_scan(r, w, k, v, kk, b, s0, prm, tb=8):
    n, t, _ = r.shape
    tb = _tile(t, tb)
    nblk = t // tb
    seq = pl.BlockSpec((None, tb, RW_WIDTH), lambda i, j: (i, j, 0))
    st = pl.BlockSpec((None, 4 * RW_HEAD, SEG_CHUNK), lambda i, j: (i, 0, 0))
    cst = pl.BlockSpec((SEG_CHUNK, SEG_CHUNK), lambda i, j: (0, 0))
    return pl.pallas_call(
        functools.partial(_rwkv_scan_kernel, tb=tb, nblk=nblk),
        grid=(n, nblk),
        in_specs=[seq] * 6 + [st, cst, cst],
        out_specs=[seq, st],
        out_shape=[jax.ShapeDtypeStruct((n, t, RW_WIDTH), F32),
                   jax.ShapeDtypeStruct((n, 4 * RW_HEAD, SEG_CHUNK), F32)],
        scratch_shapes=[pltpu.VMEM((4 * RW_HEAD, SEG_CHUNK), F32)],
        compiler_params=_cparams(("parallel", "arbitrary")),
        name="rwkv_scan",
    )(r, w, k, v, kk, b, s0, prm["bones"], prm["eye4"])


def _rwkv_chunk_kernel(r_ref, lw_ref, k_ref, v_ref, kk_ref, b_ref, y_ref, hout_ref, h_ref, *, nchunks):
    t = pl.program_id(1)
    cs = RW_CHUNK
    n = SEG_CHUNK

    @pl.when(t == 0)
    def _():
        h_ref[...] = jnp.zeros(h_ref.shape, F32)

    row = lax.broadcasted_iota(jnp.int32, (n, n), 0)
    lane = lax.broadcasted_iota(jnp.int32, (n, n), 1)
    same_head = (row // cs) == (lane // cs)
    strict = same_head & ((lane % cs) < (row % cs))
    incl = same_head & ((lane % cs) <= (row % cs))
    eye = row == lane
    lane_head = lax.broadcasted_iota(jnp.int32, (cs, n), 1) // RW_HEAD
    row_in = lax.broadcasted_iota(jnp.int32, (cs, n), 0)

    def stack(x):
        return jnp.concatenate([jnp.where(lane_head == hl, x, 0.0) for hl in range(4)], axis=0)

    def tile4(x):
        return jnp.concatenate([x] * 4, axis=1)

    def cumsum_rows(x):
        for sh in (1, 2, 4, 8, 16, 32):
            x = x + jnp.where(row_in >= sh, pltpu.roll(x, sh, 0), 0.0)
        return x

    for c in range(4):
        sl = slice(c * n, (c + 1) * n)
        lw = lw_ref[:, sl]
        l_in = cumsum_rows(lw)
        l_end = l_in[cs - 1:cs, :]
        e_neg = jnp.exp(-l_in)
        e_tail = jnp.exp(l_end - l_in)
        kq = stack(kk_ref[:, sl] * jnp.exp(l_in - lw)).astype(BF16)
        rq = stack(r_ref[:, sl] * jnp.exp(l_in)).astype(BF16)
        k_c = k_ref[:, sl]
        b_c = b_ref[:, sl]
        kh = (k_c * e_neg).astype(BF16)
        bh = (b_c * e_neg).astype(BF16)
        a_k = jnp.where(strict, tile4(_dot_nt(kq, kh)), 0.0)
        a_b = jnp.where(strict, tile4(_dot_nt(kq, bh)), 0.0)
        r_k = jnp.where(incl, tile4(_dot_nt(rq, kh)), 0.0)
        r_b = jnp.where(incl, tile4(_dot_nt(rq, bh)), 0.0)
        pw = a_b
        inv = jnp.where(eye, 1.0, 0.0) - a_b
        for _ in range(5):
            pb = pw.astype(BF16)
            pw = _dot(pb, pb)
            inv = inv + _dot(inv.astype(BF16), pw.astype(BF16))
        h0 = h_ref[c]
        h0b = h0.astype(BF16)
        v_bd = stack(v_ref[:, sl]).astype(BF16)
        u = _dot(inv.astype(BF16), (_dot(kq, h0b) + _dot(a_k.astype(BF16), v_bd)).astype(BF16))
        ub = u.astype(BF16)
        y = _dot(rq, h0b) + _dot(r_k.astype(BF16), v_bd) - _dot(r_b.astype(BF16), ub)
        y_ref[:, sl] = y[0:cs] + y[cs:2 * cs] + y[2 * cs:3 * cs] + y[3 * cs:4 * cs]
        tail = jnp.concatenate([stack(k_c * e_tail), -stack(b_c * e_tail)], axis=0)
        upd = _dot(tail.T.astype(BF16), jnp.concatenate([v_bd, ub], axis=0))
        decay_col = jnp.sum(jnp.where(eye, jnp.broadcast_to(jnp.exp(l_end), (n, n)), 0.0), axis=1, keepdims=True)
        h_ref[c] = decay_col * h0 + upd

    @pl.when(t == nchunks - 1)
    def _():
        hout_ref[...] = h_ref[...]


def _rwkv_chunked(r, lw, k, v, kk, b):
    n, t, _ = r.shape
    assert t % RW_CHUNK == 0 and RW_CHUNK == RW_HEAD
    nchunks = t // RW_CHUNK
    seq = pl.BlockSpec((None, RW_CHUNK, RW_WIDTH), lambda i, j: (i, j, 0))
    st = pl.BlockSpec((None, 4, SEG_CHUNK, SEG_CHUNK), lambda i, j: (i, 0, 0, 0))
    y, h = pl.pallas_call(
        functools.partial(_rwkv_chunk_kernel, nchunks=nchunks),
        grid=(n, nchunks),
        in_specs=[seq] * 6,
        out_specs=[seq, st],
        out_shape=[jax.ShapeDtypeStruct((n, t, RW_WIDTH), F32),
                   jax.ShapeDtypeStruct((n, 4, SEG_CHUNK, SEG_CHUNK), F32)],
        scratch_shapes=[pltpu.VMEM((4, SEG_CHUNK, SEG_CHUNK), F32)],
        compiler_params=_cparams(("parallel", "arbitrary")),
        name="rwkv_chunked",
    )(r, lw, k, v, kk, b)
    h = h.reshape(n, 4, 4, RW_HEAD, 4, RW_HEAD)
    hd = jnp.stack([h[:, :, hl, :, hl, :] for hl in range(4)], axis=2)
    return y, jnp.swapaxes(hd, 3, 4).reshape(n, RW_HEADS, RW_HEAD, RW_HEAD)


def _rwkv_post_kernel(y_ref, r_ref, k_ref, v_ref, g_ref, rk_ref, lg_ref, lb_ref, bones_ref, o_ref):
    bones = bones_ref[...]
    y = y_ref[...]
    d = y - _seg_sum(y, bones) * (1.0 / RW_HEAD)
    var = _seg_sum(d * d, bones) * (1.0 / RW_HEAD)
    yn = d * lax.rsqrt(var + GN_EPS) * lg_ref[...] + lb_ref[...]
    bonus = _seg_sum(r_ref[...] * k_ref[...] * rk_ref[...], bones) * v_ref[...]
    o_ref[...] = ((yn + bonus) * g_ref[...]).astype(o_ref.dtype)


def _rwkv_post(y, r, k, v, g, prm, tm=256):
    m = y.shape[0]
    tm = _tile(m, tm)
    big = pl.BlockSpec((tm, RW_WIDTH), lambda i: (i, 0))
    row = pl.BlockSpec((1, RW_WIDTH), lambda i: (0, 0))
    return pl.pallas_call(
        _rwkv_post_kernel,
        grid=(m // tm,),
        in_specs=[big] * 5 + [row] * 3 + [pl.BlockSpec((SEG_CHUNK, SEG_CHUNK), lambda i: (0, 0))],
        out_specs=big,
        out_shape=jax.ShapeDtypeStruct((m, RW_WIDTH), BF16),
        compiler_params=_cparams(("parallel",)),
        name="rwkv_post",
    )(y, r, k, v, g, prm["r_k"], prm["ln_g"], prm["ln_b"], prm["bones"])


def _state_to_stacked(s):
    n = s.shape[0]
    s = s.reshape(n, 4, 4, RW_HEAD, RW_HEAD)
    return jnp.transpose(s, (0, 1, 3, 2, 4)).reshape(n, 4 * RW_HEAD, SEG_CHUNK)


def _state_from_stacked(s):
    n = s.shape[0]
    s = s.reshape(n, 4, RW_HEAD, 4, RW_HEAD)
    return jnp.transpose(s, (0, 1, 3, 2, 4)).reshape(n, RW_HEADS, RW_HEAD, RW_HEAD)


def _top3_rows(sc, idx, nvalid_mask):
    big = float(sc.shape[0])
    sc = jnp.where(nvalid_mask, sc, NEG)
    sel = jnp.zeros(sc.shape, F32)
    for _ in range(MOBA_TOPK):
        mx = jnp.max(sc, axis=0, keepdims=True)
        first = jnp.min(jnp.where(sc == mx, idx, big), axis=0, keepdims=True)
        hit = (idx == first) & (mx > 0.5 * NEG)
        sel = jnp.where(hit, 1.0, sel)
        sc = jnp.where(hit, NEG, sc)
    return sel


def _moba_prompt_kernel(q_ref, k_ref, v_ref, bown_ref, bprev_ref, bfar_ref, o_ref,
                        kb_ref, vt_ref, kmh_ref, kml_ref, sel_ref, m_ref, acc_ref, s_ref, *, nb):
    i = pl.program_id(1)
    blk = MOBA_BLOCK

    @pl.when(i == 0)
    def _():
        def prep(j, c):
            rows = pl.ds(pl.multiple_of(j * blk, blk), blk)
            kj = k_ref[rows, :]
            kb_ref[rows, :] = kj.astype(BF16)
            km = jnp.sum(kj, axis=0, keepdims=True) * (1.0 / blk)
            hi, lo = _split(km)
            kmh_ref[pl.ds(j, 1), :] = hi.astype(F32)
            kml_ref[pl.ds(j, 1), :] = lo.astype(F32)
            vt_ref[j] = v_ref[rows, :].T.astype(BF16)
            return c
        lax.fori_loop(0, nb, prep, 0)

    qt = q_ref[...].T
    qh, ql = _split(qt)
    kmh = kmh_ref[...].astype(BF16)
    kml = kml_ref[...].astype(BF16)
    sc = _dot(kmh, qh) + _dot(kmh, ql) + _dot(kml, qh)
    bidx = lax.broadcasted_iota(jnp.int32, sc.shape, 0)
    sel_ref[...] = (_top3_rows(sc, bidx.astype(F32), bidx < i) - 1.0) * (-NEG)

    qs = (qt * (HEAD_DIM ** -0.5 * LOG2E)).astype(BF16)

    def keys(j, n):
        return kb_ref[pl.ds(pl.multiple_of(j * blk, blk), n * blk), :]

    def values_t(j, n):
        return vt_ref[j] if n == 1 else jnp.concatenate([vt_ref[j + g] for g in range(n)], axis=1)

    bfar = bfar_ref[0:1, 0:1]
    kidx = lax.broadcasted_iota(jnp.int32, (blk, blk), 0)
    qidx = lax.broadcasted_iota(jnp.int32, (blk, blk), 1)
    causal = kidx <= qidx

    def mask_rows(j, n):
        rows = [jnp.broadcast_to(sel_ref[pl.ds(j + g, 1), :], (blk, blk)) for g in range(n)]
        return rows[0] if n == 1 else jnp.concatenate(rows, axis=0)

    def far_scores(j, n):
        return _dot(keys(j, n), qs) + mask_rows(j, n)

    def fold(s, vt, first):
        m_new = jnp.max(s, axis=0, keepdims=True)
        if not first:
            m_old = m_ref[...]
            m_new = jnp.maximum(m_new, m_old)
        p = jnp.exp2(s - m_new)
        l_rows = jnp.broadcast_to(jnp.sum(p, axis=0, keepdims=True), (ONES_ROWS, blk))
        acc = jnp.concatenate([_dot(vt, p.astype(BF16)), l_rows], axis=0)
        if not first:
            acc = acc + jnp.exp2(m_old - m_new) * acc_ref[...]
        acc_ref[...] = acc
        m_ref[...] = m_new

    def near(ref):
        return (ref[...] - bfar) * LOG2E

    @pl.when(i == 0)
    def _():
        fold(_dot(keys(i, 1), qs) + jnp.where(causal, near(bown_ref), NEG), vt_ref[i], True)

    @pl.when(i >= 1)
    def _():
        bias2 = jnp.concatenate([near(bprev_ref) + mask_rows(i - 1, 1), jnp.where(causal, near(bown_ref), NEG)],
                                axis=0)
        fold(_dot(keys(i - 1, 2), qs) + bias2, values_t(i - 1, 2), True)

    n_far = jnp.maximum(i - 1, 0)
    n_quads = n_far // 4

    @pl.when((n_far & 2) != 0)
    def _():
        fold(far_scores(n_quads * 4, 2), values_t(n_quads * 4, 2), False)

    @pl.when((n_far & 1) != 0)
    def _():
        fold(far_scores(n_far - 1, 1), vt_ref[n_far - 1], False)

    @pl.when(n_quads > 0)
    def _():
        s_ref[...] = far_scores(0, 4)

    def far_quad(jq, c):
        s_next = far_scores(4 * jnp.minimum(jq + 1, n_quads - 1), 4)
        parts = []
        per = 4 // FAR_PARTS
        for hk in range(FAR_PARTS):
            s = s_ref[hk * per * blk:(hk + 1) * per * blk, :]
            m_loc = jnp.max(s, axis=0, keepdims=True)
            p = jnp.exp2(s - m_loc)
            l_rows = jnp.broadcast_to(jnp.sum(p, axis=0, keepdims=True), (ONES_ROWS, blk))
            parts.append((m_loc, jnp.concatenate([_dot(values_t(4 * jq + per * hk, per), p.astype(BF16)), l_rows],
                                                 axis=0)))
        m_old = m_ref[...]
        m_new = m_old
        for m_loc, _ in parts:
            m_new = jnp.maximum(m_new, m_loc)
        acc = jnp.exp2(m_old - m_new) * acc_ref[...]
        for m_loc, acc_loc in parts:
            acc = acc + jnp.exp2(m_loc - m_new) * acc_loc
        acc_ref[...] = acc
        m_ref[...] = m_new
        s_ref[...] = s_next
        return c
    lax.fori_loop(0, n_quads, far_quad, 0)

    acc = acc_ref[...]
    o_ref[...] = (acc[0:HEAD_DIM] / acc[HEAD_DIM:HEAD_DIM + 1]).T.astype(o_ref.dtype)


def _moba_prompt(z1, bias):
    t = z1.shape[0]
    assert t % MOBA_BLOCK == 0
    nb = t // MOBA_BLOCK
    blk = MOBA_BLOCK
    h8 = ATTN_HEADS
    tile = pl.BlockSpec((None, blk, blk), lambda h, i: (h, 0, 0))
    return pl.pallas_call(
        functools.partial(_moba_prompt_kernel, nb=nb),
        grid=(h8, nb),
        in_specs=[pl.BlockSpec((blk, HEAD_DIM), lambda h, i: (i, h)),
                  pl.BlockSpec((t, HEAD_DIM), lambda h, i: (0, h8 + h)),
                  pl.BlockSpec((t, HEAD_DIM), lambda h, i: (0, 2 * h8 + h)),
                  tile, tile,
                  pl.BlockSpec((None, 1, HEAD_DIM), lambda h, i: (h, 0, 0))],
        out_specs=pl.BlockSpec((blk, HEAD_DIM), lambda h, i: (i, h)),
        out_shape=jax.ShapeDtypeStruct((t, ATTN_WIDTH), BF16),
        scratch_shapes=[pltpu.VMEM((t, HEAD_DIM), BF16),
                        pltpu.VMEM((nb, HEAD_DIM, blk), BF16),
                        pltpu.VMEM((nb, HEAD_DIM), F32),
                        pltpu.VMEM((nb, HEAD_DIM), F32),
                        pltpu.VMEM((nb, blk), F32),
                        pltpu.VMEM((1, blk), F32),
                        pltpu.VMEM((HEAD_DIM + ONES_ROWS, blk), F32),
                        pltpu.VMEM((4 * blk, blk), F32)],
        compiler_params=_cparams(("arbitrary", "arbitrary")),
        name="moba_prompt",
    )(z1, z1, z1, bias["own_t"], bias["prev_t"], bias["far"])


def _diag_extract(s):
    lane = lax.broadcasted_iota(jnp.int32, (ATTN_HEADS, s.shape[1]), 1) % ATTN_HEADS
    out = jnp.zeros((ATTN_HEADS, s.shape[1]), F32)
    for hp in range(ATTN_HEADS):
        out = out + jnp.where(lane == hp, s[hp * 8:(hp + 1) * 8, :], 0.0)
    return out


def _diag_expand(p):
    lane = lax.broadcasted_iota(jnp.int32, p.shape, 1) % ATTN_HEADS
    return jnp.concatenate([jnp.where(lane == hp, p, 0.0) for hp in range(ATTN_HEADS)], axis=0)


def _class_allreduce(x, op):
    for sh in (8, 16, 32, 64):
        x = op(x, pltpu.roll(x, sh, 1))
    return x


def _fold_tiles(x, op):
    out = x[:, 0:128]
    for c in range(1, x.shape[1] // 128):
        out = op(out, x[:, c * 128:(c + 1) * 128])
    return out


def _moba_sample_kernel(pt_ref, q_ref, *refs, nblk, grp):
    k_refs = refs[:2 * grp]
    v_refs = refs[2 * grp:4 * grp]
    (kn_ref, vn_ref, blast_ref, bown_ref, bfar_ref, o_ref,
     sc_ref, bsum_ref, bmax_ref, bexp_ref, selx_ref, m_ref, li_ref, acc_ref) = refs[4 * grp:]
    ph = pl.program_id(1)
    step = pl.program_id(2)
    nsteps = nblk // grp
    ntok = 8
    rows = PAGE_SIZE * ATTN_HEADS
    nkeys = 2 * rows
    far16 = jnp.concatenate([bfar_ref[...]] * (nkeys // 128), axis=1)
    lane_blk = lax.broadcasted_iota(jnp.int32, (ntok, 128), 1) // 8

    def block_bias(j):
        return jnp.where(j == nblk - 1, blast_ref[...], far16)

    def tile16(x):
        return jnp.concatenate([x] * (nkeys // 128), axis=1)

    def compact_tile(j):
        return pl.ds(pl.multiple_of((j // 16) * 128, 128), 128)

    def put_compact(ref, j, x):
        ref[:, compact_tile(j)] = jnp.where(lane_blk == j % 16, x, ref[:, compact_tile(j)])

    def get_compact(ref, j):
        mine = jnp.where(lane_blk == j % 16, ref[:, compact_tile(j)], 0.0)
        return _class_allreduce(mine, jnp.add)

    @pl.when((ph == 0) & (step == 0))
    def _():
        bmax_ref[...] = jnp.zeros(bmax_ref.shape, F32)
        bexp_ref[...] = jnp.zeros(bexp_ref.shape, F32)

    @pl.when(ph == 0)
    def _():
        qs = (q_ref[...] * (HEAD_DIM ** -0.5)).astype(BF16)
        for g in range(grp):
            j = step * grp + g
            k0 = k_refs[2 * g][...]
            k1 = k_refs[2 * g + 1][...]
            bsum_ref[pl.ds(pl.multiple_of(j * 8, 8), 8), :] = jnp.sum(k0, axis=0) + jnp.sum(k1, axis=0)
            k2 = jnp.concatenate([k0.reshape(rows, HEAD_DIM), k1.reshape(rows, HEAD_DIM)], axis=0).astype(BF16)
            sc = _diag_extract(_dot_nt(qs, k2)) + block_bias(j)
            sc_ref[j] = sc
            bm = _class_allreduce(_fold_tiles(sc, jnp.maximum), jnp.maximum)
            be = _class_allreduce(_fold_tiles(jnp.exp(sc - tile16(bm)), jnp.add), jnp.add)
            put_compact(bmax_ref, j, bm)
            put_compact(bexp_ref, j, be)

    @pl.when((ph == 0) & (step == nsteps - 1))
    def _():
        q = q_ref[...]
        qh, ql = _split(q)
        bh, bl = _split(bsum_ref[...] * (1.0 / MOBA_BLOCK))
        scx = _diag_extract(_dot_nt(qh, bh) + _dot_nt(qh, bl) + _dot_nt(ql, bh))
        width = nblk * 8
        jidx = (lax.broadcasted_iota(jnp.int32, (ntok, width), 1) // 8).astype(F32)

        def creduce(x, op):
            y = _class_allreduce(_fold_tiles(x, op), op)
            return jnp.concatenate([y] * (width // 128), axis=1)

        selx = jnp.zeros((ntok, width), F32)
        for _ in range(MOBA_TOPK):
            mx = creduce(scx, jnp.maximum)
            first = creduce(jnp.where(scx == mx, jidx, float(nblk)), jnp.minimum)
            hit = jidx == first
            selx = jnp.where(hit, 1.0, selx)
            scx = jnp.where(hit, NEG, scx)
        selx_ref[...] = selx
        picked = selx > 0.5

        qs = (q * (HEAD_DIM ** -0.5)).astype(BF16)
        so = _diag_extract(_dot_nt(qs, kn_ref[...].astype(BF16))) + bown_ref[...]
        tq = lax.broadcasted_iota(jnp.int32, so.shape, 0)
        tk = lax.broadcasted_iota(jnp.int32, so.shape, 1) // 8
        so = jnp.concatenate([jnp.where(tk <= tq, so, NEG), jnp.full((ntok, 64), NEG, F32)], axis=1)

        bmax = bmax_ref[...]
        m = jnp.maximum(_class_allreduce(so, jnp.maximum),
                        creduce(jnp.where(picked, bmax, NEG), jnp.maximum)[:, 0:128])
        mw = jnp.concatenate([m] * (width // 128), axis=1)
        l_blocks = creduce(jnp.where(picked, bexp_ref[...] * jnp.exp(bmax - mw), 0.0), jnp.add)[:, 0:128]
        po = jnp.exp(so - m)
        li = 1.0 / (_class_allreduce(po, jnp.add) + l_blocks)
        m_ref[...] = m
        li_ref[...] = li
        pfull = _diag_expand((po * li)[:, 0:64]).astype(BF16)
        acc_ref[...] = _dot(pfull, vn_ref[...].astype(BF16))

    @pl.when(ph == 1)
    def _():
        m16 = tile16(m_ref[...])
        li16 = tile16(li_ref[...])
        acc = acc_ref[...]
        for g in range(grp):
            j = step * grp + g
            p = jnp.exp(sc_ref[j] - m16) * li16
            p = jnp.where(tile16(get_compact(selx_ref, j)) > 0.5, p, 0.0)
            v2 = jnp.concatenate([v_refs[2 * g][...].reshape(rows, HEAD_DIM),
                                  v_refs[2 * g + 1][...].reshape(rows, HEAD_DIM)], axis=0).astype(BF16)
            acc = acc + _dot(_diag_expand(p).astype(BF16), v2)
        acc_ref[...] = acc

    @pl.when((ph == 1) & (step == nsteps - 1))
    def _():
        o_ref[...] = acc_ref[...]


def _moba_sample(qht, knew, vnew, cache_k, cache_v, page_table, bias):
    nb_, n_pages = page_table.shape
    assert n_pages % PAGES_PER_BLOCK == 0 and PAGES_PER_BLOCK == 2
    nblk = n_pages // 2
    assert nblk >= MOBA_TOPK and (nblk * 8) % 128 == 0
    nkeys = 2 * PAGE_SIZE * ATTN_HEADS
    grp = SAMPLE_BLOCKS_PER_STEP
    assert nblk % grp == 0
    nsteps = nblk // grp

    def kmap(u):
        return lambda b, p, j, pt: (0, pt[b, 2 * grp * jnp.where(p == 0, j, nsteps - 1) + u], 0, 0, 0)

    def vmap_(u):
        return lambda b, p, j, pt: (0, pt[b, 2 * grp * jnp.where(p == 0, 0, j) + u], 0, 0, 0)

    page = (None, None, PAGE_SIZE, ATTN_HEADS, HEAD_DIM)
    per_b = pl.BlockSpec((None, 64, HEAD_DIM), lambda b, p, j, pt: (b, 0, 0))
    cst = lambda r, c: pl.BlockSpec((r, c), lambda b, p, j, pt: (0, 0))
    grid_spec = pltpu.PrefetchScalarGridSpec(
        num_scalar_prefetch=1,
        grid=(nb_, 2, nsteps),
        in_specs=[per_b]
                 + [pl.BlockSpec(page, kmap(u)) for u in range(2 * grp)]
                 + [pl.BlockSpec(page, vmap_(u)) for u in range(2 * grp)]
                 + [per_b, per_b, cst(8, nkeys), cst(8, 64), cst(1, 128)],
        out_specs=per_b,
        scratch_shapes=[pltpu.VMEM((nblk, 8, nkeys), F32),
                        pltpu.VMEM((nblk * 8, HEAD_DIM), F32),
                        pltpu.VMEM((8, nblk * 8), F32),
                        pltpu.VMEM((8, nblk * 8), F32),
                        pltpu.VMEM((8, nblk * 8), F32),
                        pltpu.VMEM((8, 128), F32),
                        pltpu.VMEM((8, 128), F32),
                        pltpu.VMEM((64, HEAD_DIM), F32)])
    return pl.pallas_call(
        functools.partial(_moba_sample_kernel, nblk=nblk, grp=grp),
        grid_spec=grid_spec,
        out_shape=jax.ShapeDtypeStruct((nb_, 64, HEAD_DIM), F32),
        compiler_params=_cparams(("arbitrary", "arbitrary", "arbitrary")),
        name="moba_sample",
    )(page_table, qht, *([cache_k] * (2 * grp)), *([cache_v] * (2 * grp)), knew, vnew,
      bias["last_s"], bias["own_s"], bias["far_s"])


def _bias_of_distance(dist, rel_bias):
    dist = jnp.maximum(dist, 0)
    exact = N_BUCKETS // 2
    log_ratio = jnp.log(jnp.maximum(dist, 1).astype(F32) / exact) / math.log(MAX_DISTANCE / exact)
    large = jnp.minimum(exact + (log_ratio * (N_BUCKETS - exact)).astype(jnp.int32), N_BUCKETS - 1)
    return rel_bias[jnp.where(dist < exact, dist, large)]


def _bias_tiles(rel_bias, past_len, dec_seq):
    blk = MOBA_BLOCK
    val = _bias_of_distance(jnp.arange(2 * blk), rel_bias)
    val_t = val.T

    def toeplitz(ext):
        n2 = ext.shape[1]
        n = n2 // 2
        return jnp.tile(ext, (1, n))[:, :n * (n2 - 1)].reshape(ext.shape[0], n, n2 - 1)[:, :, :n]

    own_t = toeplitz(val_t)
    prev_t = toeplitz(jnp.roll(val_t, -blk, axis=1))
    far_row = val[2 * blk - 1]
    far = jnp.broadcast_to(far_row[:, None, None], (ATTN_HEADS, 1, HEAD_DIM))
    assert past_len % blk == 0 and 2 * blk - 1 >= MAX_DISTANCE and dec_seq < blk
    last = jnp.stack([val[t + 1:t + 1 + blk][::-1] for t in range(dec_seq)])
    last_s = last.reshape(dec_seq, blk * ATTN_HEADS)
    own = jnp.stack([jnp.concatenate([val[:t + 1][::-1], jnp.zeros((dec_seq - 1 - t, ATTN_HEADS), F32)])
                     for t in range(dec_seq)])
    own_s = own.reshape(dec_seq, dec_seq * ATTN_HEADS)
    far_s = jnp.tile(far_row, 128 // ATTN_HEADS)[None, :]
    return dict(own_t=own_t, prev_t=prev_t, far=far, last_s=last_s, own_s=own_s, far_s=far_s)


def _np_consts():
    seg = np.arange(SEG_CHUNK) // RW_HEAD
    bones = (seg[:, None] == seg[None, :]).astype(np.float32)
    idx = np.arange(SEG_CHUNK) % RW_HEAD
    eye4 = (idx[:, None] == idx[None, :]).astype(np.float32)
    return bones, eye4


def _pad_rows(w, row0, total):
    return jnp.zeros((total, w.shape[1]), BF16).at[row0:row0 + w.shape[0]].set(w.astype(BF16))


def kernel(x_prompt, x_sample, cache_k, cache_v, page_table, state_wkv, state_shift, w_in, rel_bias, rw_mu, rw_w0, rw_w2, rw_a0, rw_a2, rw_g2, rw_k_k, rw_k_a, rw_r_k, rw_ln_g, rw_ln_b, w_up_attn, w_up_rwkv, w_o, ln1_g, ln1_b, w_ffn_gate, w_ffn_up, w_ffn_down, ln2_g, ln2_b):
    assert x_prompt.shape[0] == 1 and w_in.shape[0] == DEPTH == 1
    t_p = x_prompt.shape[1]
    nb_s, t_s, _ = x_sample.shape
    assert t_s == 8
    past_len = page_table.shape[1] * PAGE_SIZE

    w_in2 = w_in[0]
    w_main = w_in2[:, :MAIN_COLS].astype(BF16)
    w_lora = jnp.pad(w_in2[:, LORA_COL:GATE_COL], ((0, 0), (0, LORA_PAD - LORA_COLS))).astype(BF16)
    w_gate = w_in2[:, GATE_COL:].astype(BF16)
    wb_up_attn, wb_up_rwkv, wb_o = w_up_attn[0].astype(BF16), w_up_rwkv[0].astype(BF16), w_o[0].astype(BF16)
    wb_gate, wb_up, wb_down = w_ffn_gate[0].astype(BF16), w_ffn_up[0].astype(BF16), w_ffn_down[0].astype(BF16)
    bones_np, eye_np = _np_consts()
    mu = rw_mu[0]
    prm = dict(
        mu=mu[None, :3 * RW_WIDTH],
        mul=jnp.pad(mu[3 * RW_WIDTH:], (0, LORA_PAD - LORA_COLS))[None, :],
        w0=rw_w0, a0=rw_a0,
        w2=_pad_rows(rw_w2[0], 0, LORA_PAD),
        a2=_pad_rows(rw_a2[0], DECAY_LORA, LORA_PAD),
        g2=_pad_rows(rw_g2[0], DECAY_LORA + AAA_LORA, LORA_PAD),
        k_k=rw_k_k, k_a=rw_k_a, r_k=rw_r_k.reshape(1, RW_WIDTH), ln_g=rw_ln_g, ln_b=rw_ln_b,
        bones=jnp.asarray(bones_np, BF16), eye4=jnp.asarray(eye_np, F32))
    bias = _bias_tiles(rel_bias, past_len, t_s)

    def group(x2d, nseq, tseq, zprev, zlprev, s0, attend, tt, tb):
        z1 = _proj(x2d, w_main, 0, MAIN_COLS)
        zl = _proj(x2d, w_lora, 0, LORA_PAD)
        zg = _proj(x2d, w_gate, 0, 2 * D_MODEL)
        o_attn = attend(z1)
        r, w, k, v, kk, b, g = _rwkv_prep(z1.reshape(nseq, tseq, MAIN_COLS), zl.reshape(nseq, tseq, LORA_PAD),
                                          zprev, zlprev, prm, tt)
        if s0 is None:
            y, s_fin = _rwkv_chunked(r, w, k, v, kk, b)
        else:
            y, s_fin = _rwkv_scan(r, w, k, v, kk, b, s0, prm, tb)
            s_fin = _state_from_stacked(s_fin)
        flat = lambda u: u.reshape(nseq * tseq, RW_WIDTH)
        o_rwkv = _rwkv_post(flat(y), flat(r), flat(k), flat(v), flat(g), prm)
        mixed = _merge(o_attn, o_rwkv, zg, wb_up_attn, wb_up_rwkv)
        h = _proj_ln(mixed, wb_o, x2d, ln1_g, ln1_b, tm=1024)
        act = _ffn_up(h, wb_gate, wb_up)
        out = _proj_ln(act, wb_down, h, ln2_g, ln2_b, tm=512)
        new_k = z1[:, ATTN_WIDTH:2 * ATTN_WIDTH]
        new_v = z1[:, 2 * ATTN_WIDTH:3 * ATTN_WIDTH]
        return out, new_k, new_v, s_fin

    xp = x_prompt[0]
    yp, kp, vp, sp = group(
        xp, 1, t_p,
        jnp.zeros((1, 1, 3 * RW_WIDTH), F32), jnp.zeros((1, 1, LORA_PAD), F32),
        None, lambda z1: _moba_prompt(z1, bias), 256, 8)

    xs = x_sample.reshape(nb_s * t_s, D_MODEL)
    sh = state_shift[0]
    zprev = _proj(sh, w_main, RKV_COL, 3 * RW_WIDTH)[:, None, :]
    zlprev = _proj(sh, w_lora, 0, LORA_PAD)[:, None, :]

    def attend_sample(z1):
        q = z1[:, 0:ATTN_WIDTH].reshape(nb_s, t_s, ATTN_HEADS, HEAD_DIM)
        qht = jnp.transpose(q, (0, 2, 1, 3)).reshape(nb_s, ATTN_HEADS * t_s, HEAD_DIM)
        kn = z1[:, ATTN_WIDTH:2 * ATTN_WIDTH].reshape(nb_s, t_s * ATTN_HEADS, HEAD_DIM)
        vn = z1[:, 2 * ATTN_WIDTH:3 * ATTN_WIDTH].reshape(nb_s, t_s * ATTN_HEADS, HEAD_DIM)
        o = _moba_sample(qht, kn, vn, cache_k, cache_v, page_table, bias)
        o = jnp.transpose(o.reshape(nb_s, ATTN_HEADS, t_s, HEAD_DIM), (0, 2, 1, 3))
        return o.reshape(nb_s * t_s, ATTN_WIDTH).astype(BF16)

    ys, ks, vs, ss = group(xs, nb_s, t_s, zprev, zlprev, _state_to_stacked(state_wkv[0]),
                           attend_sample, 8, 8)

    return (yp[None], ys.reshape(nb_s, t_s, D_MODEL),
            kp.reshape(1, 1, t_p, ATTN_HEADS, HEAD_DIM), vp.reshape(1, 1, t_p, ATTN_HEADS, HEAD_DIM),
            sp[None], xp[None, -1:, :],
            ks.reshape(1, nb_s, t_s, ATTN_HEADS, HEAD_DIM), vs.reshape(1, nb_s, t_s, ATTN_HEADS, HEAD_DIM),
            ss[None], x_sample[None, :, -1, :])
```

```python
import functools
import math

import numpy as np
import jax
import jax.numpy as jnp
from jax import lax
from jax.experimental import pallas as pl
from jax.experimental.pallas import tpu as pltpu

F32 = jnp.float32
BF16 = jnp.bfloat16

D_MODEL = 2048
HEAD_DIM = 128
ATTN_HEADS = D_MODEL // 256
ATTN_WIDTH = ATTN_HEADS * HEAD_DIM
MOBA_BLOCK = 256
MOBA_TOPK = 3
N_BUCKETS = 32
MAX_DISTANCE = 128
PAGE_SIZE = 128
RW_HEAD = 64
RW_HEADS = D_MODEL // 128
RW_WIDTH = RW_HEADS * RW_HEAD
DECAY_LORA = 96
AAA_LORA = 96
GATE_LORA = 256
LORA_COLS = DECAY_LORA + AAA_LORA + GATE_LORA
LORA_PAD = 512
GN_EPS = RW_HEAD * 1e-5
D_FF = 5632
LN_EPS = 1e-5
LN_ROWS = 128
DEPTH = 1
DEEPNORM_ALPHA = (2 * DEPTH) ** 0.25

RKV_COL = 3 * ATTN_WIDTH
LORA_COL = RKV_COL + 3 * RW_WIDTH
GATE_COL = LORA_COL + LORA_COLS
MAIN_COLS = LORA_COL

NEG = -1e30
LOG2E = math.log2(math.e)
FAR_PARTS = 4
ONES_ROWS = 8
SEG_CHUNK = 256
PAGES_PER_BLOCK = MOBA_BLOCK // PAGE_SIZE
SAMPLE_BLOCKS_PER_STEP = 8
SAMPLE_LOOKAHEAD = 2
RW_CHUNK = 64
RW_GROUP = 4
CHUNKS_PER_STEP = 2
VMEM_LIMIT = 56 * 1024 * 1024


def _cparams(sem):
    return pltpu.CompilerParams(dimension_semantics=sem, vmem_limit_bytes=VMEM_LIMIT)


def _dot(a, b):
    return jnp.dot(a, b, preferred_element_type=F32)


def _dot_nt(a, b):
    return lax.dot_general(a, b, (((1,), (1,)), ((), ())), preferred_element_type=F32)


def _split(x):
    hi = x.astype(BF16)
    lo = (x - hi.astype(F32)).astype(BF16)
    return hi, lo


def _sigmoid(x):
    return 1.0 / (1.0 + jnp.exp(-x))


def _tile(m, pref):
    t = min(m, pref)
    assert m % t == 0, (m, pref)
    return t


def _proj_kernel(x_ref, w_ref, o_ref, xb_ref):
    @pl.when(pl.program_id(1) == 0)
    def _():
        xb_ref[...] = x_ref[...].astype(BF16)

    o_ref[...] = _dot(xb_ref[...], w_ref[...].astype(BF16)).astype(o_ref.dtype)


def _proj(x, w, col0, ncols, tm=1024, tn=512):
    m, k = x.shape
    tm = _tile(m, tm)
    tn = _tile(ncols, tn)
    assert col0 % tn == 0
    c0 = col0 // tn
    return pl.pallas_call(
        _proj_kernel,
        grid=(m // tm, ncols // tn),
        in_specs=[pl.BlockSpec((tm, k), lambda i, j: (i, 0)),
                  pl.BlockSpec((k, tn), lambda i, j: (0, c0 + j))],
        out_specs=pl.BlockSpec((tm, tn), lambda i, j: (i, j)),
        out_shape=jax.ShapeDtypeStruct((m, ncols), F32),
        scratch_shapes=[pltpu.VMEM((tm, k), BF16)],
        compiler_params=_cparams(("parallel", "arbitrary")),
        name="proj",
    )(x, w)


def _merge_kernel(oa_ref, or_ref, wa_ref, wr_ref, ga_ref, gr_ref, o_ref):
    a = _dot(oa_ref[...], wa_ref[...].astype(BF16))
    r = _dot(or_ref[...], wr_ref[...].astype(BF16))
    o_ref[...] = (_sigmoid(ga_ref[...]) * a + _sigmoid(gr_ref[...]) * r).astype(o_ref.dtype)


def _merge(o_attn, o_rwkv, zg, w_up_attn, w_up_rwkv, tm=1024, tn=512):
    m = o_attn.shape[0]
    tm = _tile(m, tm)
    nj = D_MODEL // tn
    return pl.pallas_call(
        _merge_kernel,
        grid=(m // tm, nj),
        in_specs=[pl.BlockSpec((tm, ATTN_WIDTH), lambda i, j: (i, 0)),
                  pl.BlockSpec((tm, RW_WIDTH), lambda i, j: (i, 0)),
                  pl.BlockSpec((ATTN_WIDTH, tn), lambda i, j: (0, j)),
                  pl.BlockSpec((RW_WIDTH, tn), lambda i, j: (0, j)),
                  pl.BlockSpec((tm, tn), lambda i, j: (i, j)),
                  pl.BlockSpec((tm, tn), lambda i, j: (i, j + nj))],
        out_specs=pl.BlockSpec((tm, tn), lambda i, j: (i, j)),
        out_shape=jax.ShapeDtypeStruct((m, D_MODEL), BF16),
        compiler_params=_cparams(("parallel", "arbitrary")),
        name="merge",
    )(o_attn, o_rwkv, w_up_attn, w_up_rwkv, zg, zg)


def _proj_ln_kernel(m_ref, w_ref, x_ref, g_ref, b_ref, o_ref, *, tn, nj):
    j = pl.program_id(1)
    col = pl.multiple_of(j * tn, tn)
    mb = m_ref[...].astype(BF16)
    o_ref[:, pl.ds(col, tn)] = DEEPNORM_ALPHA * x_ref[...] + _dot(mb, w_ref[...].astype(BF16))

    @pl.when(j == nj - 1)
    def _():
        ln_rows = min(LN_ROWS, o_ref.shape[0])
        assert o_ref.shape[0] % ln_rows == 0

        def norm_rows(c, carry):
            rows = pl.ds(pl.multiple_of(c * ln_rows, ln_rows), ln_rows)
            y = o_ref[rows, :]
            mu = jnp.mean(y, axis=-1, keepdims=True)
            d = y - mu
            var = jnp.mean(d * d, axis=-1, keepdims=True)
            o_ref[rows, :] = d * lax.rsqrt(var + LN_EPS) * g_ref[...] + b_ref[...]
            return carry
        lax.fori_loop(0, o_ref.shape[0] // ln_rows, norm_rows, 0)


def _proj_ln(mat, w, x, g, b, tm=512, tn=256):
    m, k = mat.shape
    tm = _tile(m, tm)
    nj = D_MODEL // tn
    return pl.pallas_call(
        functools.partial(_proj_ln_kernel, tn=tn, nj=nj),
        grid=(m // tm, nj),
        in_specs=[pl.BlockSpec((tm, k), lambda i, j: (i, 0)),
                  pl.BlockSpec((k, tn), lambda i, j: (0, j)),
                  pl.BlockSpec((tm, tn), lambda i, j: (i, j)),
                  pl.BlockSpec((1, D_MODEL), lambda i, j: (0, 0)),
                  pl.BlockSpec((1, D_MODEL), lambda i, j: (0, 0))],
        out_specs=pl.BlockSpec((tm, D_MODEL), lambda i, j: (i, 0)),
        out_shape=jax.ShapeDtypeStruct((m, D_MODEL), F32),
        compiler_params=_cparams(("parallel", "arbitrary")),
        name="proj_ln",
    )(mat, w, x, g, b)


def _ffn_up_kernel(h_ref, wg_ref, wu_ref, o_ref, hb_ref):
    @pl.when(pl.program_id(1) == 0)
    def _():
        hb_ref[...] = h_ref[...].astype(BF16)

    hb = hb_ref[...]
    a = _dot(hb, wg_ref[...].astype(BF16))
    u = _dot(hb, wu_ref[...].astype(BF16))
    o_ref[...] = (a * _sigmoid(a) * u).astype(o_ref.dtype)


def _ffn_up(h, wg, wu, tm=1024, tn=256):
    m = h.shape[0]
    tm = _tile(m, tm)
    return pl.pallas_call(
        _ffn_up_kernel,
        grid=(m // tm, D_FF // tn),
        in_specs=[pl.BlockSpec((tm, D_MODEL), lambda i, j: (i, 0)),
                  pl.BlockSpec((D_MODEL, tn), lambda i, j: (0, j)),
                  pl.BlockSpec((D_MODEL, tn), lambda i, j: (0, j))],
        out_specs=pl.BlockSpec((tm, tn), lambda i, j: (i, j)),
        out_shape=jax.ShapeDtypeStruct((m, D_FF), BF16),
        scratch_shapes=[pltpu.VMEM((tm, D_MODEL), BF16)],
        compiler_params=_cparams(("parallel", "arbitrary")),
        name="ffn_up",
    )(h, wg, wu)


def _seg_sum(x, bones):
    outs = []
    for c in range(x.shape[1] // SEG_CHUNK):
        hi, lo = _split(x[:, c * SEG_CHUNK:(c + 1) * SEG_CHUNK])
        outs.append(_dot(hi, bones) + _dot(lo, bones))
    return jnp.concatenate(outs, axis=1)


def _shifted(z, prev):
    zs = pltpu.roll(z, 1, 0)
    row = lax.broadcasted_iota(jnp.int32, z.shape, 0)
    return jnp.where(row == 0, prev, zs)


def _rwkv_prep_kernel(z_ref, zl_ref, zp_ref, zlp_ref, mu_ref, mul_ref, w0_ref, a0_ref, w2_ref, a2_ref,
                      g2_ref, kkw_ref, kaw_ref, bones_ref,
                      r_o, w_o, k_o, v_o, kk_o, b_o, g_o, prev_ref, prevl_ref):
    @pl.when(pl.program_id(1) == 0)
    def _():
        prev_ref[...] = zp_ref[...]
        prevl_ref[...] = zlp_ref[...]

    z = z_ref[...]
    zl = zl_ref[...]
    tt = z.shape[0]
    zm = z + (_shifted(z, prev_ref[...]) - z) * mu_ref[...]
    zlm = zl + (_shifted(zl, prevl_ref[...]) - zl) * mul_ref[...]
    prev_ref[...] = z[tt - 1:tt, :]
    prevl_ref[...] = zl[tt - 1:tt, :]

    r = zm[:, 0:RW_WIDTH]
    k = zm[:, RW_WIDTH:2 * RW_WIDTH]
    v = zm[:, 2 * RW_WIDTH:3 * RW_WIDTH]
    xw = w0_ref[...] + _dot(jnp.tanh(zlm).astype(BF16), w2_ref[...])
    nx = -xw
    softplus = jnp.maximum(nx, 0.0) + jnp.log(1.0 + jnp.exp(-jnp.abs(nx)))
    log_decay = -jnp.exp(-softplus - 0.5)
    a = _sigmoid(a0_ref[...] + _dot(zlm.astype(BF16), a2_ref[...]))
    g = _dot(_sigmoid(zlm).astype(BF16), g2_ref[...])
    kk = k * kkw_ref[...]
    ssq = _seg_sum(kk * kk, bones_ref[...])
    kk = kk / jnp.maximum(jnp.sqrt(ssq), 1e-12)
    r_o[...] = r
    w_o[...] = log_decay
    k_o[...] = k * (1.0 + (a - 1.0) * kaw_ref[...])
    v_o[...] = v
    kk_o[...] = kk
    b_o[...] = kk * a
    g_o[...] = g


def _rwkv_prep(z1, zl, zprev, zlprev, prm, tt):
    n, t, _ = z1.shape
    tt = _tile(t, tt)
    row = lambda c: pl.BlockSpec((1, c), lambda i, j: (0, 0))
    mat = lambda r, c: pl.BlockSpec((r, c), lambda i, j: (0, 0))
    seq = lambda c: pl.BlockSpec((None, tt, c), lambda i, j: (i, j, 0))
    outs = pl.pallas_call(
        _rwkv_prep_kernel,
        grid=(n, t // tt),
        in_specs=[pl.BlockSpec((None, tt, 3 * RW_WIDTH), lambda i, j: (i, j, 1)),
                  seq(LORA_PAD),
                  pl.BlockSpec((None, 1, 3 * RW_WIDTH), lambda i, j: (i, 0, 0)),
                  pl.BlockSpec((None, 1, LORA_PAD), lambda i, j: (i, 0, 0)),
                  row(3 * RW_WIDTH), row(LORA_PAD), row(RW_WIDTH), row(RW_WIDTH),
                  mat(LORA_PAD, RW_WIDTH), mat(LORA_PAD, RW_WIDTH), mat(LORA_PAD, RW_WIDTH),
                  row(RW_WIDTH), row(RW_WIDTH), mat(SEG_CHUNK, SEG_CHUNK)],
        out_specs=[seq(RW_WIDTH)] * 7,
        out_shape=[jax.ShapeDtypeStruct((n, t, RW_WIDTH), F32)] * 7,
        scratch_shapes=[pltpu.VMEM((1, 3 * RW_WIDTH), F32), pltpu.VMEM((1, LORA_PAD), F32)],
        compiler_params=_cparams(("parallel", "arbitrary")),
        name="rwkv_prep",
    )(z1, zl, zprev, zlprev, prm["mu"], prm["mul"], prm["w0"], prm["a0"], prm["w2"], prm["a2"],
      prm["g2"], prm["k_k"], prm["k_a"], prm["bones"])
    return outs


def _rwkv_scan_kernel(r_ref, w_ref, k_ref, v_ref, kk_ref, b_ref, s0_ref, bones_ref, eye_ref,
                      y_ref, sout_ref, s_ref, *, tb, nblk):
    t = pl.program_id(1)

    @pl.when(t == 0)
    def _():
        s_ref[...] = s0_ref[...]

    bones = bones_ref[...]
    eye = eye_ref[...]

    def bc(ref, s):
        return jnp.concatenate(
            [jnp.broadcast_to(ref[pl.ds(s, 1), c * SEG_CHUNK:(c + 1) * SEG_CHUNK], (RW_HEAD, SEG_CHUNK))
             for c in range(4)], axis=0)

    def step(s, carry):
        st = s_ref[...]
        hi, lo = _split(st * bc(kk_ref, s))
        sa = _dot(hi, bones) + _dot(lo, bones)
        vcol = _dot((bc(v_ref, s) * eye).astype(BF16), bones)
        sn = st * jnp.exp(bc(w_ref, s)) - sa * bc(b_ref, s) + vcol * bc(k_ref, s)
        s_ref[...] = sn
        yb = _dot((sn * bc(r_ref, s)).astype(BF16), bones)
        y4 = jnp.sum((yb * eye).reshape(4, RW_HEAD, SEG_CHUNK), axis=1)
        for c in range(4):
            y_ref[pl.ds(s, 1), c * SEG_CHUNK:(c + 1) * SEG_CHUNK] = y4[c:c + 1, :]
        return carry

    lax.fori_loop(0, tb, step, 0)

    @pl.when(t == nblk - 1)
    def _():
        sout_ref[...] = s_ref[...]


def _rwkv_scan(r, w, k, v, kk, b, s0, prm, tb=8):
    n, t, _ = r.shape
    tb = _tile(t, tb)
    nblk = t // tb
    seq = pl.BlockSpec((None, tb, RW_WIDTH), lambda i, j: (i, j, 0))
    st = pl.BlockSpec((None, 4 * RW_HEAD, SEG_CHUNK), lambda i, j: (i, 0, 0))
    cst = pl.BlockSpec((SEG_CHUNK, SEG_CHUNK), lambda i, j: (0, 0))
    return pl.pallas_call(
        functools.partial(_rwkv_scan_kernel, tb=tb, nblk=nblk),
        grid=(n, nblk),
        in_specs=[seq] * 6 + [st, cst, cst],
        out_specs=[seq, st],
        out_shape=[jax.ShapeDtypeStruct((n, t, RW_WIDTH), F32),
                   jax.ShapeDtypeStruct((n, 4 * RW_HEAD, SEG_CHUNK), F32)],
        scratch_shapes=[pltpu.VMEM((4 * RW_HEAD, SEG_CHUNK), F32)],
        compiler_params=_cparams(("parallel", "arbitrary")),
        name="rwkv_scan",
    )(r, w, k, v, kk, b, s0, prm["bones"], prm["eye4"])


def _rwkv_chunk_kernel(r_ref, lw_ref, k_ref, v_ref, kk_ref, b_ref, y_ref, hout_ref, h_ref, *, nchunks):
    t = pl.program_id(1)
    cs = RW_CHUNK
    n = RW_GROUP * RW_HEAD
    ngroups = RW_HEADS // RW_GROUP

    @pl.when(t == 0)
    def _():
        h_ref[...] = jnp.zeros(h_ref.shape, F32)

    row = lax.broadcasted_iota(jnp.int32, (n, n), 0)
    lane = lax.broadcasted_iota(jnp.int32, (n, n), 1)
    same_head = (row // cs) == (lane // cs)
    strict = same_head & ((lane % cs) < (row % cs))
    incl = same_head & ((lane % cs) <= (row % cs))
    eye = row == lane
    lane_head = lax.broadcasted_iota(jnp.int32, (cs, n), 1) // RW_HEAD
    row_in = lax.broadcasted_iota(jnp.int32, (cs, n), 0)

    def stack(x):
        return jnp.concatenate([jnp.where(lane_head == hl, x, 0.0) for hl in range(RW_GROUP)], axis=0)

    def tile4(x):
        return jnp.concatenate([x] * RW_GROUP, axis=1)

    def cumsum_rows(x):
        for sh in (1, 2, 4, 8, 16, 32):
            x = x + jnp.where(row_in >= sh, pltpu.roll(x, sh, 0), 0.0)
        return x

    units = [(ci, c) for ci in range(CHUNKS_PER_STEP) for c in range(ngroups)]
    pre = {}
    for ci, c in units:
        sl = slice(c * n, (c + 1) * n)
        rs = slice(ci * cs, (ci + 1) * cs)
        lw = lw_ref[rs, sl]
        l_in = cumsum_rows(lw)
        l_end = l_in[cs - 1:cs, :]
        e_neg = jnp.exp(-l_in)
        e_tail = jnp.exp(l_end - l_in)
        kq = stack(kk_ref[rs, sl] * jnp.exp(l_in - lw)).astype(BF16)
        rq = stack(r_ref[rs, sl] * jnp.exp(l_in)).astype(BF16)
        k_c = k_ref[rs, sl]
        b_c = b_ref[rs, sl]
        kh = (k_c * e_neg).astype(BF16)
        bh = (b_c * e_neg).astype(BF16)
        tail = jnp.concatenate([stack(k_c * e_tail), -stack(b_c * e_tail)], axis=0)
        pre[ci, c] = dict(
            sl=sl, rs=rs, kq=kq, rq=rq,
            a_k=jnp.where(strict, tile4(_dot_nt(kq, kh)), 0.0).astype(BF16),
            a_b=jnp.where(strict, tile4(_dot_nt(kq, bh)), 0.0),
            r_k=jnp.where(incl, tile4(_dot_nt(rq, kh)), 0.0).astype(BF16),
            r_b=jnp.where(incl, tile4(_dot_nt(rq, bh)), 0.0).astype(BF16),
            v_bd=stack(v_ref[rs, sl]).astype(BF16),
            tail_t=tail.T.astype(BF16),
            decay_col=jnp.sum(jnp.where(eye, jnp.broadcast_to(jnp.exp(l_end), (n, n)), 0.0), axis=1,
                              keepdims=True))
    pw = {u_: pre[u_]["a_b"] for u_ in units}
    inv = {u_: jnp.where(eye, 1.0, 0.0) - pre[u_]["a_b"] for u_ in units}
    for _ in range(5):
        for u_ in units:
            pb = pw[u_].astype(BF16)
            pw[u_] = _dot(pb, pb)
        for u_ in units:
            inv[u_] = inv[u_] + _dot(inv[u_].astype(BF16), pw[u_].astype(BF16))
    av = {u_: _dot(pre[u_]["a_k"], pre[u_]["v_bd"]) for u_ in units}
    invb = {u_: inv[u_].astype(BF16) for u_ in units}
    yv = {u_: _dot(pre[u_]["r_k"], pre[u_]["v_bd"]) for u_ in units}
    groups = range(ngroups)
    for ci in range(CHUNKS_PER_STEP):
        h0 = {c: h_ref[c] for c in groups}
        h0b = {c: h0[c].astype(BF16) for c in groups}
        rhs = {c: (_dot(pre[ci, c]["kq"], h0b[c]) + av[ci, c]).astype(BF16) for c in groups}
        yh = {c: _dot(pre[ci, c]["rq"], h0b[c]) for c in groups}
        ub = {c: _dot(invb[ci, c], rhs[c]).astype(BF16) for c in groups}
        for c in groups:
            p_ = pre[ci, c]
            y = yh[c] + yv[ci, c] - _dot(p_["r_b"], ub[c])
            y_ref[p_["rs"], p_["sl"]] = sum(y[hl * cs:(hl + 1) * cs] for hl in range(1, RW_GROUP)) + y[0:cs]
        for c in groups:
            p_ = pre[ci, c]
            upd = _dot(p_["tail_t"], jnp.concatenate([p_["v_bd"], ub[c]], axis=0))
            h_ref[c] = p_["decay_col"] * h0[c] + upd

    @pl.when(t == nchunks - 1)
    def _():
        hout_ref[...] = h_ref[...]


def _rwkv_chunked(r, lw, k, v, kk, b):
    n, t, _ = r.shape
    rows = RW_CHUNK * CHUNKS_PER_STEP
    assert t % rows == 0 and RW_CHUNK == RW_HEAD
    nchunks = t // rows
    seq = pl.BlockSpec((None, rows, RW_WIDTH), lambda i, j: (i, j, 0))
    gw = RW_GROUP * RW_HEAD
    ngroups = RW_HEADS // RW_GROUP
    st = pl.BlockSpec((None, ngroups, gw, gw), lambda i, j: (i, 0, 0, 0))
    y, h = pl.pallas_call(
        functools.partial(_rwkv_chunk_kernel, nchunks=nchunks),
        grid=(n, nchunks),
        in_specs=[seq] * 6,
        out_specs=[seq, st],
        out_shape=[jax.ShapeDtypeStruct((n, t, RW_WIDTH), F32),
                   jax.ShapeDtypeStruct((n, ngroups, gw, gw), F32)],
        scratch_shapes=[pltpu.VMEM((ngroups, gw, gw), F32)],
        compiler_params=_cparams(("parallel", "arbitrary")),
        name="rwkv_chunked",
    )(r, lw, k, v, kk, b)
    h = h.reshape(n, ngroups, RW_GROUP, RW_HEAD, RW_GROUP, RW_HEAD)
    hd = jnp.stack([h[:, :, hl, :, hl, :] for hl in range(RW_GROUP)], axis=2)
    return y, jnp.swapaxes(hd, 3, 4).reshape(n, RW_HEADS, RW_HEAD, RW_HEAD)


def _rwkv_post_kernel(y_ref, r_ref, k_ref, v_ref, g_ref, rk_ref, lg_ref, lb_ref, bones_ref, o_ref):
    bones = bones_ref[...]
    y = y_ref[...]
    d = y - _seg_sum(y, bones) * (1.0 / RW_HEAD)
    var = _seg_sum(d * d, bones) * (1.0 / RW_HEAD)
    yn = d * lax.rsqrt(var + GN_EPS) * lg_ref[...] + lb_ref[...]
    bonus = _seg_sum(r_ref[...] * k_ref[...] * rk_ref[...], bones) * v_ref[...]
    o_ref[...] = ((yn + bonus) * g_ref[...]).astype(o_ref.dtype)


def _rwkv_post(y, r, k, v, g, prm, tm=256):
    m = y.shape[0]
    tm = _tile(m, tm)
    big = pl.BlockSpec((tm, RW_WIDTH), lambda i: (i, 0))
    row = pl.BlockSpec((1, RW_WIDTH), lambda i: (0, 0))
    return pl.pallas_call(
        _rwkv_post_kernel,
        grid=(m // tm,),
        in_specs=[big] * 5 + [row] * 3 + [pl.BlockSpec((SEG_CHUNK, SEG_CHUNK), lambda i: (0, 0))],
        out_specs=big,
        out_shape=jax.ShapeDtypeStruct((m, RW_WIDTH), BF16),
        compiler_params=_cparams(("parallel",)),
        name="rwkv_post",
    )(y, r, k, v, g, prm["r_k"], prm["ln_g"], prm["ln_b"], prm["bones"])


def _state_to_stacked(s):
    n = s.shape[0]
    s = s.reshape(n, 4, 4, RW_HEAD, RW_HEAD)
    return jnp.transpose(s, (0, 1, 3, 2, 4)).reshape(n, 4 * RW_HEAD, SEG_CHUNK)


def _state_from_stacked(s):
    n = s.shape[0]
    s = s.reshape(n, 4, RW_HEAD, 4, RW_HEAD)
    return jnp.transpose(s, (0, 1, 3, 2, 4)).reshape(n, RW_HEADS, RW_HEAD, RW_HEAD)


def _top3_rows(sc, idx, nvalid_mask):
    big = float(sc.shape[0])
    sc = jnp.where(nvalid_mask, sc, NEG)
    sel = jnp.zeros(sc.shape, F32)
    for _ in range(MOBA_TOPK):
        mx = jnp.max(sc, axis=0, keepdims=True)
        first = jnp.min(jnp.where(sc == mx, idx, big), axis=0, keepdims=True)
        hit = (idx == first) & (mx > 0.5 * NEG)
        sel = jnp.where(hit, 1.0, sel)
        sc = jnp.where(hit, NEG, sc)
    return sel


def _moba_prompt_kernel(q_ref, k_ref, v_ref, bown_ref, bprev_ref, bfar_ref, o_ref,
                        kb_ref, vt_ref, kmh_ref, kml_ref, sel_ref, m_ref, acc_ref, s_ref, *, nb):
    i = pl.program_id(1)
    blk = MOBA_BLOCK

    @pl.when(i == 0)
    def _():
        def prep(j, c):
            rows = pl.ds(pl.multiple_of(j * blk, blk), blk)
            kj = k_ref[rows, :]
            kb_ref[rows, :] = kj.astype(BF16)
            km = jnp.sum(kj, axis=0, keepdims=True) * (1.0 / blk)
            hi, lo = _split(km)
            kmh_ref[pl.ds(j, 1), :] = hi.astype(F32)
            kml_ref[pl.ds(j, 1), :] = lo.astype(F32)
            vt_ref[j] = v_ref[rows, :].T.astype(BF16)
            return c
        lax.fori_loop(0, nb, prep, 0)

    qt = q_ref[...].T
    qh, ql = _split(qt)
    kmh = kmh_ref[...].astype(BF16)
    kml = kml_ref[...].astype(BF16)
    sc = _dot(kmh, qh) + _dot(kmh, ql) + _dot(kml, qh)
    bidx = lax.broadcasted_iota(jnp.int32, sc.shape, 0)
    sel_ref[...] = (_top3_rows(sc, bidx.astype(F32), bidx < i) - 1.0) * (-NEG)

    qs = (qt * (HEAD_DIM ** -0.5 * LOG2E)).astype(BF16)

    def keys(j, n):
        return kb_ref[pl.ds(pl.multiple_of(j * blk, blk), n * blk), :]

    def values_t(j, n):
        return vt_ref[j] if n == 1 else jnp.concatenate([vt_ref[j + g] for g in range(n)], axis=1)

    bfar = bfar_ref[0:1, 0:1]
    kidx = lax.broadcasted_iota(jnp.int32, (blk, blk), 0)
    qidx = lax.broadcasted_iota(jnp.int32, (blk, blk), 1)
    causal = kidx <= qidx

    def mask_rows(j, n):
        rows = [jnp.broadcast_to(sel_ref[pl.ds(j + g, 1), :], (blk, blk)) for g in range(n)]
        return rows[0] if n == 1 else jnp.concatenate(rows, axis=0)

    def far_scores(j, n):
        return _dot(keys(j, n), qs) + mask_rows(j, n)

    def fold(s, vt, first):
        m_new = jnp.max(s, axis=0, keepdims=True)
        if not first:
            m_old = m_ref[...]
            m_new = jnp.maximum(m_new, m_old)
        p = jnp.exp2(s - m_new)
        l_rows = jnp.broadcast_to(jnp.sum(p, axis=0, keepdims=True), (ONES_ROWS, blk))
        acc = jnp.concatenate([_dot(vt, p.astype(BF16)), l_rows], axis=0)
        if not first:
            acc = acc + jnp.exp2(m_old - m_new) * acc_ref[...]
        acc_ref[...] = acc
        m_ref[...] = m_new

    def near(ref):
        return (ref[...] - bfar) * LOG2E

    @pl.when(i == 0)
    def _():
        fold(_dot(keys(i, 1), qs) + jnp.where(causal, near(bown_ref), NEG), vt_ref[i], True)

    @pl.when(i >= 1)
    def _():
        bias2 = jnp.concatenate([near(bprev_ref) + mask_rows(i - 1, 1), jnp.where(causal, near(bown_ref), NEG)],
                                axis=0)
        fold(_dot(keys(i - 1, 2), qs) + bias2, values_t(i - 1, 2), True)

    n_far = jnp.maximum(i - 1, 0)
    n_quads = n_far // 4

    @pl.when((n_far & 2) != 0)
    def _():
        fold(far_scores(n_quads * 4, 2), values_t(n_quads * 4, 2), False)

    @pl.when((n_far & 1) != 0)
    def _():
        fold(far_scores(n_far - 1, 1), vt_ref[n_far - 1], False)

    @pl.when(n_quads > 0)
    def _():
        s_ref[...] = far_scores(0, 4)

    def far_quad(jq, c):
        s_next = far_scores(4 * jnp.minimum(jq + 1, n_quads - 1), 4)
        parts = []
        per = 4 // FAR_PARTS
        for hk in range(FAR_PARTS):
            s = s_ref[hk * per * blk:(hk + 1) * per * blk, :]
            m_loc = jnp.max(s, axis=0, keepdims=True)
            p = jnp.exp2(s - m_loc)
            l_rows = jnp.broadcast_to(jnp.sum(p, axis=0, keepdims=True), (ONES_ROWS, blk))
            parts.append((m_loc, jnp.concatenate([_dot(values_t(4 * jq + per * hk, per), p.astype(BF16)), l_rows],
                                                 axis=0)))
        m_old = m_ref[...]
        m_new = m_old
        for m_loc, _ in parts:
            m_new = jnp.maximum(m_new, m_loc)
        acc = jnp.exp2(m_old - m_new) * acc_ref[...]
        for m_loc, acc_loc in parts:
            acc = acc + jnp.exp2(m_loc - m_new) * acc_loc
        acc_ref[...] = acc
        m_ref[...] = m_new
        s_ref[...] = s_next
        return c
    lax.fori_loop(0, n_quads, far_quad, 0)

    acc = acc_ref[...]
    o_ref[...] = (acc[0:HEAD_DIM] / acc[HEAD_DIM:HEAD_DIM + 1]).T.astype(o_ref.dtype)


def _moba_prompt(z1, bias):
    t = z1.shape[0]
    assert t % MOBA_BLOCK == 0
    nb = t // MOBA_BLOCK
    blk = MOBA_BLOCK
    h8 = ATTN_HEADS
    tile = pl.BlockSpec((None, blk, blk), lambda h, i: (h, 0, 0))
    return pl.pallas_call(
        functools.partial(_moba_prompt_kernel, nb=nb),
        grid=(h8, nb),
        in_specs=[pl.BlockSpec((blk, HEAD_DIM), lambda h, i: (i, h)),
                  pl.BlockSpec((t, HEAD_DIM), lambda h, i: (0, h8 + h)),
                  pl.BlockSpec((t, HEAD_DIM), lambda h, i: (0, 2 * h8 + h)),
                  tile, tile,
                  pl.BlockSpec((None, 1, HEAD_DIM), lambda h, i: (h, 0, 0))],
        out_specs=pl.BlockSpec((blk, HEAD_DIM), lambda h, i: (i, h)),
        out_shape=jax.ShapeDtypeStruct((t, ATTN_WIDTH), BF16),
        scratch_shapes=[pltpu.VMEM((t, HEAD_DIM), BF16),
                        pltpu.VMEM((nb, HEAD_DIM, blk), BF16),
                        pltpu.VMEM((nb, HEAD_DIM), F32),
                        pltpu.VMEM((nb, HEAD_DIM), F32),
                        pltpu.VMEM((nb, blk), F32),
                        pltpu.VMEM((1, blk), F32),
                        pltpu.VMEM((HEAD_DIM + ONES_ROWS, blk), F32),
                        pltpu.VMEM((4 * blk, blk), F32)],
        compiler_params=_cparams(("arbitrary", "arbitrary")),
        name="moba_prompt",
    )(z1, z1, z1, bias["own_t"], bias["prev_t"], bias["far"])


def _diag_extract(s):
    lane = lax.broadcasted_iota(jnp.int32, (ATTN_HEADS, s.shape[1]), 1) % ATTN_HEADS
    out = jnp.zeros((ATTN_HEADS, s.shape[1]), F32)
    for hp in range(ATTN_HEADS):
        out = out + jnp.where(lane == hp, s[hp * 8:(hp + 1) * 8, :], 0.0)
    return out


def _diag_expand(p):
    lane = lax.broadcasted_iota(jnp.int32, p.shape, 1) % ATTN_HEADS
    return jnp.concatenate([jnp.where(lane == hp, p, 0.0) for hp in range(ATTN_HEADS)], axis=0)


def _class_allreduce(x, op):
    for sh in (8, 16, 32, 64):
        x = op(x, pltpu.roll(x, sh, 1))
    return x


def _class_allreduce_many(xs, op):
    for sh in (8, 16, 32, 64):
        xs = [op(x, pltpu.roll(x, sh, 1)) for x in xs]
    return xs


def _fold_tiles(x, op):
    out = x[:, 0:128]
    for c in range(1, x.shape[1] // 128):
        out = op(out, x[:, c * 128:(c + 1) * 128])
    return out


def _moba_sample_kernel(pt_ref, q_ref, ck_ref, cv_ref, kn_ref, vn_ref, blast_ref, bown_ref, bfar_ref, o_ref,
                        sc_ref, bsum_ref, bmax_ref, bexp_ref, selx_ref, m_ref, li_ref, acc_ref,
                        page_ref, sem_ref, *, nblk, grp, nbatch):
    ph = pl.program_id(1)
    step = pl.program_id(2)
    nsteps = nblk // grp
    npages = 2 * grp
    ring = SAMPLE_LOOKAHEAD + 1
    total = nbatch * 2 * nsteps
    g_lin = (pl.program_id(0) * 2 + ph) * nsteps + step

    def page_copy(src_ref, page, slot):
        return pltpu.make_async_copy(src_ref.at[0, page], page_ref.at[slot], sem_ref.at[slot])

    def request(gl):
        b2 = gl // (2 * nsteps)
        r2 = gl % (2 * nsteps)
        ph2 = r2 // nsteps
        p0 = (r2 % nsteps) * npages
        base = (gl % ring) * npages
        for src_ref, which in ((ck_ref, 0), (cv_ref, 1)):
            @pl.when(ph2 == which)
            def _():
                for u in range(npages):
                    page_copy(src_ref, pt_ref[b2, p0 + u], base + u).start()

    @pl.when(g_lin == 0)
    def _():
        for gl in range(min(SAMPLE_LOOKAHEAD, total)):
            request(gl)

    @pl.when(g_lin + SAMPLE_LOOKAHEAD < total)
    def _():
        request(g_lin + SAMPLE_LOOKAHEAD)

    slot0 = (g_lin % ring) * npages
    for u in range(npages):
        page_copy(ck_ref, 0, slot0 + u).wait()
    k_refs = v_refs = [page_ref.at[slot0 + u] for u in range(npages)]
    ntok = 8
    rows = PAGE_SIZE * ATTN_HEADS
    nkeys = 2 * rows
    far16 = jnp.concatenate([bfar_ref[...]] * (nkeys // 128), axis=1)
    lane_blk = lax.broadcasted_iota(jnp.int32, (ntok, 128), 1) // 8

    def block_bias(j):
        return jnp.where(j == nblk - 1, blast_ref[...], far16)

    def tile16(x):
        return jnp.concatenate([x] * (nkeys // 128), axis=1)

    def compact_tile(j):
        return pl.ds(pl.multiple_of((j // 16) * 128, 128), 128)

    def put_compact(ref, j, x):
        ref[:, compact_tile(j)] = jnp.where(lane_blk == j % 16, x, ref[:, compact_tile(j)])

    @pl.when((ph == 0) & (step == 0))
    def _():
        bmax_ref[...] = jnp.zeros(bmax_ref.shape, F32)
        bexp_ref[...] = jnp.zeros(bexp_ref.shape, F32)

    @pl.when(ph == 0)
    def _():
        qs = (q_ref[...] * (HEAD_DIM ** -0.5)).astype(BF16)
        gs = range(grp)
        js = [step * grp + g for g in gs]
        raw = []
        for g in gs:
            k0 = k_refs[2 * g][...]
            k1 = k_refs[2 * g + 1][...]
            bsum_ref[pl.ds(pl.multiple_of(js[g] * 8, 8), 8), :] = jnp.sum(k0, axis=0) + jnp.sum(k1, axis=0)
            k2 = jnp.concatenate([k0.reshape(rows, HEAD_DIM), k1.reshape(rows, HEAD_DIM)], axis=0).astype(BF16)
            raw.append(_dot_nt(qs, k2))
        sc = [_diag_extract(raw[g]) + block_bias(js[g]) for g in gs]
        for g in gs:
            sc_ref[js[g]] = sc[g]
        bm = _class_allreduce_many([_fold_tiles(sc[g], jnp.maximum) for g in gs], jnp.maximum)
        be = _class_allreduce_many([_fold_tiles(jnp.exp(sc[g] - tile16(bm[g])), jnp.add) for g in gs], jnp.add)
        for g in gs:
            put_compact(bmax_ref, js[g], bm[g])
            put_compact(bexp_ref, js[g], be[g])

    @pl.when((ph == 0) & (step == nsteps - 1))
    def _():
        q = q_ref[...]
        qh, ql = _split(q)
        bh, bl = _split(bsum_ref[...] * (1.0 / MOBA_BLOCK))
        scx = _diag_extract(_dot_nt(qh, bh) + _dot_nt(qh, bl) + _dot_nt(ql, bh))
        width = nblk * 8
        jidx = (lax.broadcasted_iota(jnp.int32, (ntok, width), 1) // 8).astype(F32)

        def creduce(x, op):
            y = _class_allreduce(_fold_tiles(x, op), op)
            return jnp.concatenate([y] * (width // 128), axis=1)

        selx = jnp.zeros((ntok, width), F32)
        for _ in range(MOBA_TOPK):
            mx = creduce(scx, jnp.maximum)
            first = creduce(jnp.where(scx == mx, jidx, float(nblk)), jnp.minimum)
            hit = jidx == first
            selx = jnp.where(hit, 1.0, selx)
            scx = jnp.where(hit, NEG, scx)
        selx_ref[...] = selx
        picked = selx > 0.5

        qs = (q * (HEAD_DIM ** -0.5)).astype(BF16)
        so = _diag_extract(_dot_nt(qs, kn_ref[...].astype(BF16))) + bown_ref[...]
        tq = lax.broadcasted_iota(jnp.int32, so.shape, 0)
        tk = lax.broadcasted_iota(jnp.int32, so.shape, 1) // 8
        so = jnp.concatenate([jnp.where(tk <= tq, so, NEG), jnp.full((ntok, 64), NEG, F32)], axis=1)

        bmax = bmax_ref[...]
        m = jnp.maximum(_class_allreduce(so, jnp.maximum),
                        creduce(jnp.where(picked, bmax, NEG), jnp.maximum)[:, 0:128])
        mw = jnp.concatenate([m] * (width // 128), axis=1)
        l_blocks = creduce(jnp.where(picked, bexp_ref[...] * jnp.exp(bmax - mw), 0.0), jnp.add)[:, 0:128]
        po = jnp.exp(so - m)
        li = 1.0 / (_class_allreduce(po, jnp.add) + l_blocks)
        m_ref[...] = m
        li_ref[...] = li
        pfull = _diag_expand((po * li)[:, 0:64]).astype(BF16)
        acc_ref[...] = _dot(pfull, vn_ref[...].astype(BF16))

    @pl.when(ph == 1)
    def _():
        m16 = tile16(m_ref[...])
        li16 = tile16(li_ref[...])
        gs = range(grp)
        js = [step * grp + g for g in gs]
        mine = [jnp.where(lane_blk == js[g] % 16, selx_ref[:, compact_tile(js[g])], 0.0) for g in gs]
        picked = _class_allreduce_many(mine, jnp.add)
        p = [jnp.where(tile16(picked[g]) > 0.5, jnp.exp(sc_ref[js[g]] - m16) * li16, 0.0) for g in gs]
        pv = []
        for g in gs:
            v2 = jnp.concatenate([v_refs[2 * g][...].reshape(rows, HEAD_DIM),
                                  v_refs[2 * g + 1][...].reshape(rows, HEAD_DIM)], axis=0).astype(BF16)
            pv.append(_dot(_diag_expand(p[g]).astype(BF16), v2))
        acc = acc_ref[...]
        for g in gs:
            acc = acc + pv[g]
        acc_ref[...] = acc

    @pl.when((ph == 1) & (step == nsteps - 1))
    def _():
        o_ref[...] = acc_ref[...]


def _moba_sample(qht, knew, vnew, cache_k, cache_v, page_table, bias):
    nb_, n_pages = page_table.shape
    assert n_pages % PAGES_PER_BLOCK == 0 and PAGES_PER_BLOCK == 2
    nblk = n_pages // 2
    assert nblk >= MOBA_TOPK and (nblk * 8) % 128 == 0
    nkeys = 2 * PAGE_SIZE * ATTN_HEADS
    grp = SAMPLE_BLOCKS_PER_STEP
    assert nblk % grp == 0
    nsteps = nblk // grp

    per_b = pl.BlockSpec((None, 64, HEAD_DIM), lambda b, p, j, pt: (b, 0, 0))
    cst = lambda r, c: pl.BlockSpec((r, c), lambda b, p, j, pt: (0, 0))
    hbm = pl.BlockSpec(memory_space=pl.ANY)
    nslots = (SAMPLE_LOOKAHEAD + 1) * 2 * grp
    grid_spec = pltpu.PrefetchScalarGridSpec(
        num_scalar_prefetch=1,
        grid=(nb_, 2, nsteps),
        in_specs=[per_b, hbm, hbm, per_b, per_b, cst(8, nkeys), cst(8, 64), cst(1, 128)],
        out_specs=per_b,
        scratch_shapes=[pltpu.VMEM((nblk, 8, nkeys), F32),
                        pltpu.VMEM((nblk * 8, HEAD_DIM), F32),
                        pltpu.VMEM((8, nblk * 8), F32),
                        pltpu.VMEM((8, nblk * 8), F32),
                        pltpu.VMEM((8, nblk * 8), F32),
                        pltpu.VMEM((8, 128), F32),
                        pltpu.VMEM((8, 128), F32),
                        pltpu.VMEM((64, HEAD_DIM), F32),
                        pltpu.VMEM((nslots, PAGE_SIZE, ATTN_HEADS, HEAD_DIM), F32),
                        pltpu.SemaphoreType.DMA((nslots,))])
    return pl.pallas_call(
        functools.partial(_moba_sample_kernel, nblk=nblk, grp=grp, nbatch=nb_),
        grid_spec=grid_spec,
        out_shape=jax.ShapeDtypeStruct((nb_, 64, HEAD_DIM), F32),
        compiler_params=_cparams(("arbitrary", "arbitrary", "arbitrary")),
        name="moba_sample",
    )(page_table, qht, cache_k, cache_v, knew, vnew, bias["last_s"], bias["own_s"], bias["far_s"])


def _bias_of_distance(dist, rel_bias):
    dist = jnp.maximum(dist, 0)
    exact = N_BUCKETS // 2
    log_ratio = jnp.log(jnp.maximum(dist, 1).astype(F32) / exact) / math.log(MAX_DISTANCE / exact)
    large = jnp.minimum(exact + (log_ratio * (N_BUCKETS - exact)).astype(jnp.int32), N_BUCKETS - 1)
    return rel_bias[jnp.where(dist < exact, dist, large)]


def _bias_tiles(rel_bias, past_len, dec_seq):
    blk = MOBA_BLOCK
    val = _bias_of_distance(jnp.arange(2 * blk), rel_bias)
    val_t = val.T

    def toeplitz(ext):
        n2 = ext.shape[1]
        n = n2 // 2
        return jnp.tile(ext, (1, n))[:, :n * (n2 - 1)].reshape(ext.shape[0], n, n2 - 1)[:, :, :n]

    own_t = toeplitz(val_t)
    prev_t = toeplitz(jnp.roll(val_t, -blk, axis=1))
    far_row = val[2 * blk - 1]
    far = jnp.broadcast_to(far_row[:, None, None], (ATTN_HEADS, 1, HEAD_DIM))
    assert past_len % blk == 0 and 2 * blk - 1 >= MAX_DISTANCE and dec_seq < blk
    last = jnp.stack([val[t + 1:t + 1 + blk][::-1] for t in range(dec_seq)])
    last_s = last.reshape(dec_seq, blk * ATTN_HEADS)
    own = jnp.stack([jnp.concatenate([val[:t + 1][::-1], jnp.zeros((dec_seq - 1 - t, ATTN_HEADS), F32)])
                     for t in range(dec_seq)])
    own_s = own.reshape(dec_seq, dec_seq * ATTN_HEADS)
    far_s = jnp.tile(far_row, 128 // ATTN_HEADS)[None, :]
    return dict(own_t=own_t, prev_t=prev_t, far=far, last_s=last_s, own_s=own_s, far_s=far_s)


def _np_consts():
    seg = np.arange(SEG_CHUNK) // RW_HEAD
    bones = (seg[:, None] == seg[None, :]).astype(np.float32)
    idx = np.arange(SEG_CHUNK) % RW_HEAD
    eye4 = (idx[:, None] == idx[None, :]).astype(np.float32)
    return bones, eye4


def _pad_rows(w, row0, total):
    return jnp.zeros((total, w.shape[1]), BF16).at[row0:row0 + w.shape[0]].set(w.astype(BF16))


def kernel(x_prompt, x_sample, cache_k, cache_v, page_table, state_wkv, state_shift, w_in, rel_bias, rw_mu, rw_w0, rw_w2, rw_a0, rw_a2, rw_g2, rw_k_k, rw_k_a, rw_r_k, rw_ln_g, rw_ln_b, w_up_attn, w_up_rwkv, w_o, ln1_g, ln1_b, w_ffn_gate, w_ffn_up, w_ffn_down, ln2_g, ln2_b):
    assert x_prompt.shape[0] == 1 and w_in.shape[0] == DEPTH == 1
    t_p = x_prompt.shape[1]
    nb_s, t_s, _ = x_sample.shape
    assert t_s == 8
    past_len = page_table.shape[1] * PAGE_SIZE

    w_in2 = w_in[0]
    w_main = w_in2[:, :MAIN_COLS].astype(BF16)
    w_lora = jnp.pad(w_in2[:, LORA_COL:GATE_COL], ((0, 0), (0, LORA_PAD - LORA_COLS))).astype(BF16)
    w_gate = w_in2[:, GATE_COL:].astype(BF16)
    wb_up_attn, wb_up_rwkv, wb_o = w_up_attn[0].astype(BF16), w_up_rwkv[0].astype(BF16), w_o[0].astype(BF16)
    wb_gate, wb_up, wb_down = w_ffn_gate[0].astype(BF16), w_ffn_up[0].astype(BF16), w_ffn_down[0].astype(BF16)
    bones_np, eye_np = _np_consts()
    mu = rw_mu[0]
    prm = dict(
        mu=mu[None, :3 * RW_WIDTH],
        mul=jnp.pad(mu[3 * RW_WIDTH:], (0, LORA_PAD - LORA_COLS))[None, :],
        w0=rw_w0, a0=rw_a0,
        w2=_pad_rows(rw_w2[0], 0, LORA_PAD),
        a2=_pad_rows(rw_a2[0], DECAY_LORA, LORA_PAD),
        g2=_pad_rows(rw_g2[0], DECAY_LORA + AAA_LORA, LORA_PAD),
        k_k=rw_k_k, k_a=rw_k_a, r_k=rw_r_k.reshape(1, RW_WIDTH), ln_g=rw_ln_g, ln_b=rw_ln_b,
        bones=jnp.asarray(bones_np, BF16), eye4=jnp.asarray(eye_np, F32))
    bias = _bias_tiles(rel_bias, past_len, t_s)

    def group(x2d, nseq, tseq, zprev, zlprev, s0, attend, tt, tb):
        z1 = _proj(x2d, w_main, 0, MAIN_COLS)
        zl = _proj(x2d, w_lora, 0, LORA_PAD)
        zg = _proj(x2d, w_gate, 0, 2 * D_MODEL)
        o_attn = attend(z1)
        r, w, k, v, kk, b, g = _rwkv_prep(z1.reshape(nseq, tseq, MAIN_COLS), zl.reshape(nseq, tseq, LORA_PAD),
                                          zprev, zlprev, prm, tt)
        if s0 is None:
            y, s_fin = _rwkv_chunked(r, w, k, v, kk, b)
        else:
            y, s_fin = _rwkv_scan(r, w, k, v, kk, b, s0, prm, tb)
            s_fin = _state_from_stacked(s_fin)
        flat = lambda u: u.reshape(nseq * tseq, RW_WIDTH)
        o_rwkv = _rwkv_post(flat(y), flat(r), flat(k), flat(v), flat(g), prm)
        mixed = _merge(o_attn, o_rwkv, zg, wb_up_attn, wb_up_rwkv)
        h = _proj_ln(mixed, wb_o, x2d, ln1_g, ln1_b, tm=1024)
        act = _ffn_up(h, wb_gate, wb_up)
        out = _proj_ln(act, wb_down, h, ln2_g, ln2_b, tm=512)
        new_k = z1[:, ATTN_WIDTH:2 * ATTN_WIDTH]
        new_v = z1[:, 2 * ATTN_WIDTH:3 * ATTN_WIDTH]
        return out, new_k, new_v, s_fin

    xp = x_prompt[0]
    yp, kp, vp, sp = group(
        xp, 1, t_p,
        jnp.zeros((1, 1, 3 * RW_WIDTH), F32), jnp.zeros((1, 1, LORA_PAD), F32),
        None, lambda z1: _moba_prompt(z1, bias), 256, 8)

    xs = x_sample.reshape(nb_s * t_s, D_MODEL)
    sh = state_shift[0]
    zprev = _proj(sh, w_main, RKV_COL, 3 * RW_WIDTH)[:, None, :]
    zlprev = _proj(sh, w_lora, 0, LORA_PAD)[:, None, :]

    def attend_sample(z1):
        q = z1[:, 0:ATTN_WIDTH].reshape(nb_s, t_s, ATTN_HEADS, HEAD_DIM)
        qht = jnp.transpose(q, (0, 2, 1, 3)).reshape(nb_s, ATTN_HEADS * t_s, HEAD_DIM)
        kn = z1[:, ATTN_WIDTH:2 * ATTN_WIDTH].reshape(nb_s, t_s * ATTN_HEADS, HEAD_DIM)
        vn = z1[:, 2 * ATTN_WIDTH:3 * ATTN_WIDTH].reshape(nb_s, t_s * ATTN_HEADS, HEAD_DIM)
        o = _moba_sample(qht, kn, vn, cache_k, cache_v, page_table, bias)
        o = jnp.transpose(o.reshape(nb_s, ATTN_HEADS, t_s, HEAD_DIM), (0, 2, 1, 3))
        return o.reshape(nb_s * t_s, ATTN_WIDTH).astype(BF16)

    ys, ks, vs, ss = group(xs, nb_s, t_s, zprev, zlprev, _state_to_stacked(state_wkv[0]),
                           attend_sample, 8, 8)

    return (yp[None], ys.reshape(nb_s, t_s, D_MODEL),
            kp.reshape(1, 1, t_p, ATTN_HEADS, HEAD_DIM), vp.reshape(1, 1, t_p, ATTN_HEADS, HEAD_DIM),
            sp[None], xp[None, -1:, :],
            ks.reshape(1, nb_s, t_s, ATTN_HEADS, HEAD_DIM), vs.reshape(1, nb_s, t_s, ATTN_HEADS, HEAD_DIM),
            ss[None], x_sample[None, :, -1, :])
```

```python
import functools
import math

import numpy as np
import jax
import jax.numpy as jnp
from jax import lax
from jax.experimental import pallas as pl
from jax.experimental.pallas import tpu as pltpu

F32 = jnp.float32
BF16 = jnp.bfloat16

D_MODEL = 2048
HEAD_DIM = 128
ATTN_HEADS = D_MODEL // 256
ATTN_WIDTH = ATTN_HEADS * HEAD_DIM
MOBA_BLOCK = 256
MOBA_TOPK = 3
N_BUCKETS = 32
MAX_DISTANCE = 128
PAGE_SIZE = 128
RW_HEAD = 64
RW_HEADS = D_MODEL // 128
RW_WIDTH = RW_HEADS * RW_HEAD
DECAY_LORA = 96
AAA_LORA = 96
GATE_LORA = 256
LORA_COLS = DECAY_LORA + AAA_LORA + GATE_LORA
LORA_PAD = 512
GN_EPS = RW_HEAD * 1e-5
D_FF = 5632
LN_EPS = 1e-5
LN_ROWS = 128
DEPTH = 1
DEEPNORM_ALPHA = (2 * DEPTH) ** 0.25

RKV_COL = 3 * ATTN_WIDTH
LORA_COL = RKV_COL + 3 * RW_WIDTH
GATE_COL = LORA_COL + LORA_COLS
MAIN_COLS = LORA_COL

NEG = -1e30
LOG2E = math.log2(math.e)
FAR_PARTS = 4
ONES_ROWS = 8
SEG_CHUNK = 256
PAGES_PER_BLOCK = MOBA_BLOCK // PAGE_SIZE
SAMPLE_BLOCKS_PER_STEP = 8
SAMPLE_LOOKAHEAD = 2
RW_CHUNK = 64
SCAN_SEQS_PER_STEP = 4
RW_GROUP = 4
CHUNKS_PER_STEP = 2
VMEM_LIMIT = 56 * 1024 * 1024


def _cparams(sem):
    return pltpu.CompilerParams(dimension_semantics=sem, vmem_limit_bytes=VMEM_LIMIT)


def _dot(a, b):
    return jnp.dot(a, b, preferred_element_type=F32)


def _dot_nt(a, b):
    return lax.dot_general(a, b, (((1,), (1,)), ((), ())), preferred_element_type=F32)


def _split(x):
    hi = x.astype(BF16)
    lo = (x - hi.astype(F32)).astype(BF16)
    return hi, lo


def _sigmoid(x):
    return 1.0 / (1.0 + jnp.exp(-x))


def _tile(m, pref):
    t = min(m, pref)
    assert m % t == 0, (m, pref)
    return t


def _proj_kernel(x_ref, w_ref, o_ref, xb_ref):
    @pl.when(pl.program_id(1) == 0)
    def _():
        xb_ref[...] = x_ref[...].astype(BF16)

    o_ref[...] = _dot(xb_ref[...], w_ref[...].astype(BF16)).astype(o_ref.dtype)


def _proj(x, w, col0, ncols, tm=1024, tn=512):
    m, k = x.shape
    tm = _tile(m, tm)
    tn = _tile(ncols, tn)
    assert col0 % tn == 0
    c0 = col0 // tn
    return pl.pallas_call(
        _proj_kernel,
        grid=(m // tm, ncols // tn),
        in_specs=[pl.BlockSpec((tm, k), lambda i, j: (i, 0)),
                  pl.BlockSpec((k, tn), lambda i, j: (0, c0 + j))],
        out_specs=pl.BlockSpec((tm, tn), lambda i, j: (i, j)),
        out_shape=jax.ShapeDtypeStruct((m, ncols), F32),
        scratch_shapes=[pltpu.VMEM((tm, k), BF16)],
        compiler_params=_cparams(("parallel", "arbitrary")),
        name="proj",
    )(x, w)


def _proj_split_kernel(x_ref, w_ref, *refs, bounds):
    o_refs, xb_ref = refs[:-1], refs[-1]
    j = pl.program_id(1)

    @pl.when(j == 0)
    def _():
        xb_ref[...] = x_ref[...].astype(BF16)

    acc = _dot(xb_ref[...], w_ref[...].astype(BF16))
    for o_ref, (lo, hi) in zip(o_refs, bounds):
        @pl.when((j >= lo) & (j < hi))
        def _():
            o_ref[...] = acc


def _proj_split(x, w, widths, tm=1024, tn=512):
    m, k = x.shape
    tm = _tile(m, tm)
    assert all(wd % tn == 0 for wd in widths)
    edges = np.cumsum([0] + [wd // tn for wd in widths])
    bounds = [(int(edges[g]), int(edges[g + 1])) for g in range(len(widths))]

    def out_map(lo, hi):
        return lambda i, j: (i, jnp.clip(j - lo, 0, hi - lo - 1))

    return pl.pallas_call(
        functools.partial(_proj_split_kernel, bounds=bounds),
        grid=(m // tm, int(edges[-1])),
        in_specs=[pl.BlockSpec((tm, k), lambda i, j: (i, 0)),
                  pl.BlockSpec((k, tn), lambda i, j: (0, j))],
        out_specs=[pl.BlockSpec((tm, tn), out_map(lo, hi)) for lo, hi in bounds],
        out_shape=[jax.ShapeDtypeStruct((m, wd), F32) for wd in widths],
        scratch_shapes=[pltpu.VMEM((tm, k), BF16)],
        compiler_params=_cparams(("parallel", "arbitrary")),
        name="proj_split",
    )(x, w)


def _merge_kernel(oa_ref, or_ref, wa_ref, wr_ref, ga_ref, gr_ref, o_ref):
    a = _dot(oa_ref[...], wa_ref[...].astype(BF16))
    r = _dot(or_ref[...], wr_ref[...].astype(BF16))
    o_ref[...] = (_sigmoid(ga_ref[...]) * a + _sigmoid(gr_ref[...]) * r).astype(o_ref.dtype)


def _merge(o_attn, o_rwkv, zg, w_up_attn, w_up_rwkv, tm=1024, tn=512):
    m = o_attn.shape[0]
    tm = _tile(m, tm)
    nj = D_MODEL // tn
    return pl.pallas_call(
        _merge_kernel,
        grid=(m // tm, nj),
        in_specs=[pl.BlockSpec((tm, ATTN_WIDTH), lambda i, j: (i, 0)),
                  pl.BlockSpec((tm, RW_WIDTH), lambda i, j: (i, 0)),
                  pl.BlockSpec((ATTN_WIDTH, tn), lambda i, j: (0, j)),
                  pl.BlockSpec((RW_WIDTH, tn), lambda i, j: (0, j)),
                  pl.BlockSpec((tm, tn), lambda i, j: (i, j)),
                  pl.BlockSpec((tm, tn), lambda i, j: (i, j + nj))],
        out_specs=pl.BlockSpec((tm, tn), lambda i, j: (i, j)),
        out_shape=jax.ShapeDtypeStruct((m, D_MODEL), BF16),
        compiler_params=_cparams(("parallel", "arbitrary")),
        name="merge",
    )(o_attn, o_rwkv, w_up_attn, w_up_rwkv, zg, zg)


def _proj_ln_kernel(m_ref, w_ref, x_ref, g_ref, b_ref, o_ref, *, tn, nj):
    j = pl.program_id(1)
    col = pl.multiple_of(j * tn, tn)
    mb = m_ref[...].astype(BF16)
    o_ref[:, pl.ds(col, tn)] = DEEPNORM_ALPHA * x_ref[...] + _dot(mb, w_ref[...].astype(BF16))

    @pl.when(j == nj - 1)
    def _():
        ln_rows = min(LN_ROWS, o_ref.shape[0])
        assert o_ref.shape[0] % ln_rows == 0

        def norm_rows(c, carry):
            rows = pl.ds(pl.multiple_of(c * ln_rows, ln_rows), ln_rows)
            y = o_ref[rows, :]
            mu = jnp.mean(y, axis=-1, keepdims=True)
            d = y - mu
            var = jnp.mean(d * d, axis=-1, keepdims=True)
            o_ref[rows, :] = d * lax.rsqrt(var + LN_EPS) * g_ref[...] + b_ref[...]
            return carry
        lax.fori_loop(0, o_ref.shape[0] // ln_rows, norm_rows, 0)


def _proj_ln(mat, w, x, g, b, tm=512, tn=256):
    m, k = mat.shape
    tm = _tile(m, tm)
    nj = D_MODEL // tn
    return pl.pallas_call(
        functools.partial(_proj_ln_kernel, tn=tn, nj=nj),
        grid=(m // tm, nj),
        in_specs=[pl.BlockSpec((tm, k), lambda i, j: (i, 0)),
                  pl.BlockSpec((k, tn), lambda i, j: (0, j)),
                  pl.BlockSpec((tm, tn), lambda i, j: (i, j)),
                  pl.BlockSpec((1, D_MODEL), lambda i, j: (0, 0)),
                  pl.BlockSpec((1, D_MODEL), lambda i, j: (0, 0))],
        out_specs=pl.BlockSpec((tm, D_MODEL), lambda i, j: (i, 0)),
        out_shape=jax.ShapeDtypeStruct((m, D_MODEL), F32),
        compiler_params=_cparams(("parallel", "arbitrary")),
        name="proj_ln",
    )(mat, w, x, g, b)


def _ffn_up_kernel(h_ref, wg_ref, wu_ref, o_ref, hb_ref):
    @pl.when(pl.program_id(1) == 0)
    def _():
        hb_ref[...] = h_ref[...].astype(BF16)

    hb = hb_ref[...]
    a = _dot(hb, wg_ref[...].astype(BF16))
    u = _dot(hb, wu_ref[...].astype(BF16))
    o_ref[...] = (a * _sigmoid(a) * u).astype(o_ref.dtype)


def _ffn_up(h, wg, wu, tm=1024, tn=256):
    m = h.shape[0]
    tm = _tile(m, tm)
    return pl.pallas_call(
        _ffn_up_kernel,
        grid=(m // tm, D_FF // tn),
        in_specs=[pl.BlockSpec((tm, D_MODEL), lambda i, j: (i, 0)),
                  pl.BlockSpec((D_MODEL, tn), lambda i, j: (0, j)),
                  pl.BlockSpec((D_MODEL, tn), lambda i, j: (0, j))],
        out_specs=pl.BlockSpec((tm, tn), lambda i, j: (i, j)),
        out_shape=jax.ShapeDtypeStruct((m, D_FF), BF16),
        scratch_shapes=[pltpu.VMEM((tm, D_MODEL), BF16)],
        compiler_params=_cparams(("parallel", "arbitrary")),
        name="ffn_up",
    )(h, wg, wu)


def _seg_sum(x, bones):
    outs = []
    for c in range(x.shape[1] // SEG_CHUNK):
        hi, lo = _split(x[:, c * SEG_CHUNK:(c + 1) * SEG_CHUNK])
        outs.append(_dot(hi, bones) + _dot(lo, bones))
    return jnp.concatenate(outs, axis=1)


def _shifted(z, prev):
    zs = pltpu.roll(z, 1, 0)
    row = lax.broadcasted_iota(jnp.int32, z.shape, 0)
    return jnp.where(row == 0, prev, zs)


def _rwkv_prep_kernel(z_ref, zl_ref, zp_ref, zlp_ref, mu_ref, mul_ref, w0_ref, a0_ref, w2_ref, a2_ref,
                      g2_ref, kkw_ref, kaw_ref, bones_ref,
                      r_o, w_o, k_o, v_o, kk_o, b_o, g_o, prev_ref, prevl_ref):
    @pl.when(pl.program_id(1) == 0)
    def _():
        prev_ref[...] = zp_ref[...]
        prevl_ref[...] = zlp_ref[...]

    z = z_ref[...]
    zl = zl_ref[...]
    tt = z.shape[0]
    zm = z + (_shifted(z, prev_ref[...]) - z) * mu_ref[...]
    zlm = zl + (_shifted(zl, prevl_ref[...]) - zl) * mul_ref[...]
    prev_ref[...] = z[tt - 1:tt, :]
    prevl_ref[...] = zl[tt - 1:tt, :]

    r = zm[:, 0:RW_WIDTH]
    k = zm[:, RW_WIDTH:2 * RW_WIDTH]
    v = zm[:, 2 * RW_WIDTH:3 * RW_WIDTH]
    xw = w0_ref[...] + _dot(jnp.tanh(zlm).astype(BF16), w2_ref[...])
    nx = -xw
    softplus = jnp.maximum(nx, 0.0) + jnp.log(1.0 + jnp.exp(-jnp.abs(nx)))
    log_decay = -jnp.exp(-softplus - 0.5)
    a = _sigmoid(a0_ref[...] + _dot(zlm.astype(BF16), a2_ref[...]))
    g = _dot(_sigmoid(zlm).astype(BF16), g2_ref[...])
    kk = k * kkw_ref[...]
    ssq = _seg_sum(kk * kk, bones_ref[...])
    kk = kk / jnp.maximum(jnp.sqrt(ssq), 1e-12)
    r_o[...] = r
    w_o[...] = log_decay
    k_o[...] = k * (1.0 + (a - 1.0) * kaw_ref[...])
    v_o[...] = v
    kk_o[...] = kk
    b_o[...] = kk * a
    g_o[...] = g


def _rwkv_prep(z1, zl, zprev, zlprev, prm, tt):
    n, t, _ = z1.shape
    tt = _tile(t, tt)
    row = lambda c: pl.BlockSpec((1, c), lambda i, j: (0, 0))
    mat = lambda r, c: pl.BlockSpec((r, c), lambda i, j: (0, 0))
    seq = lambda c: pl.BlockSpec((None, tt, c), lambda i, j: (i, j, 0))
    outs = pl.pallas_call(
        _rwkv_prep_kernel,
        grid=(n, t // tt),
        in_specs=[seq(3 * RW_WIDTH),
                  seq(LORA_PAD),
                  pl.BlockSpec((None, 1, 3 * RW_WIDTH), lambda i, j: (i, 0, 0)),
                  pl.BlockSpec((None, 1, LORA_PAD), lambda i, j: (i, 0, 0)),
                  row(3 * RW_WIDTH), row(LORA_PAD), row(RW_WIDTH), row(RW_WIDTH),
                  mat(LORA_PAD, RW_WIDTH), mat(LORA_PAD, RW_WIDTH), mat(LORA_PAD, RW_WIDTH),
                  row(RW_WIDTH), row(RW_WIDTH), mat(SEG_CHUNK, SEG_CHUNK)],
        out_specs=[seq(RW_WIDTH)] * 7,
        out_shape=[jax.ShapeDtypeStruct((n, t, RW_WIDTH), F32)] * 7,
        scratch_shapes=[pltpu.VMEM((1, 3 * RW_WIDTH), F32), pltpu.VMEM((1, LORA_PAD), F32)],
        compiler_params=_cparams(("parallel", "arbitrary")),
        name="rwkv_prep",
    )(z1, zl, zprev, zlprev, prm["mu"], prm["mul"], prm["w0"], prm["a0"], prm["w2"], prm["a2"],
      prm["g2"], prm["k_k"], prm["k_a"], prm["bones"])
    return outs


def _rwkv_scan_kernel(r_ref, w_ref, k_ref, v_ref, kk_ref, b_ref, s0_ref, bones_ref, eye_ref,
                      y_ref, sout_ref, s_ref, *, tb, nblk, nseq):
    t = pl.program_id(1)

    @pl.when(t == 0)
    def _():
        s_ref[...] = s0_ref[...]

    bones = bones_ref[...]
    eye = eye_ref[...]
    seqs = range(nseq)

    def bc(ref, q, s):
        return jnp.concatenate(
            [jnp.broadcast_to(ref[q, pl.ds(s, 1), c * SEG_CHUNK:(c + 1) * SEG_CHUNK], (RW_HEAD, SEG_CHUNK))
             for c in range(4)], axis=0)

    def step(s, carry):
        st = [s_ref[q] for q in seqs]
        split = [_split(st[q] * bc(kk_ref, q, s)) for q in seqs]
        sa = [_dot(split[q][0], bones) + _dot(split[q][1], bones) for q in seqs]
        vcol = [_dot((bc(v_ref, q, s) * eye).astype(BF16), bones) for q in seqs]
        sn = [st[q] * jnp.exp(bc(w_ref, q, s)) - sa[q] * bc(b_ref, q, s) + vcol[q] * bc(k_ref, q, s) for q in seqs]
        for q in seqs:
            s_ref[q] = sn[q]
        yb = [_dot((sn[q] * bc(r_ref, q, s)).astype(BF16), bones) for q in seqs]
        for q in seqs:
            y4 = jnp.sum((yb[q] * eye).reshape(4, RW_HEAD, SEG_CHUNK), axis=1)
            for c in range(4):
                y_ref[q, pl.ds(s, 1), c * SEG_CHUNK:(c + 1) * SEG_CHUNK] = y4[c:c + 1, :]
        return carry

    lax.fori_loop(0, tb, step, 0)

    @pl.when(t == nblk - 1)
    def _():
        sout_ref[...] = s_ref[...]


def _rwkv_scan(r, w, k, v, kk, b, s0, prm, tb=8):
    n, t, _ = r.shape
    tb = _tile(t, tb)
    nblk = t // tb
    nseq = _tile(n, SCAN_SEQS_PER_STEP)
    seq = pl.BlockSpec((nseq, tb, RW_WIDTH), lambda i, j: (i, j, 0))
    st = pl.BlockSpec((nseq, 4 * RW_HEAD, SEG_CHUNK), lambda i, j: (i, 0, 0))
    cst = pl.BlockSpec((SEG_CHUNK, SEG_CHUNK), lambda i, j: (0, 0))
    return pl.pallas_call(
        functools.partial(_rwkv_scan_kernel, tb=tb, nblk=nblk, nseq=nseq),
        grid=(n // nseq, nblk),
        in_specs=[seq] * 6 + [st, cst, cst],
        out_specs=[seq, st],
        out_shape=[jax.ShapeDtypeStruct((n, t, RW_WIDTH), F32),
                   jax.ShapeDtypeStruct((n, 4 * RW_HEAD, SEG_CHUNK), F32)],
        scratch_shapes=[pltpu.VMEM((nseq, 4 * RW_HEAD, SEG_CHUNK), F32)],
        compiler_params=_cparams(("parallel", "arbitrary")),
        name="rwkv_scan",
    )(r, w, k, v, kk, b, s0, prm["bones"], prm["eye4"])


def _rwkv_chunk_kernel(r_ref, lw_ref, k_ref, v_ref, kk_ref, b_ref, y_ref, hout_ref, h_ref, *, nchunks):
    t = pl.program_id(1)
    cs = RW_CHUNK
    n = RW_GROUP * RW_HEAD
    ngroups = RW_HEADS // RW_GROUP

    @pl.when(t == 0)
    def _():
        h_ref[...] = jnp.zeros(h_ref.shape, F32)

    row = lax.broadcasted_iota(jnp.int32, (n, n), 0)
    lane = lax.broadcasted_iota(jnp.int32, (n, n), 1)
    same_head = (row // cs) == (lane // cs)
    strict = same_head & ((lane % cs) < (row % cs))
    incl = same_head & ((lane % cs) <= (row % cs))
    eye = row == lane
    lane_head = lax.broadcasted_iota(jnp.int32, (cs, n), 1) // RW_HEAD
    row_in = lax.broadcasted_iota(jnp.int32, (cs, n), 0)

    def stack(x):
        return jnp.concatenate([jnp.where(lane_head == hl, x, 0.0) for hl in range(RW_GROUP)], axis=0)

    def tile4(x):
        return jnp.concatenate([x] * RW_GROUP, axis=1)

    def cumsum_rows(x):
        for sh in (1, 2, 4, 8, 16, 32):
            x = x + jnp.where(row_in >= sh, pltpu.roll(x, sh, 0), 0.0)
        return x

    units = [(ci, c) for ci in range(CHUNKS_PER_STEP) for c in range(ngroups)]
    pre = {}
    for ci, c in units:
        sl = slice(c * n, (c + 1) * n)
        rs = slice(ci * cs, (ci + 1) * cs)
        lw = lw_ref[rs, sl]
        l_in = cumsum_rows(lw)
        l_end = l_in[cs - 1:cs, :]
        e_neg = jnp.exp(-l_in)
        e_tail = jnp.exp(l_end - l_in)
        kq = stack(kk_ref[rs, sl] * jnp.exp(l_in - lw)).astype(BF16)
        rq = stack(r_ref[rs, sl] * jnp.exp(l_in)).astype(BF16)
        k_c = k_ref[rs, sl]
        b_c = b_ref[rs, sl]
        kh = (k_c * e_neg).astype(BF16)
        bh = (b_c * e_neg).astype(BF16)
        tail = jnp.concatenate([stack(k_c * e_tail), -stack(b_c * e_tail)], axis=0)
        pre[ci, c] = dict(
            sl=sl, rs=rs, kq=kq, rq=rq,
            a_k=jnp.where(strict, tile4(_dot_nt(kq, kh)), 0.0).astype(BF16),
            a_b=jnp.where(strict, tile4(_dot_nt(kq, bh)), 0.0),
            r_k=jnp.where(incl, tile4(_dot_nt(rq, kh)), 0.0).astype(BF16),
            r_b=jnp.where(incl, tile4(_dot_nt(rq, bh)), 0.0).astype(BF16),
            v_bd=stack(v_ref[rs, sl]).astype(BF16),
            tail_t=tail.T.astype(BF16),
            decay_col=jnp.sum(jnp.where(eye, jnp.broadcast_to(jnp.exp(l_end), (n, n)), 0.0), axis=1,
                              keepdims=True))
    pw = {u_: pre[u_]["a_b"] for u_ in units}
    inv = {u_: jnp.where(eye, 1.0, 0.0) - pre[u_]["a_b"] for u_ in units}
    for _ in range(5):
        for u_ in units:
            pb = pw[u_].astype(BF16)
            pw[u_] = _dot(pb, pb)
        for u_ in units:
            inv[u_] = inv[u_] + _dot(inv[u_].astype(BF16), pw[u_].astype(BF16))
    av = {u_: _dot(pre[u_]["a_k"], pre[u_]["v_bd"]) for u_ in units}
    invb = {u_: inv[u_].astype(BF16) for u_ in units}
    yv = {u_: _dot(pre[u_]["r_k"], pre[u_]["v_bd"]) for u_ in units}
    groups = range(ngroups)
    for ci in range(CHUNKS_PER_STEP):
        h0 = {c: h_ref[c] for c in groups}
        h0b = {c: h0[c].astype(BF16) for c in groups}
        rhs = {c: (_dot(pre[ci, c]["kq"], h0b[c]) + av[ci, c]).astype(BF16) for c in groups}
        yh = {c: _dot(pre[ci, c]["rq"], h0b[c]) for c in groups}
        ub = {c: _dot(invb[ci, c], rhs[c]).astype(BF16) for c in groups}
        for c in groups:
            p_ = pre[ci, c]
            y = yh[c] + yv[ci, c] - _dot(p_["r_b"], ub[c])
            y_ref[p_["rs"], p_["sl"]] = sum(y[hl * cs:(hl + 1) * cs] for hl in range(1, RW_GROUP)) + y[0:cs]
        for c in groups:
            p_ = pre[ci, c]
            upd = _dot(p_["tail_t"], jnp.concatenate([p_["v_bd"], ub[c]], axis=0))
            h_ref[c] = p_["decay_col"] * h0[c] + upd

    @pl.when(t == nchunks - 1)
    def _():
        hout_ref[...] = h_ref[...]


def _rwkv_chunked(r, lw, k, v, kk, b):
    n, t, _ = r.shape
    rows = RW_CHUNK * CHUNKS_PER_STEP
    assert t % rows == 0 and RW_CHUNK == RW_HEAD
    nchunks = t // rows
    seq = pl.BlockSpec((None, rows, RW_WIDTH), lambda i, j: (i, j, 0))
    gw = RW_GROUP * RW_HEAD
    ngroups = RW_HEADS // RW_GROUP
    st = pl.BlockSpec((None, ngroups, gw, gw), lambda i, j: (i, 0, 0, 0))
    y, h = pl.pallas_call(
        functools.partial(_rwkv_chunk_kernel, nchunks=nchunks),
        grid=(n, nchunks),
        in_specs=[seq] * 6,
        out_specs=[seq, st],
        out_shape=[jax.ShapeDtypeStruct((n, t, RW_WIDTH), F32),
                   jax.ShapeDtypeStruct((n, ngroups, gw, gw), F32)],
        scratch_shapes=[pltpu.VMEM((ngroups, gw, gw), F32)],
        compiler_params=_cparams(("parallel", "arbitrary")),
        name="rwkv_chunked",
    )(r, lw, k, v, kk, b)
    h = h.reshape(n, ngroups, RW_GROUP, RW_HEAD, RW_GROUP, RW_HEAD)
    hd = jnp.stack([h[:, :, hl, :, hl, :] for hl in range(RW_GROUP)], axis=2)
    return y, jnp.swapaxes(hd, 3, 4).reshape(n, RW_HEADS, RW_HEAD, RW_HEAD)


def _rwkv_post_kernel(y_ref, r_ref, k_ref, v_ref, g_ref, rk_ref, lg_ref, lb_ref, bones_ref, o_ref):
    bones = bones_ref[...]
    y = y_ref[...]
    d = y - _seg_sum(y, bones) * (1.0 / RW_HEAD)
    var = _seg_sum(d * d, bones) * (1.0 / RW_HEAD)
    yn = d * lax.rsqrt(var + GN_EPS) * lg_ref[...] + lb_ref[...]
    bonus = _seg_sum(r_ref[...] * k_ref[...] * rk_ref[...], bones) * v_ref[...]
    o_ref[...] = ((yn + bonus) * g_ref[...]).astype(o_ref.dtype)


def _rwkv_post(y, r, k, v, g, prm, tm=256):
    m = y.shape[0]
    tm = _tile(m, tm)
    big = pl.BlockSpec((tm, RW_WIDTH), lambda i: (i, 0))
    row = pl.BlockSpec((1, RW_WIDTH), lambda i: (0, 0))
    return pl.pallas_call(
        _rwkv_post_kernel,
        grid=(m // tm,),
        in_specs=[big] * 5 + [row] * 3 + [pl.BlockSpec((SEG_CHUNK, SEG_CHUNK), lambda i: (0, 0))],
        out_specs=big,
        out_shape=jax.ShapeDtypeStruct((m, RW_WIDTH), BF16),
        compiler_params=_cparams(("parallel",)),
        name="rwkv_post",
    )(y, r, k, v, g, prm["r_k"], prm["ln_g"], prm["ln_b"], prm["bones"])


def _state_to_stacked(s):
    n = s.shape[0]
    s = s.reshape(n, 4, 4, RW_HEAD, RW_HEAD)
    return jnp.transpose(s, (0, 1, 3, 2, 4)).reshape(n, 4 * RW_HEAD, SEG_CHUNK)


def _state_from_stacked(s):
    n = s.shape[0]
    s = s.reshape(n, 4, RW_HEAD, 4, RW_HEAD)
    return jnp.transpose(s, (0, 1, 3, 2, 4)).reshape(n, RW_HEADS, RW_HEAD, RW_HEAD)


def _top3_rows(sc, idx, nvalid_mask):
    big = float(sc.shape[0])
    sc = jnp.where(nvalid_mask, sc, NEG)
    sel = jnp.zeros(sc.shape, F32)
    for _ in range(MOBA_TOPK):
        mx = jnp.max(sc, axis=0, keepdims=True)
        first = jnp.min(jnp.where(sc == mx, idx, big), axis=0, keepdims=True)
        hit = (idx == first) & (mx > 0.5 * NEG)
        sel = jnp.where(hit, 1.0, sel)
        sc = jnp.where(hit, NEG, sc)
    return sel


def _moba_prompt_kernel(q_ref, k_ref, v_ref, bown_ref, bprev_ref, bfar_ref, o_ref,
                        kb_ref, vt_ref, kmh_ref, kml_ref, sel_ref, m_ref, acc_ref, s_ref, *, nb):
    i = pl.program_id(1)
    blk = MOBA_BLOCK

    @pl.when(i == 0)
    def _():
        def prep(j, c):
            rows = pl.ds(pl.multiple_of(j * blk, blk), blk)
            kj = k_ref[rows, :]
            kb_ref[rows, :] = kj.astype(BF16)
            km = jnp.sum(kj, axis=0, keepdims=True) * (1.0 / blk)
            hi, lo = _split(km)
            kmh_ref[pl.ds(j, 1), :] = hi.astype(F32)
            kml_ref[pl.ds(j, 1), :] = lo.astype(F32)
            vt_ref[j] = v_ref[rows, :].T.astype(BF16)
            return c
        lax.fori_loop(0, nb, prep, 0)

    qt = q_ref[...].T
    qh, ql = _split(qt)
    kmh = kmh_ref[...].astype(BF16)
    kml = kml_ref[...].astype(BF16)
    sc = _dot(kmh, qh) + _dot(kmh, ql) + _dot(kml, qh)
    bidx = lax.broadcasted_iota(jnp.int32, sc.shape, 0)
    sel_ref[...] = (_top3_rows(sc, bidx.astype(F32), bidx < i) - 1.0) * (-NEG)

    qs = (qt * (HEAD_DIM ** -0.5 * LOG2E)).astype(BF16)

    def keys(j, n):
        return kb_ref[pl.ds(pl.multiple_of(j * blk, blk), n * blk), :]

    def values_t(j, n):
        return vt_ref[j] if n == 1 else jnp.concatenate([vt_ref[j + g] for g in range(n)], axis=1)

    bfar = bfar_ref[0:1, 0:1]
    kidx = lax.broadcasted_iota(jnp.int32, (blk, blk), 0)
    qidx = lax.broadcasted_iota(jnp.int32, (blk, blk), 1)
    causal = kidx <= qidx

    def mask_rows(j, n):
        rows = [jnp.broadcast_to(sel_ref[pl.ds(j + g, 1), :], (blk, blk)) for g in range(n)]
        return rows[0] if n == 1 else jnp.concatenate(rows, axis=0)

    def far_scores(j, n):
        return _dot(keys(j, n), qs) + mask_rows(j, n)

    def near(ref):
        return (ref[...] - bfar) * LOG2E

    n_far = jnp.maximum(i - 1, 0)
    n_quads = n_far // 4
    rem0 = n_quads * 4

    ip = jnp.maximum(i - 1, 0)
    jr = [jnp.minimum(rem0 + r, nb - 1) for r in range(3)]
    absent = [jnp.where(rem0 + r < n_far, 0.0, NEG) for r in range(3)]
    bias_a = jnp.concatenate([near(bprev_ref) + mask_rows(ip, 1), jnp.where(causal, near(bown_ref), NEG)], axis=0)
    scores = [_dot(jnp.concatenate([keys(ip, 1), keys(i, 1)], axis=0), qs) + bias_a]
    scores += [_dot(keys(jr[r], 1), qs) + (mask_rows(jr[r], 1) + absent[r]) for r in range(3)]
    s_ref[...] = far_scores(0, 4)
    vts = [jnp.concatenate([vt_ref[ip], vt_ref[i]], axis=1)] + [vt_ref[jr[r]] for r in range(3)]
    m_loc = [jnp.max(s, axis=0, keepdims=True) for s in scores]
    prob = [jnp.exp2(s - m) for s, m in zip(scores, m_loc)]
    l_loc = [jnp.broadcast_to(jnp.sum(p, axis=0, keepdims=True), (ONES_ROWS, blk)) for p in prob]
    pv = [_dot(vt, p.astype(BF16)) for vt, p in zip(vts, prob)]
    m_new = jnp.maximum(jnp.maximum(m_loc[0], m_loc[1]), jnp.maximum(m_loc[2], m_loc[3]))
    acc = None
    for m, o, l in zip(m_loc, pv, l_loc):
        term = jnp.exp2(m - m_new) * jnp.concatenate([o, l], axis=0)
        acc = term if acc is None else acc + term
    acc_ref[...] = acc
    m_ref[...] = m_new


    def far_quad(jq, c):
        s_next = far_scores(4 * jnp.minimum(jq + 1, n_quads - 1), 4)
        per = 4 // FAR_PARTS
        hks = range(FAR_PARTS)
        sp = [s_ref[hk * per * blk:(hk + 1) * per * blk, :] for hk in hks]
        m_loc = [jnp.max(s, axis=0, keepdims=True) for s in sp]
        prob = [jnp.exp2(s - m) for s, m in zip(sp, m_loc)]
        l_loc = [jnp.broadcast_to(jnp.sum(p, axis=0, keepdims=True), (ONES_ROWS, blk)) for p in prob]
        pv = [_dot(values_t(4 * jq + per * hk, per), prob[hk].astype(BF16)) for hk in hks]
        m_old = m_ref[...]
        m_new = m_old
        for m in m_loc:
            m_new = jnp.maximum(m_new, m)
        acc = jnp.exp2(m_old - m_new) * acc_ref[...]
        for m, o, l in zip(m_loc, pv, l_loc):
            acc = acc + jnp.exp2(m - m_new) * jnp.concatenate([o, l], axis=0)
        acc_ref[...] = acc
        m_ref[...] = m_new
        s_ref[...] = s_next
        return c
    lax.fori_loop(0, n_quads, far_quad, 0)

    acc = acc_ref[...]
    o_ref[...] = (acc[0:HEAD_DIM] / acc[HEAD_DIM:HEAD_DIM + 1]).T.astype(o_ref.dtype)


def _moba_prompt(q, k, v, bias):
    t = q.shape[0]
    assert t % MOBA_BLOCK == 0
    nb = t // MOBA_BLOCK
    assert nb >= 4
    blk = MOBA_BLOCK
    h8 = ATTN_HEADS
    tile = pl.BlockSpec((None, blk, blk), lambda h, i: (h, 0, 0))
    return pl.pallas_call(
        functools.partial(_moba_prompt_kernel, nb=nb),
        grid=(h8, nb),
        in_specs=[pl.BlockSpec((blk, HEAD_DIM), lambda h, i: (i, h)),
                  pl.BlockSpec((t, HEAD_DIM), lambda h, i: (0, h)),
                  pl.BlockSpec((t, HEAD_DIM), lambda h, i: (0, h)),
                  tile, tile,
                  pl.BlockSpec((None, 1, HEAD_DIM), lambda h, i: (h, 0, 0))],
        out_specs=pl.BlockSpec((blk, HEAD_DIM), lambda h, i: (i, h)),
        out_shape=jax.ShapeDtypeStruct((t, ATTN_WIDTH), BF16),
        scratch_shapes=[pltpu.VMEM((t, HEAD_DIM), BF16),
                        pltpu.VMEM((nb, HEAD_DIM, blk), BF16),
                        pltpu.VMEM((nb, HEAD_DIM), F32),
                        pltpu.VMEM((nb, HEAD_DIM), F32),
                        pltpu.VMEM((nb, blk), F32),
                        pltpu.VMEM((1, blk), F32),
                        pltpu.VMEM((HEAD_DIM + ONES_ROWS, blk), F32),
                        pltpu.VMEM((4 * blk, blk), F32)],
        compiler_params=_cparams(("arbitrary", "arbitrary")),
        name="moba_prompt",
    )(q, k, v, bias["own_t"], bias["prev_t"], bias["far"])


def _diag_extract(s):
    lane = lax.broadcasted_iota(jnp.int32, (ATTN_HEADS, s.shape[1]), 1) % ATTN_HEADS
    out = jnp.zeros((ATTN_HEADS, s.shape[1]), F32)
    for hp in range(ATTN_HEADS):
        out = out + jnp.where(lane == hp, s[hp * 8:(hp + 1) * 8, :], 0.0)
    return out


def _diag_expand(p):
    lane = lax.broadcasted_iota(jnp.int32, p.shape, 1) % ATTN_HEADS
    return jnp.concatenate([jnp.where(lane == hp, p, 0.0) for hp in range(ATTN_HEADS)], axis=0)


def _class_allreduce(x, op):
    for sh in (8, 16, 32, 64):
        x = op(x, pltpu.roll(x, sh, 1))
    return x


def _class_allreduce_many(xs, op):
    for sh in (8, 16, 32, 64):
        xs = [op(x, pltpu.roll(x, sh, 1)) for x in xs]
    return xs


def _fold_tiles(x, op):
    out = x[:, 0:128]
    for c in range(1, x.shape[1] // 128):
        out = op(out, x[:, c * 128:(c + 1) * 128])
    return out


def _moba_sample_kernel(pt_ref, q_ref, ck_ref, cv_ref, kn_ref, vn_ref, blast_ref, bown_ref, bfar_ref, o_ref,
                        sc_ref, bsum_ref, bmax_ref, bexp_ref, selx_ref, m_ref, li_ref, acc_ref,
                        page_ref, sem_ref, *, nblk, grp, nbatch):
    ph = pl.program_id(1)
    step = pl.program_id(2)
    nsteps = nblk // grp
    npages = 2 * grp
    ring = SAMPLE_LOOKAHEAD + 1
    total = nbatch * 2 * nsteps
    g_lin = (pl.program_id(0) * 2 + ph) * nsteps + step

    def page_copy(src_ref, page, slot):
        return pltpu.make_async_copy(src_ref.at[0, page], page_ref.at[slot], sem_ref.at[slot])

    def request(gl):
        b2 = gl // (2 * nsteps)
        r2 = gl % (2 * nsteps)
        ph2 = r2 // nsteps
        p0 = (r2 % nsteps) * npages
        base = (gl % ring) * npages
        for src_ref, which in ((ck_ref, 0), (cv_ref, 1)):
            @pl.when(ph2 == which)
            def _():
                for u in range(npages):
                    page_copy(src_ref, pt_ref[b2, p0 + u], base + u).start()

    @pl.when(g_lin == 0)
    def _():
        for gl in range(min(SAMPLE_LOOKAHEAD, total)):
            request(gl)

    @pl.when(g_lin + SAMPLE_LOOKAHEAD < total)
    def _():
        request(g_lin + SAMPLE_LOOKAHEAD)

    slot0 = (g_lin % ring) * npages
    for u in range(npages):
        page_copy(ck_ref, 0, slot0 + u).wait()
    k_refs = v_refs = [page_ref.at[slot0 + u] for u in range(npages)]
    ntok = 8
    rows = PAGE_SIZE * ATTN_HEADS
    nkeys = 2 * rows
    far16 = jnp.concatenate([bfar_ref[...]] * (nkeys // 128), axis=1)
    lane_blk = lax.broadcasted_iota(jnp.int32, (ntok, 128), 1) // 8

    def block_bias(j):
        return jnp.where(j == nblk - 1, blast_ref[...], far16)

    def tile16(x):
        return jnp.concatenate([x] * (nkeys // 128), axis=1)

    def compact_tile(j):
        return pl.ds(pl.multiple_of((j // 16) * 128, 128), 128)

    def put_compact(ref, j, x):
        ref[:, compact_tile(j)] = jnp.where(lane_blk == j % 16, x, ref[:, compact_tile(j)])

    @pl.when((ph == 0) & (step == 0))
    def _():
        bmax_ref[...] = jnp.zeros(bmax_ref.shape, F32)
        bexp_ref[...] = jnp.zeros(bexp_ref.shape, F32)

    @pl.when(ph == 0)
    def _():
        qs = (q_ref[...] * (HEAD_DIM ** -0.5)).astype(BF16)
        gs = range(grp)
        js = [step * grp + g for g in gs]
        raw = []
        for g in gs:
            k0 = k_refs[2 * g][...]
            k1 = k_refs[2 * g + 1][...]
            bsum_ref[pl.ds(pl.multiple_of(js[g] * 8, 8), 8), :] = jnp.sum(k0, axis=0) + jnp.sum(k1, axis=0)
            k2 = jnp.concatenate([k0.reshape(rows, HEAD_DIM), k1.reshape(rows, HEAD_DIM)], axis=0).astype(BF16)
            raw.append(_dot_nt(qs, k2))
        sc = [_diag_extract(raw[g]) + block_bias(js[g]) for g in gs]
        for g in gs:
            sc_ref[js[g]] = sc[g]
        bm = _class_allreduce_many([_fold_tiles(sc[g], jnp.maximum) for g in gs], jnp.maximum)
        be = _class_allreduce_many([_fold_tiles(jnp.exp(sc[g] - tile16(bm[g])), jnp.add) for g in gs], jnp.add)
        for g in gs:
            put_compact(bmax_ref, js[g], bm[g])
            put_compact(bexp_ref, js[g], be[g])

    @pl.when((ph == 0) & (step == nsteps - 1))
    def _():
        q = q_ref[...]
        qh, ql = _split(q)
        bh, bl = _split(bsum_ref[...] * (1.0 / MOBA_BLOCK))
        scx = _diag_extract(_dot_nt(qh, bh) + _dot_nt(qh, bl) + _dot_nt(ql, bh))
        width = nblk * 8
        jidx = (lax.broadcasted_iota(jnp.int32, (ntok, width), 1) // 8).astype(F32)

        def creduce(x, op):
            y = _class_allreduce(_fold_tiles(x, op), op)
            return jnp.concatenate([y] * (width // 128), axis=1)

        selx = jnp.zeros((ntok, width), F32)
        for _ in range(MOBA_TOPK):
            mx = creduce(scx, jnp.maximum)
            first = creduce(jnp.where(scx == mx, jidx, float(nblk)), jnp.minimum)
            hit = jidx == first
            selx = jnp.where(hit, 1.0, selx)
            scx = jnp.where(hit, NEG, scx)
        selx_ref[...] = selx
        picked = selx > 0.5

        qs = (q * (HEAD_DIM ** -0.5)).astype(BF16)
        so = _diag_extract(_dot_nt(qs, kn_ref[...].astype(BF16))) + bown_ref[...]
        tq = lax.broadcasted_iota(jnp.int32, so.shape, 0)
        tk = lax.broadcasted_iota(jnp.int32, so.shape, 1) // 8
        so = jnp.concatenate([jnp.where(tk <= tq, so, NEG), jnp.full((ntok, 64), NEG, F32)], axis=1)

        bmax = bmax_ref[...]
        m = jnp.maximum(_class_allreduce(so, jnp.maximum),
                        creduce(jnp.where(picked, bmax, NEG), jnp.maximum)[:, 0:128])
        mw = jnp.concatenate([m] * (width // 128), axis=1)
        l_blocks = creduce(jnp.where(picked, bexp_ref[...] * jnp.exp(bmax - mw), 0.0), jnp.add)[:, 0:128]
        po = jnp.exp(so - m)
        li = 1.0 / (_class_allreduce(po, jnp.add) + l_blocks)
        m_ref[...] = m
        li_ref[...] = li
        pfull = _diag_expand((po * li)[:, 0:64]).astype(BF16)
        acc_ref[...] = _dot(pfull, vn_ref[...].astype(BF16))

    @pl.when(ph == 1)
    def _():
        m16 = tile16(m_ref[...])
        li16 = tile16(li_ref[...])
        gs = range(grp)
        js = [step * grp + g for g in gs]
        mine = [jnp.where(lane_blk == js[g] % 16, selx_ref[:, compact_tile(js[g])], 0.0) for g in gs]
        picked = _class_allreduce_many(mine, jnp.add)
        p = [jnp.where(tile16(picked[g]) > 0.5, jnp.exp(sc_ref[js[g]] - m16) * li16, 0.0) for g in gs]
        pv = []
        for g in gs:
            v2 = jnp.concatenate([v_refs[2 * g][...].reshape(rows, HEAD_DIM),
                                  v_refs[2 * g + 1][...].reshape(rows, HEAD_DIM)], axis=0).astype(BF16)
            pv.append(_dot(_diag_expand(p[g]).astype(BF16), v2))
        acc = acc_ref[...]
        for g in gs:
            acc = acc + pv[g]
        acc_ref[...] = acc

    @pl.when((ph == 1) & (step == nsteps - 1))
    def _():
        o_ref[...] = acc_ref[...]


def _moba_sample(qht, knew, vnew, cache_k, cache_v, page_table, bias):
    nb_, n_pages = page_table.shape
    assert n_pages % PAGES_PER_BLOCK == 0 and PAGES_PER_BLOCK == 2
    nblk = n_pages // 2
    assert nblk >= MOBA_TOPK and (nblk * 8) % 128 == 0
    nkeys = 2 * PAGE_SIZE * ATTN_HEADS
    grp = SAMPLE_BLOCKS_PER_STEP
    assert nblk % grp == 0
    nsteps = nblk // grp

    per_b = pl.BlockSpec((None, 64, HEAD_DIM), lambda b, p, j, pt: (b, 0, 0))
    cst = lambda r, c: pl.BlockSpec((r, c), lambda b, p, j, pt: (0, 0))
    hbm = pl.BlockSpec(memory_space=pl.ANY)
    nslots = (SAMPLE_LOOKAHEAD + 1) * 2 * grp
    grid_spec = pltpu.PrefetchScalarGridSpec(
        num_scalar_prefetch=1,
        grid=(nb_, 2, nsteps),
        in_specs=[per_b, hbm, hbm, per_b, per_b, cst(8, nkeys), cst(8, 64), cst(1, 128)],
        out_specs=per_b,
        scratch_shapes=[pltpu.VMEM((nblk, 8, nkeys), F32),
                        pltpu.VMEM((nblk * 8, HEAD_DIM), F32),
                        pltpu.VMEM((8, nblk * 8), F32),
                        pltpu.VMEM((8, nblk * 8), F32),
                        pltpu.VMEM((8, nblk * 8), F32),
                        pltpu.VMEM((8, 128), F32),
                        pltpu.VMEM((8, 128), F32),
                        pltpu.VMEM((64, HEAD_DIM), F32),
                        pltpu.VMEM((nslots, PAGE_SIZE, ATTN_HEADS, HEAD_DIM), F32),
                        pltpu.SemaphoreType.DMA((nslots,))])
    return pl.pallas_call(
        functools.partial(_moba_sample_kernel, nblk=nblk, grp=grp, nbatch=nb_),
        grid_spec=grid_spec,
        out_shape=jax.ShapeDtypeStruct((nb_, 64, HEAD_DIM), F32),
        compiler_params=_cparams(("arbitrary", "arbitrary", "arbitrary")),
        name="moba_sample",
    )(page_table, qht, cache_k, cache_v, knew, vnew, bias["last_s"], bias["own_s"], bias["far_s"])


def _bias_of_distance(dist, rel_bias):
    dist = jnp.maximum(dist, 0)
    exact = N_BUCKETS // 2
    log_ratio = jnp.log(jnp.maximum(dist, 1).astype(F32) / exact) / math.log(MAX_DISTANCE / exact)
    large = jnp.minimum(exact + (log_ratio * (N_BUCKETS - exact)).astype(jnp.int32), N_BUCKETS - 1)
    return rel_bias[jnp.where(dist < exact, dist, large)]


def _bias_tiles(rel_bias, past_len, dec_seq):
    blk = MOBA_BLOCK
    val = _bias_of_distance(jnp.arange(2 * blk), rel_bias)
    val_t = val.T

    def toeplitz(ext):
        n2 = ext.shape[1]
        n = n2 // 2
        return jnp.tile(ext, (1, n))[:, :n * (n2 - 1)].reshape(ext.shape[0], n, n2 - 1)[:, :, :n]

    own_t = toeplitz(val_t)
    prev_t = toeplitz(jnp.roll(val_t, -blk, axis=1))
    far_row = val[2 * blk - 1]
    far = jnp.broadcast_to(far_row[:, None, None], (ATTN_HEADS, 1, HEAD_DIM))
    assert past_len % blk == 0 and 2 * blk - 1 >= MAX_DISTANCE and dec_seq < blk
    last = jnp.stack([val[t + 1:t + 1 + blk][::-1] for t in range(dec_seq)])
    last_s = last.reshape(dec_seq, blk * ATTN_HEADS)
    own = jnp.stack([jnp.concatenate([val[:t + 1][::-1], jnp.zeros((dec_seq - 1 - t, ATTN_HEADS), F32)])
                     for t in range(dec_seq)])
    own_s = own.reshape(dec_seq, dec_seq * ATTN_HEADS)
    far_s = jnp.tile(far_row, 128 // ATTN_HEADS)[None, :]
    return dict(own_t=own_t, prev_t=prev_t, far=far, last_s=last_s, own_s=own_s, far_s=far_s)


def _np_consts():
    seg = np.arange(SEG_CHUNK) // RW_HEAD
    bones = (seg[:, None] == seg[None, :]).astype(np.float32)
    idx = np.arange(SEG_CHUNK) % RW_HEAD
    eye4 = (idx[:, None] == idx[None, :]).astype(np.float32)
    return bones, eye4


def _pad_rows(w, row0, total):
    return jnp.zeros((total, w.shape[1]), BF16).at[row0:row0 + w.shape[0]].set(w.astype(BF16))


def kernel(x_prompt, x_sample, cache_k, cache_v, page_table, state_wkv, state_shift, w_in, rel_bias, rw_mu, rw_w0, rw_w2, rw_a0, rw_a2, rw_g2, rw_k_k, rw_k_a, rw_r_k, rw_ln_g, rw_ln_b, w_up_attn, w_up_rwkv, w_o, ln1_g, ln1_b, w_ffn_gate, w_ffn_up, w_ffn_down, ln2_g, ln2_b):
    assert x_prompt.shape[0] == 1 and w_in.shape[0] == DEPTH == 1
    t_p = x_prompt.shape[1]
    nb_s, t_s, _ = x_sample.shape
    assert t_s == 8
    past_len = page_table.shape[1] * PAGE_SIZE

    w_main = w_in[0].astype(BF16)
    w_lora = jnp.pad(w_main[:, LORA_COL:GATE_COL], ((0, 0), (0, LORA_PAD - LORA_COLS)))
    w_gate = w_main[:, GATE_COL:]
    wb_up_attn, wb_up_rwkv, wb_o = w_up_attn[0].astype(BF16), w_up_rwkv[0].astype(BF16), w_o[0].astype(BF16)
    wb_gate, wb_up, wb_down = w_ffn_gate[0].astype(BF16), w_ffn_up[0].astype(BF16), w_ffn_down[0].astype(BF16)
    bones_np, eye_np = _np_consts()
    mu = rw_mu[0]
    prm = dict(
        mu=mu[None, :3 * RW_WIDTH],
        mul=jnp.pad(mu[3 * RW_WIDTH:], (0, LORA_PAD - LORA_COLS))[None, :],
        w0=rw_w0, a0=rw_a0,
        w2=_pad_rows(rw_w2[0], 0, LORA_PAD),
        a2=_pad_rows(rw_a2[0], DECAY_LORA, LORA_PAD),
        g2=_pad_rows(rw_g2[0], DECAY_LORA + AAA_LORA, LORA_PAD),
        k_k=rw_k_k, k_a=rw_k_a, r_k=rw_r_k.reshape(1, RW_WIDTH), ln_g=rw_ln_g, ln_b=rw_ln_b,
        bones=jnp.asarray(bones_np, BF16), eye4=jnp.asarray(eye_np, F32))
    bias = _bias_tiles(rel_bias, past_len, t_s)

    def group(x2d, nseq, tseq, zprev, zlprev, s0, attend, tt, tb):
        zq, new_k, new_v, zrkv = _proj_split(x2d, w_main, (ATTN_WIDTH, ATTN_WIDTH, ATTN_WIDTH, 3 * RW_WIDTH))
        zl = _proj(x2d, w_lora, 0, LORA_PAD)
        zg = _proj(x2d, w_gate, 0, 2 * D_MODEL)
        o_attn = attend(zq, new_k, new_v)
        r, w, k, v, kk, b, g = _rwkv_prep(zrkv.reshape(nseq, tseq, 3 * RW_WIDTH),
                                          zl.reshape(nseq, tseq, LORA_PAD), zprev, zlprev, prm, tt)
        if s0 is None:
            y, s_fin = _rwkv_chunked(r, w, k, v, kk, b)
        else:
            y, s_fin = _rwkv_scan(r, w, k, v, kk, b, s0, prm, tb)
            s_fin = _state_from_stacked(s_fin)
        flat = lambda u: u.reshape(nseq * tseq, RW_WIDTH)
        o_rwkv = _rwkv_post(flat(y), flat(r), flat(k), flat(v), flat(g), prm)
        mixed = _merge(o_attn, o_rwkv, zg, wb_up_attn, wb_up_rwkv)
        h = _proj_ln(mixed, wb_o, x2d, ln1_g, ln1_b, tm=1024)
        act = _ffn_up(h, wb_gate, wb_up)
        out = _proj_ln(act, wb_down, h, ln2_g, ln2_b, tm=512)
        return out, new_k, new_v, s_fin

    xp = x_prompt[0]
    yp, kp, vp, sp = group(
        xp, 1, t_p,
        jnp.zeros((1, 1, 3 * RW_WIDTH), F32), jnp.zeros((1, 1, LORA_PAD), F32),
        None, lambda zq, zk, zv: _moba_prompt(zq, zk, zv, bias), 256, 8)

    xs = x_sample.reshape(nb_s * t_s, D_MODEL)
    sh = state_shift[0]
    zprev = _proj(sh, w_main, RKV_COL, 3 * RW_WIDTH)[:, None, :]
    zlprev = _proj(sh, w_lora, 0, LORA_PAD)[:, None, :]

    def attend_sample(zq, zk, zv):
        q = zq.reshape(nb_s, t_s, ATTN_HEADS, HEAD_DIM)
        qht = jnp.transpose(q, (0, 2, 1, 3)).reshape(nb_s, ATTN_HEADS * t_s, HEAD_DIM)
        kn = zk.reshape(nb_s, t_s * ATTN_HEADS, HEAD_DIM)
        vn = zv.reshape(nb_s, t_s * ATTN_HEADS, HEAD_DIM)
        o = _moba_sample(qht, kn, vn, cache_k, cache_v, page_table, bias)
        o = jnp.transpose(o.reshape(nb_s, ATTN_HEADS, t_s, HEAD_DIM), (0, 2, 1, 3))
        return o.reshape(nb_s * t_s, ATTN_WIDTH).astype(BF16)

    ys, ks, vs, ss = group(xs, nb_s, t_s, zprev, zlprev, _state_to_stacked(state_wkv[0]),
                           attend_sample, 8, 8)

    return (yp[None], ys.reshape(nb_s, t_s, D_MODEL),
            kp.reshape(1, 1, t_p, ATTN_HEADS, HEAD_DIM), vp.reshape(1, 1, t_p, ATTN_HEADS, HEAD_DIM),
            sp[None], xp[None, -1:, :],
            ks.reshape(1, nb_s, t_s, ATTN_HEADS, HEAD_DIM), vs.reshape(1, nb_s, t_s, ATTN_HEADS, HEAD_DIM),
            ss[None], x_sample[None, :, -1, :])
```

```python
import functools
import math

import numpy as np
import jax
import jax.numpy as jnp
from jax import lax
from jax.experimental import pallas as pl
from jax.experimental.pallas import tpu as pltpu

F32 = jnp.float32
BF16 = jnp.bfloat16

D_MODEL = 2048
HEAD_DIM = 128
ATTN_HEADS = D_MODEL // 256
ATTN_WIDTH = ATTN_HEADS * HEAD_DIM
MOBA_BLOCK = 256
MOBA_TOPK = 3
N_BUCKETS = 32
MAX_DISTANCE = 128
PAGE_SIZE = 128
RW_HEAD = 64
RW_HEADS = D_MODEL // 128
RW_WIDTH = RW_HEADS * RW_HEAD
DECAY_LORA = 96
AAA_LORA = 96
GATE_LORA = 256
LORA_COLS = DECAY_LORA + AAA_LORA + GATE_LORA
LORA_PAD = 512
GN_EPS = RW_HEAD * 1e-5
D_FF = 5632
LN_EPS = 1e-5
SMALL_M = 256
WIDE_TILE_BYTES = 8 * 1024 * 1024
LN_ROWS = 128
DEPTH = 1
DEEPNORM_ALPHA = (2 * DEPTH) ** 0.25

RKV_COL = 3 * ATTN_WIDTH
LORA_COL = RKV_COL + 3 * RW_WIDTH
GATE_COL = LORA_COL + LORA_COLS
MAIN_COLS = LORA_COL

NEG = -1e30
LOG2E = math.log2(math.e)
FAR_PARTS = 4
ONES_ROWS = 8
SEG_CHUNK = 256
PAGES_PER_BLOCK = MOBA_BLOCK // PAGE_SIZE
SAMPLE_BLOCKS_PER_STEP = 8
SAMPLE_LOOKAHEAD = 2
RW_CHUNK = 64
SCAN_SEQS_PER_STEP = 4
RW_GROUP = 4
CHUNKS_PER_STEP = 2
VMEM_LIMIT = 56 * 1024 * 1024


def _cparams(sem):
    return pltpu.CompilerParams(dimension_semantics=sem, vmem_limit_bytes=VMEM_LIMIT)


def _dot(a, b):
    return jnp.dot(a, b, preferred_element_type=F32)


def _dot_nt(a, b):
    return lax.dot_general(a, b, (((1,), (1,)), ((), ())), preferred_element_type=F32)


def _split(x):
    hi = x.astype(BF16)
    lo = (x - hi.astype(F32)).astype(BF16)
    return hi, lo


def _sigmoid(x):
    return 1.0 / (1.0 + jnp.exp(-x))


def _tile(m, pref):
    t = min(m, pref)
    assert m % t == 0, (m, pref)
    return t


def _wide_tn(m, tn, n, k=D_MODEL):
    if m > SMALL_M:
        return tn
    for cand in (2048, 1024, 512):
        if cand >= tn and n % cand == 0 and k * cand * 2 <= WIDE_TILE_BYTES:
            return cand
    return tn


def _proj_kernel(x_ref, w_ref, o_ref, xb_ref):
    @pl.when(pl.program_id(1) == 0)
    def _():
        xb_ref[...] = x_ref[...].astype(BF16)

    o_ref[...] = _dot(xb_ref[...], w_ref[...].astype(BF16)).astype(o_ref.dtype)


def _proj(x, w, col0, ncols, tm=1024, tn=512):
    m, k = x.shape
    tm = _tile(m, tm)
    tn = _tile(ncols, _wide_tn(m, tn, ncols))
    assert col0 % tn == 0
    c0 = col0 // tn
    return pl.pallas_call(
        _proj_kernel,
        grid=(m // tm, ncols // tn),
        in_specs=[pl.BlockSpec((tm, k), lambda i, j: (i, 0)),
                  pl.BlockSpec((k, tn), lambda i, j: (0, c0 + j))],
        out_specs=pl.BlockSpec((tm, tn), lambda i, j: (i, j)),
        out_shape=jax.ShapeDtypeStruct((m, ncols), F32),
        scratch_shapes=[pltpu.VMEM((tm, k), BF16)],
        compiler_params=_cparams(("parallel", "arbitrary")),
        name="proj",
    )(x, w)


def _proj_split_kernel(x_ref, w_ref, *refs, bounds):
    o_refs, xb_ref = refs[:-1], refs[-1]
    j = pl.program_id(1)

    @pl.when(j == 0)
    def _():
        xb_ref[...] = x_ref[...].astype(BF16)

    acc = _dot(xb_ref[...], w_ref[...].astype(BF16))
    for o_ref, (lo, hi) in zip(o_refs, bounds):
        @pl.when((j >= lo) & (j < hi))
        def _():
            o_ref[...] = acc


def _proj_split(x, w, widths, tm=1024, tn=512):
    m, k = x.shape
    tm = _tile(m, tm)
    tn = _wide_tn(m, tn, math.gcd(*widths))
    assert all(wd % tn == 0 for wd in widths)
    edges = np.cumsum([0] + [wd // tn for wd in widths])
    bounds = [(int(edges[g]), int(edges[g + 1])) for g in range(len(widths))]

    def out_map(lo, hi):
        return lambda i, j: (i, jnp.clip(j - lo, 0, hi - lo - 1))

    return pl.pallas_call(
        functools.partial(_proj_split_kernel, bounds=bounds),
        grid=(m // tm, int(edges[-1])),
        in_specs=[pl.BlockSpec((tm, k), lambda i, j: (i, 0)),
                  pl.BlockSpec((k, tn), lambda i, j: (0, j))],
        out_specs=[pl.BlockSpec((tm, tn), out_map(lo, hi)) for lo, hi in bounds],
        out_shape=[jax.ShapeDtypeStruct((m, wd), F32) for wd in widths],
        scratch_shapes=[pltpu.VMEM((tm, k), BF16)],
        compiler_params=_cparams(("parallel", "arbitrary")),
        name="proj_split",
    )(x, w)


def _merge_kernel(oa_ref, or_ref, wa_ref, wr_ref, ga_ref, gr_ref, o_ref):
    a = _dot(oa_ref[...], wa_ref[...].astype(BF16))
    r = _dot(or_ref[...], wr_ref[...].astype(BF16))
    o_ref[...] = (_sigmoid(ga_ref[...]) * a + _sigmoid(gr_ref[...]) * r).astype(o_ref.dtype)


def _merge(o_attn, o_rwkv, zg, w_up_attn, w_up_rwkv, tm=1024, tn=512):
    m = o_attn.shape[0]
    tm = _tile(m, tm)
    tn = _wide_tn(m, tn, D_MODEL)
    nj = D_MODEL // tn
    return pl.pallas_call(
        _merge_kernel,
        grid=(m // tm, nj),
        in_specs=[pl.BlockSpec((tm, ATTN_WIDTH), lambda i, j: (i, 0)),
                  pl.BlockSpec((tm, RW_WIDTH), lambda i, j: (i, 0)),
                  pl.BlockSpec((ATTN_WIDTH, tn), lambda i, j: (0, j)),
                  pl.BlockSpec((RW_WIDTH, tn), lambda i, j: (0, j)),
                  pl.BlockSpec((tm, tn), lambda i, j: (i, j)),
                  pl.BlockSpec((tm, tn), lambda i, j: (i, j + nj))],
        out_specs=pl.BlockSpec((tm, tn), lambda i, j: (i, j)),
        out_shape=jax.ShapeDtypeStruct((m, D_MODEL), BF16),
        compiler_params=_cparams(("parallel", "arbitrary")),
        name="merge",
    )(o_attn, o_rwkv, w_up_attn, w_up_rwkv, zg, zg)


def _proj_ln_kernel(m_ref, w_ref, x_ref, g_ref, b_ref, o_ref, *, tn, nj):
    j = pl.program_id(1)
    col = pl.multiple_of(j * tn, tn)
    mb = m_ref[...].astype(BF16)
    o_ref[:, pl.ds(col, tn)] = DEEPNORM_ALPHA * x_ref[...] + _dot(mb, w_ref[...].astype(BF16))

    @pl.when(j == nj - 1)
    def _():
        ln_rows = min(LN_ROWS, o_ref.shape[0])
        assert o_ref.shape[0] % ln_rows == 0

        def norm_rows(c, carry):
            rows = pl.ds(pl.multiple_of(c * ln_rows, ln_rows), ln_rows)
            y = o_ref[rows, :]
            mu = jnp.mean(y, axis=-1, keepdims=True)
            d = y - mu
            var = jnp.mean(d * d, axis=-1, keepdims=True)
            o_ref[rows, :] = d * lax.rsqrt(var + LN_EPS) * g_ref[...] + b_ref[...]
            return carry
        lax.fori_loop(0, o_ref.shape[0] // ln_rows, norm_rows, 0)


def _proj_ln(mat, w, x, g, b, tm=512, tn=256):
    m, k = mat.shape
    tm = _tile(m, tm)
    tn = _wide_tn(m, tn, D_MODEL, k)
    nj = D_MODEL // tn
    return pl.pallas_call(
        functools.partial(_proj_ln_kernel, tn=tn, nj=nj),
        grid=(m // tm, nj),
        in_specs=[pl.BlockSpec((tm, k), lambda i, j: (i, 0)),
                  pl.BlockSpec((k, tn), lambda i, j: (0, j)),
                  pl.BlockSpec((tm, tn), lambda i, j: (i, j)),
                  pl.BlockSpec((1, D_MODEL), lambda i, j: (0, 0)),
                  pl.BlockSpec((1, D_MODEL), lambda i, j: (0, 0))],
        out_specs=pl.BlockSpec((tm, D_MODEL), lambda i, j: (i, 0)),
        out_shape=jax.ShapeDtypeStruct((m, D_MODEL), F32),
        compiler_params=_cparams(("parallel", "arbitrary")),
        name="proj_ln",
    )(mat, w, x, g, b)


def _ffn_up_kernel(h_ref, wg_ref, wu_ref, o_ref, hb_ref):
    @pl.when(pl.program_id(1) == 0)
    def _():
        hb_ref[...] = h_ref[...].astype(BF16)

    hb = hb_ref[...]
    a = _dot(hb, wg_ref[...].astype(BF16))
    u = _dot(hb, wu_ref[...].astype(BF16))
    o_ref[...] = (a * _sigmoid(a) * u).astype(o_ref.dtype)


def _ffn_up(h, wg, wu, tm=1024, tn=256):
    m = h.shape[0]
    tm = _tile(m, tm)
    tn = _wide_tn(m, tn, D_FF)
    return pl.pallas_call(
        _ffn_up_kernel,
        grid=(m // tm, D_FF // tn),
        in_specs=[pl.BlockSpec((tm, D_MODEL), lambda i, j: (i, 0)),
                  pl.BlockSpec((D_MODEL, tn), lambda i, j: (0, j)),
                  pl.BlockSpec((D_MODEL, tn), lambda i, j: (0, j))],
        out_specs=pl.BlockSpec((tm, tn), lambda i, j: (i, j)),
        out_shape=jax.ShapeDtypeStruct((m, D_FF), BF16),
        scratch_shapes=[pltpu.VMEM((tm, D_MODEL), BF16)],
        compiler_params=_cparams(("parallel", "arbitrary")),
        name="ffn_up",
    )(h, wg, wu)


def _seg_sum(x, bones):
    outs = []
    for c in range(x.shape[1] // SEG_CHUNK):
        hi, lo = _split(x[:, c * SEG_CHUNK:(c + 1) * SEG_CHUNK])
        outs.append(_dot(hi, bones) + _dot(lo, bones))
    return jnp.concatenate(outs, axis=1)


def _shifted(z, prev):
    zs = pltpu.roll(z, 1, 0)
    row = lax.broadcasted_iota(jnp.int32, z.shape, 0)
    return jnp.where(row == 0, prev, zs)


def _rwkv_prep_kernel(z_ref, zl_ref, zp_ref, zlp_ref, mu_ref, mul_ref, w0_ref, a0_ref, w2_ref, a2_ref,
                      g2_ref, kkw_ref, kaw_ref, bones_ref,
                      r_o, w_o, k_o, v_o, kk_o, b_o, g_o, prev_ref, prevl_ref):
    @pl.when(pl.program_id(1) == 0)
    def _():
        prev_ref[...] = zp_ref[...]
        prevl_ref[...] = zlp_ref[...]

    z = z_ref[...]
    zl = zl_ref[...]
    tt = z.shape[0]
    zm = z + (_shifted(z, prev_ref[...]) - z) * mu_ref[...]
    zlm = zl + (_shifted(zl, prevl_ref[...]) - zl) * mul_ref[...]
    prev_ref[...] = z[tt - 1:tt, :]
    prevl_ref[...] = zl[tt - 1:tt, :]

    r = zm[:, 0:RW_WIDTH]
    k = zm[:, RW_WIDTH:2 * RW_WIDTH]
    v = zm[:, 2 * RW_WIDTH:3 * RW_WIDTH]
    xw = w0_ref[...] + _dot(jnp.tanh(zlm).astype(BF16), w2_ref[...])
    nx = -xw
    softplus = jnp.maximum(nx, 0.0) + jnp.log(1.0 + jnp.exp(-jnp.abs(nx)))
    log_decay = -jnp.exp(-softplus - 0.5)
    a = _sigmoid(a0_ref[...] + _dot(zlm.astype(BF16), a2_ref[...]))
    g = _dot(_sigmoid(zlm).astype(BF16), g2_ref[...])
    kk = k * kkw_ref[...]
    ssq = _seg_sum(kk * kk, bones_ref[...])
    kk = kk / jnp.maximum(jnp.sqrt(ssq), 1e-12)
    r_o[...] = r
    w_o[...] = log_decay
    k_o[...] = k * (1.0 + (a - 1.0) * kaw_ref[...])
    v_o[...] = v
    kk_o[...] = kk
    b_o[...] = kk * a
    g_o[...] = g


def _rwkv_prep(z1, zl, zprev, zlprev, prm, tt):
    n, t, _ = z1.shape
    tt = _tile(t, tt)
    row = lambda c: pl.BlockSpec((1, c), lambda i, j: (0, 0))
    mat = lambda r, c: pl.BlockSpec((r, c), lambda i, j: (0, 0))
    seq = lambda c: pl.BlockSpec((None, tt, c), lambda i, j: (i, j, 0))
    outs = pl.pallas_call(
        _rwkv_prep_kernel,
        grid=(n, t // tt),
        in_specs=[seq(3 * RW_WIDTH),
                  seq(LORA_PAD),
                  pl.BlockSpec((None, 1, 3 * RW_WIDTH), lambda i, j: (i, 0, 0)),
                  pl.BlockSpec((None, 1, LORA_PAD), lambda i, j: (i, 0, 0)),
                  row(3 * RW_WIDTH), row(LORA_PAD), row(RW_WIDTH), row(RW_WIDTH),
                  mat(LORA_PAD, RW_WIDTH), mat(LORA_PAD, RW_WIDTH), mat(LORA_PAD, RW_WIDTH),
                  row(RW_WIDTH), row(RW_WIDTH), mat(SEG_CHUNK, SEG_CHUNK)],
        out_specs=[seq(RW_WIDTH)] * 7,
        out_shape=[jax.ShapeDtypeStruct((n, t, RW_WIDTH), F32)] * 7,
        scratch_shapes=[pltpu.VMEM((1, 3 * RW_WIDTH), F32), pltpu.VMEM((1, LORA_PAD), F32)],
        compiler_params=_cparams(("parallel", "arbitrary")),
        name="rwkv_prep",
    )(z1, zl, zprev, zlprev, prm["mu"], prm["mul"], prm["w0"], prm["a0"], prm["w2"], prm["a2"],
      prm["g2"], prm["k_k"], prm["k_a"], prm["bones"])
    return outs


def _rwkv_scan_kernel(r_ref, w_ref, k_ref, v_ref, kk_ref, b_ref, s0_ref, bones_ref, eye_ref,
                      y_ref, sout_ref, s_ref, *, tb, nblk, nseq):
    t = pl.program_id(1)

    @pl.when(t == 0)
    def _():
        s_ref[...] = s0_ref[...]

    bones = bones_ref[...]
    eye = eye_ref[...]
    seqs = range(nseq)

    def bc(ref, q, s):
        return jnp.concatenate(
            [jnp.broadcast_to(ref[q, pl.ds(s, 1), c * SEG_CHUNK:(c + 1) * SEG_CHUNK], (RW_HEAD, SEG_CHUNK))
             for c in range(4)], axis=0)

    def step(s, carry):
        st = [s_ref[q] for q in seqs]
        split = [_split(st[q] * bc(kk_ref, q, s)) for q in seqs]
        sa = [_dot(split[q][0], bones) + _dot(split[q][1], bones) for q in seqs]
        vcol = [_dot((bc(v_ref, q, s) * eye).astype(BF16), bones) for q in seqs]
        sn = [st[q] * jnp.exp(bc(w_ref, q, s)) - sa[q] * bc(b_ref, q, s) + vcol[q] * bc(k_ref, q, s) for q in seqs]
        for q in seqs:
            s_ref[q] = sn[q]
        yb = [_dot((sn[q] * bc(r_ref, q, s)).astype(BF16), bones) for q in seqs]
        for q in seqs:
            y4 = jnp.sum((yb[q] * eye).reshape(4, RW_HEAD, SEG_CHUNK), axis=1)
            for c in range(4):
                y_ref[q, pl.ds(s, 1), c * SEG_CHUNK:(c + 1) * SEG_CHUNK] = y4[c:c + 1, :]
        return carry

    lax.fori_loop(0, tb, step, 0)

    @pl.when(t == nblk - 1)
    def _():
        sout_ref[...] = s_ref[...]


def _rwkv_scan(r, w, k, v, kk, b, s0, prm, tb=8):
    n, t, _ = r.shape
    tb = _tile(t, tb)
    nblk = t // tb
    nseq = _tile(n, SCAN_SEQS_PER_STEP)
    seq = pl.BlockSpec((nseq, tb, RW_WIDTH), lambda i, j: (i, j, 0))
    st = pl.BlockSpec((nseq, 4 * RW_HEAD, SEG_CHUNK), lambda i, j: (i, 0, 0))
    cst = pl.BlockSpec((SEG_CHUNK, SEG_CHUNK), lambda i, j: (0, 0))
    return pl.pallas_call(
        functools.partial(_rwkv_scan_kernel, tb=tb, nblk=nblk, nseq=nseq),
        grid=(n // nseq, nblk),
        in_specs=[seq] * 6 + [st, cst, cst],
        out_specs=[seq, st],
        out_shape=[jax.ShapeDtypeStruct((n, t, RW_WIDTH), F32),
                   jax.ShapeDtypeStruct((n, 4 * RW_HEAD, SEG_CHUNK), F32)],
        scratch_shapes=[pltpu.VMEM((nseq, 4 * RW_HEAD, SEG_CHUNK), F32)],
        compiler_params=_cparams(("parallel", "arbitrary")),
        name="rwkv_scan",
    )(r, w, k, v, kk, b, s0, prm["bones"], prm["eye4"])


def _rwkv_chunk_kernel(r_ref, lw_ref, k_ref, v_ref, kk_ref, b_ref, y_ref, hout_ref, h_ref, *, nchunks):
    t = pl.program_id(1)
    cs = RW_CHUNK
    n = RW_GROUP * RW_HEAD
    ngroups = RW_HEADS // RW_GROUP

    @pl.when(t == 0)
    def _():
        h_ref[...] = jnp.zeros(h_ref.shape, F32)

    row = lax.broadcasted_iota(jnp.int32, (n, n), 0)
    lane = lax.broadcasted_iota(jnp.int32, (n, n), 1)
    same_head = (row // cs) == (lane // cs)
    strict = same_head & ((lane % cs) < (row % cs))
    incl = same_head & ((lane % cs) <= (row % cs))
    eye = row == lane
    lane_head = lax.broadcasted_iota(jnp.int32, (cs, n), 1) // RW_HEAD
    row_in = lax.broadcasted_iota(jnp.int32, (cs, n), 0)

    def stack(x):
        return jnp.concatenate([jnp.where(lane_head == hl, x, 0.0) for hl in range(RW_GROUP)], axis=0)

    def tile4(x):
        return jnp.concatenate([x] * RW_GROUP, axis=1)

    def cumsum_rows(x):
        for sh in (1, 2, 4, 8, 16, 32):
            x = x + jnp.where(row_in >= sh, pltpu.roll(x, sh, 0), 0.0)
        return x

    units = [(ci, c) for ci in range(CHUNKS_PER_STEP) for c in range(ngroups)]
    pre = {}
    for ci, c in units:
        sl = slice(c * n, (c + 1) * n)
        rs = slice(ci * cs, (ci + 1) * cs)
        lw = lw_ref[rs, sl]
        l_in = cumsum_rows(lw)
        l_end = l_in[cs - 1:cs, :]
        e_neg = jnp.exp(-l_in)
        e_tail = jnp.exp(l_end - l_in)
        kq = stack(kk_ref[rs, sl] * jnp.exp(l_in - lw)).astype(BF16)
        rq = stack(r_ref[rs, sl] * jnp.exp(l_in)).astype(BF16)
        k_c = k_ref[rs, sl]
        b_c = b_ref[rs, sl]
        kh = (k_c * e_neg).astype(BF16)
        bh = (b_c * e_neg).astype(BF16)
        tail = jnp.concatenate([stack(k_c * e_tail), -stack(b_c * e_tail)], axis=0)
        pre[ci, c] = dict(
            sl=sl, rs=rs, kq=kq, rq=rq,
            a_k=jnp.where(strict, tile4(_dot_nt(kq, kh)), 0.0).astype(BF16),
            a_b=jnp.where(strict, tile4(_dot_nt(kq, bh)), 0.0),
            r_k=jnp.where(incl, tile4(_dot_nt(rq, kh)), 0.0).astype(BF16),
            r_b=jnp.where(incl, tile4(_dot_nt(rq, bh)), 0.0).astype(BF16),
            v_bd=stack(v_ref[rs, sl]).astype(BF16),
            tail_t=tail.T.astype(BF16),
            decay_col=jnp.sum(jnp.where(eye, jnp.broadcast_to(jnp.exp(l_end), (n, n)), 0.0), axis=1,
                              keepdims=True))
    pw = {u_: pre[u_]["a_b"] for u_ in units}
    inv = {u_: jnp.where(eye, 1.0, 0.0) - pre[u_]["a_b"] for u_ in units}
    for _ in range(5):
        for u_ in units:
            pb = pw[u_].astype(BF16)
            pw[u_] = _dot(pb, pb)
        for u_ in units:
            inv[u_] = inv[u_] + _dot(inv[u_].astype(BF16), pw[u_].astype(BF16))
    av = {u_: _dot(pre[u_]["a_k"], pre[u_]["v_bd"]) for u_ in units}
    invb = {u_: inv[u_].astype(BF16) for u_ in units}
    yv = {u_: _dot(pre[u_]["r_k"], pre[u_]["v_bd"]) for u_ in units}
    groups = range(ngroups)
    for ci in range(CHUNKS_PER_STEP):
        h0 = {c: h_ref[c] for c in groups}
        h0b = {c: h0[c].astype(BF16) for c in groups}
        rhs = {c: (_dot(pre[ci, c]["kq"], h0b[c]) + av[ci, c]).astype(BF16) for c in groups}
        yh = {c: _dot(pre[ci, c]["rq"], h0b[c]) for c in groups}
        ub = {c: _dot(invb[ci, c], rhs[c]).astype(BF16) for c in groups}
        for c in groups:
            p_ = pre[ci, c]
            y = yh[c] + yv[ci, c] - _dot(p_["r_b"], ub[c])
            y_ref[p_["rs"], p_["sl"]] = sum(y[hl * cs:(hl + 1) * cs] for hl in range(1, RW_GROUP)) + y[0:cs]
        for c in groups:
            p_ = pre[ci, c]
            upd = _dot(p_["tail_t"], jnp.concatenate([p_["v_bd"], ub[c]], axis=0))
            h_ref[c] = p_["decay_col"] * h0[c] + upd

    @pl.when(t == nchunks - 1)
    def _():
        hout_ref[...] = h_ref[...]


def _rwkv_chunked(r, lw, k, v, kk, b):
    n, t, _ = r.shape
    rows = RW_CHUNK * CHUNKS_PER_STEP
    assert t % rows == 0 and RW_CHUNK == RW_HEAD
    nchunks = t // rows
    seq = pl.BlockSpec((None, rows, RW_WIDTH), lambda i, j: (i, j, 0))
    gw = RW_GROUP * RW_HEAD
    ngroups = RW_HEADS // RW_GROUP
    st = pl.BlockSpec((None, ngroups, gw, gw), lambda i, j: (i, 0, 0, 0))
    y, h = pl.pallas_call(
        functools.partial(_rwkv_chunk_kernel, nchunks=nchunks),
        grid=(n, nchunks),
        in_specs=[seq] * 6,
        out_specs=[seq, st],
        out_shape=[jax.ShapeDtypeStruct((n, t, RW_WIDTH), F32),
                   jax.ShapeDtypeStruct((n, ngroups, gw, gw), F32)],
        scratch_shapes=[pltpu.VMEM((ngroups, gw, gw), F32)],
        compiler_params=_cparams(("parallel", "arbitrary")),
        name="rwkv_chunked",
    )(r, lw, k, v, kk, b)
    h = h.reshape(n, ngroups, RW_GROUP, RW_HEAD, RW_GROUP, RW_HEAD)
    hd = jnp.stack([h[:, :, hl, :, hl, :] for hl in range(RW_GROUP)], axis=2)
    return y, jnp.swapaxes(hd, 3, 4).reshape(n, RW_HEADS, RW_HEAD, RW_HEAD)


def _rwkv_post_kernel(y_ref, r_ref, k_ref, v_ref, g_ref, rk_ref, lg_ref, lb_ref, bones_ref, o_ref):
    bones = bones_ref[...]
    y = y_ref[...]
    d = y - _seg_sum(y, bones) * (1.0 / RW_HEAD)
    var = _seg_sum(d * d, bones) * (1.0 / RW_HEAD)
    yn = d * lax.rsqrt(var + GN_EPS) * lg_ref[...] + lb_ref[...]
    bonus = _seg_sum(r_ref[...] * k_ref[...] * rk_ref[...], bones) * v_ref[...]
    o_ref[...] = ((yn + bonus) * g_ref[...]).astype(o_ref.dtype)


def _rwkv_post(y, r, k, v, g, prm, tm=256):
    m = y.shape[0]
    tm = _tile(m, tm)
    big = pl.BlockSpec((tm, RW_WIDTH), lambda i: (i, 0))
    row = pl.BlockSpec((1, RW_WIDTH), lambda i: (0, 0))
    return pl.pallas_call(
        _rwkv_post_kernel,
        grid=(m // tm,),
        in_specs=[big] * 5 + [row] * 3 + [pl.BlockSpec((SEG_CHUNK, SEG_CHUNK), lambda i: (0, 0))],
        out_specs=big,
        out_shape=jax.ShapeDtypeStruct((m, RW_WIDTH), BF16),
        compiler_params=_cparams(("parallel",)),
        name="rwkv_post",
    )(y, r, k, v, g, prm["r_k"], prm["ln_g"], prm["ln_b"], prm["bones"])


def _state_to_stacked(s):
    n = s.shape[0]
    s = s.reshape(n, 4, 4, RW_HEAD, RW_HEAD)
    return jnp.transpose(s, (0, 1, 3, 2, 4)).reshape(n, 4 * RW_HEAD, SEG_CHUNK)


def _state_from_stacked(s):
    n = s.shape[0]
    s = s.reshape(n, 4, RW_HEAD, 4, RW_HEAD)
    return jnp.transpose(s, (0, 1, 3, 2, 4)).reshape(n, RW_HEADS, RW_HEAD, RW_HEAD)


def _top3_rows(sc, idx, nvalid_mask):
    big = float(sc.shape[0])
    sc = jnp.where(nvalid_mask, sc, NEG)
    sel = jnp.zeros(sc.shape, F32)
    for _ in range(MOBA_TOPK):
        mx = jnp.max(sc, axis=0, keepdims=True)
        first = jnp.min(jnp.where(sc == mx, idx, big), axis=0, keepdims=True)
        hit = (idx == first) & (mx > 0.5 * NEG)
        sel = jnp.where(hit, 1.0, sel)
        sc = jnp.where(hit, NEG, sc)
    return sel


def _moba_prompt_kernel(q_ref, k_ref, v_ref, bown_ref, bprev_ref, bfar_ref, o_ref,
                        kb_ref, vt_ref, kmh_ref, kml_ref, sel_ref, m_ref, acc_ref, s_ref, *, nb):
    i = pl.program_id(1)
    blk = MOBA_BLOCK

    @pl.when(i == 0)
    def _():
        def prep(j, c):
            rows = pl.ds(pl.multiple_of(j * blk, blk), blk)
            kj = k_ref[rows, :]
            kb_ref[rows, :] = kj.astype(BF16)
            km = jnp.sum(kj, axis=0, keepdims=True) * (1.0 / blk)
            hi, lo = _split(km)
            kmh_ref[pl.ds(j, 1), :] = hi.astype(F32)
            kml_ref[pl.ds(j, 1), :] = lo.astype(F32)
            vt_ref[j] = v_ref[rows, :].T.astype(BF16)
            return c
        lax.fori_loop(0, nb, prep, 0)

    qt = q_ref[...].T
    qh, ql = _split(qt)
    kmh = kmh_ref[...].astype(BF16)
    kml = kml_ref[...].astype(BF16)
    sc = _dot(kmh, qh) + _dot(kmh, ql) + _dot(kml, qh)
    bidx = lax.broadcasted_iota(jnp.int32, sc.shape, 0)
    sel_ref[...] = (_top3_rows(sc, bidx.astype(F32), bidx < i) - 1.0) * (-NEG)

    qs = (qt * (HEAD_DIM ** -0.5 * LOG2E)).astype(BF16)

    def keys(j, n):
        return kb_ref[pl.ds(pl.multiple_of(j * blk, blk), n * blk), :]

    def values_t(j, n):
        return vt_ref[j] if n == 1 else jnp.concatenate([vt_ref[j + g] for g in range(n)], axis=1)

    bfar = bfar_ref[0:1, 0:1]
    kidx = lax.broadcasted_iota(jnp.int32, (blk, blk), 0)
    qidx = lax.broadcasted_iota(jnp.int32, (blk, blk), 1)
    causal = kidx <= qidx

    def mask_rows(j, n):
        rows = [jnp.broadcast_to(sel_ref[pl.ds(j + g, 1), :], (blk, blk)) for g in range(n)]
        return rows[0] if n == 1 else jnp.concatenate(rows, axis=0)

    def far_scores(j, n):
        return _dot(keys(j, n), qs) + mask_rows(j, n)

    def near(ref):
        return (ref[...] - bfar) * LOG2E

    n_far = jnp.maximum(i - 1, 0)
    n_quads = n_far // 4
    rem0 = n_quads * 4

    ip = jnp.maximum(i - 1, 0)
    jr = [jnp.minimum(rem0 + r, nb - 1) for r in range(3)]
    absent = [jnp.where(rem0 + r < n_far, 0.0, NEG) for r in range(3)]
    bias_a = jnp.concatenate([near(bprev_ref) + mask_rows(ip, 1), jnp.where(causal, near(bown_ref), NEG)], axis=0)
    scores = [_dot(jnp.concatenate([keys(ip, 1), keys(i, 1)], axis=0), qs) + bias_a]
    scores += [_dot(keys(jr[r], 1), qs) + (mask_rows(jr[r], 1) + absent[r]) for r in range(3)]
    s_ref[...] = far_scores(0, 4)
    vts = [jnp.concatenate([vt_ref[ip], vt_ref[i]], axis=1)] + [vt_ref[jr[r]] for r in range(3)]
    m_loc = [jnp.max(s, axis=0, keepdims=True) for s in scores]
    prob = [jnp.exp2(s - m) for s, m in zip(scores, m_loc)]
    l_loc = [jnp.broadcast_to(jnp.sum(p, axis=0, keepdims=True), (ONES_ROWS, blk)) for p in prob]
    pv = [_dot(vt, p.astype(BF16)) for vt, p in zip(vts, prob)]
    m_new = jnp.maximum(jnp.maximum(m_loc[0], m_loc[1]), jnp.maximum(m_loc[2], m_loc[3]))
    acc = None
    for m, o, l in zip(m_loc, pv, l_loc):
        term = jnp.exp2(m - m_new) * jnp.concatenate([o, l], axis=0)
        acc = term if acc is None else acc + term
    acc_ref[...] = acc
    m_ref[...] = m_new


    def far_quad(jq, c):
        s_next = far_scores(4 * jnp.minimum(jq + 1, n_quads - 1), 4)
        per = 4 // FAR_PARTS
        hks = range(FAR_PARTS)
        sp = [s_ref[hk * per * blk:(hk + 1) * per * blk, :] for hk in hks]
        m_loc = [jnp.max(s, axis=0, keepdims=True) for s in sp]
        prob = [jnp.exp2(s - m) for s, m in zip(sp, m_loc)]
        l_loc = [jnp.broadcast_to(jnp.sum(p, axis=0, keepdims=True), (ONES_ROWS, blk)) for p in prob]
        pv = [_dot(values_t(4 * jq + per * hk, per), prob[hk].astype(BF16)) for hk in hks]
        m_old = m_ref[...]
        m_new = m_old
        for m in m_loc:
            m_new = jnp.maximum(m_new, m)
        acc = jnp.exp2(m_old - m_new) * acc_ref[...]
        for m, o, l in zip(m_loc, pv, l_loc):
            acc = acc + jnp.exp2(m - m_new) * jnp.concatenate([o, l], axis=0)
        acc_ref[...] = acc
        m_ref[...] = m_new
        s_ref[...] = s_next
        return c
    lax.fori_loop(0, n_quads, far_quad, 0)

    acc = acc_ref[...]
    o_ref[...] = (acc[0:HEAD_DIM] / acc[HEAD_DIM:HEAD_DIM + 1]).T.astype(o_ref.dtype)


def _moba_prompt(q, k, v, bias):
    t = q.shape[0]
    assert t % MOBA_BLOCK == 0
    nb = t // MOBA_BLOCK
    assert nb >= 4
    blk = MOBA_BLOCK
    h8 = ATTN_HEADS
    tile = pl.BlockSpec((None, blk, blk), lambda h, i: (h, 0, 0))
    return pl.pallas_call(
        functools.partial(_moba_prompt_kernel, nb=nb),
        grid=(h8, nb),
        in_specs=[pl.BlockSpec((blk, HEAD_DIM), lambda h, i: (i, h)),
                  pl.BlockSpec((t, HEAD_DIM), lambda h, i: (0, h)),
                  pl.BlockSpec((t, HEAD_DIM), lambda h, i: (0, h)),
                  tile, tile,
                  pl.BlockSpec((None, 1, HEAD_DIM), lambda h, i: (h, 0, 0))],
        out_specs=pl.BlockSpec((blk, HEAD_DIM), lambda h, i: (i, h)),
        out_shape=jax.ShapeDtypeStruct((t, ATTN_WIDTH), BF16),
        scratch_shapes=[pltpu.VMEM((t, HEAD_DIM), BF16),
                        pltpu.VMEM((nb, HEAD_DIM, blk), BF16),
                        pltpu.VMEM((nb, HEAD_DIM), F32),
                        pltpu.VMEM((nb, HEAD_DIM), F32),
                        pltpu.VMEM((nb, blk), F32),
                        pltpu.VMEM((1, blk), F32),
                        pltpu.VMEM((HEAD_DIM + ONES_ROWS, blk), F32),
                        pltpu.VMEM((4 * blk, blk), F32)],
        compiler_params=_cparams(("arbitrary", "arbitrary")),
        name="moba_prompt",
    )(q, k, v, bias["own_t"], bias["prev_t"], bias["far"])


def _diag_extract(s):
    lane = lax.broadcasted_iota(jnp.int32, (ATTN_HEADS, s.shape[1]), 1) % ATTN_HEADS
    out = jnp.zeros((ATTN_HEADS, s.shape[1]), F32)
    for hp in range(ATTN_HEADS):
        out = out + jnp.where(lane == hp, s[hp * 8:(hp + 1) * 8, :], 0.0)
    return out


def _diag_expand(p):
    lane = lax.broadcasted_iota(jnp.int32, p.shape, 1) % ATTN_HEADS
    return jnp.concatenate([jnp.where(lane == hp, p, 0.0) for hp in range(ATTN_HEADS)], axis=0)


def _class_allreduce(x, op):
    for sh in (8, 16, 32, 64):
        x = op(x, pltpu.roll(x, sh, 1))
    return x


def _class_allreduce_many(xs, op):
    for sh in (8, 16, 32, 64):
        xs = [op(x, pltpu.roll(x, sh, 1)) for x in xs]
    return xs


def _fold_tiles(x, op):
    out = x[:, 0:128]
    for c in range(1, x.shape[1] // 128):
        out = op(out, x[:, c * 128:(c + 1) * 128])
    return out


def _moba_sample_kernel(pt_ref, q_ref, ck_ref, cv_ref, kn_ref, vn_ref, blast_ref, bown_ref, bfar_ref, o_ref,
                        sc_ref, bsum_ref, bmax_ref, bexp_ref, selx_ref, m_ref, li_ref, acc_ref,
                        page_ref, sem_ref, *, nblk, grp, nbatch):
    ph = pl.program_id(1)
    step = pl.program_id(2)
    nsteps = nblk // grp
    npages = 2 * grp
    ring = SAMPLE_LOOKAHEAD + 1
    total = nbatch * 2 * nsteps
    g_lin = (pl.program_id(0) * 2 + ph) * nsteps + step

    def page_copy(src_ref, page, slot):
        return pltpu.make_async_copy(src_ref.at[0, page], page_ref.at[slot], sem_ref.at[slot])

    def request(gl):
        b2 = gl // (2 * nsteps)
        r2 = gl % (2 * nsteps)
        ph2 = r2 // nsteps
        p0 = (r2 % nsteps) * npages
        base = (gl % ring) * npages
        for src_ref, which in ((ck_ref, 0), (cv_ref, 1)):
            @pl.when(ph2 == which)
            def _():
                for u in range(npages):
                    page_copy(src_ref, pt_ref[b2, p0 + u], base + u).start()

    @pl.when(g_lin == 0)
    def _():
        for gl in range(min(SAMPLE_LOOKAHEAD, total)):
            request(gl)

    @pl.when(g_lin + SAMPLE_LOOKAHEAD < total)
    def _():
        request(g_lin + SAMPLE_LOOKAHEAD)

    slot0 = (g_lin % ring) * npages
    for u in range(npages):
        page_copy(ck_ref, 0, slot0 + u).wait()
    k_refs = v_refs = [page_ref.at[slot0 + u] for u in range(npages)]
    ntok = 8
    rows = PAGE_SIZE * ATTN_HEADS
    nkeys = 2 * rows
    far16 = jnp.concatenate([bfar_ref[...]] * (nkeys // 128), axis=1)
    lane_blk = lax.broadcasted_iota(jnp.int32, (ntok, 128), 1) // 8

    def block_bias(j):
        return jnp.where(j == nblk - 1, blast_ref[...], far16)

    def tile16(x):
        return jnp.concatenate([x] * (nkeys // 128), axis=1)

    def compact_tile(j):
        return pl.ds(pl.multiple_of((j // 16) * 128, 128), 128)

    def put_compact(ref, j, x):
        ref[:, compact_tile(j)] = jnp.where(lane_blk == j % 16, x, ref[:, compact_tile(j)])

    @pl.when((ph == 0) & (step == 0))
    def _():
        bmax_ref[...] = jnp.zeros(bmax_ref.shape, F32)
        bexp_ref[...] = jnp.zeros(bexp_ref.shape, F32)

    @pl.when(ph == 0)
    def _():
        qs = (q_ref[...] * (HEAD_DIM ** -0.5)).astype(BF16)
        gs = range(grp)
        js = [step * grp + g for g in gs]
        raw = []
        for g in gs:
            k0 = k_refs[2 * g][...]
            k1 = k_refs[2 * g + 1][...]
            bsum_ref[pl.ds(pl.multiple_of(js[g] * 8, 8), 8), :] = jnp.sum(k0, axis=0) + jnp.sum(k1, axis=0)
            k2 = jnp.concatenate([k0.reshape(rows, HEAD_DIM), k1.reshape(rows, HEAD_DIM)], axis=0).astype(BF16)
            raw.append(_dot_nt(qs, k2))
        sc = [_diag_extract(raw[g]) + block_bias(js[g]) for g in gs]
        for g in gs:
            sc_ref[js[g]] = sc[g]
        bm = _class_allreduce_many([_fold_tiles(sc[g], jnp.maximum) for g in gs], jnp.maximum)
        be = _class_allreduce_many([_fold_tiles(jnp.exp(sc[g] - tile16(bm[g])), jnp.add) for g in gs], jnp.add)
        for g in gs:
            put_compact(bmax_ref, js[g], bm[g])
            put_compact(bexp_ref, js[g], be[g])

    @pl.when((ph == 0) & (step == nsteps - 1))
    def _():
        q = q_ref[...]
        qh, ql = _split(q)
        bh, bl = _split(bsum_ref[...] * (1.0 / MOBA_BLOCK))
        scx = _diag_extract(_dot_nt(qh, bh) + _dot_nt(qh, bl) + _dot_nt(ql, bh))
        width = nblk * 8
        jidx = (lax.broadcasted_iota(jnp.int32, (ntok, width), 1) // 8).astype(F32)

        def creduce(x, op):
            y = _class_allreduce(_fold_tiles(x, op), op)
            return jnp.concatenate([y] * (width // 128), axis=1)

        selx = jnp.zeros((ntok, width), F32)
        for _ in range(MOBA_TOPK):
            mx = creduce(scx, jnp.maximum)
            first = creduce(jnp.where(scx == mx, jidx, float(nblk)), jnp.minimum)
            hit = jidx == first
            selx = jnp.where(hit, 1.0, selx)
            scx = jnp.where(hit, NEG, scx)
        selx_ref[...] = selx
        picked = selx > 0.5

        qs = (q * (HEAD_DIM ** -0.5)).astype(BF16)
        so = _diag_extract(_dot_nt(qs, kn_ref[...].astype(BF16))) + bown_ref[...]
        tq = lax.broadcasted_iota(jnp.int32, so.shape, 0)
        tk = lax.broadcasted_iota(jnp.int32, so.shape, 1) // 8
        so = jnp.concatenate([jnp.where(tk <= tq, so, NEG), jnp.full((ntok, 64), NEG, F32)], axis=1)

        bmax = bmax_ref[...]
        m = jnp.maximum(_class_allreduce(so, jnp.maximum),
                        creduce(jnp.where(picked, bmax, NEG), jnp.maximum)[:, 0:128])
        mw = jnp.concatenate([m] * (width // 128), axis=1)
        l_blocks = creduce(jnp.where(picked, bexp_ref[...] * jnp.exp(bmax - mw), 0.0), jnp.add)[:, 0:128]
        po = jnp.exp(so - m)
        li = 1.0 / (_class_allreduce(po, jnp.add) + l_blocks)
        m_ref[...] = m
        li_ref[...] = li
        pfull = _diag_expand((po * li)[:, 0:64]).astype(BF16)
        acc_ref[...] = _dot(pfull, vn_ref[...].astype(BF16))

    @pl.when(ph == 1)
    def _():
        m16 = tile16(m_ref[...])
        li16 = tile16(li_ref[...])
        gs = range(grp)
        js = [step * grp + g for g in gs]
        mine = [jnp.where(lane_blk == js[g] % 16, selx_ref[:, compact_tile(js[g])], 0.0) for g in gs]
        picked = _class_allreduce_many(mine, jnp.add)
        p = [jnp.where(tile16(picked[g]) > 0.5, jnp.exp(sc_ref[js[g]] - m16) * li16, 0.0) for g in gs]
        pv = []
        for g in gs:
            v2 = jnp.concatenate([v_refs[2 * g][...].reshape(rows, HEAD_DIM),
                                  v_refs[2 * g + 1][...].reshape(rows, HEAD_DIM)], axis=0).astype(BF16)
            pv.append(_dot(_diag_expand(p[g]).astype(BF16), v2))
        acc = acc_ref[...]
        for g in gs:
            acc = acc + pv[g]
        acc_ref[...] = acc

    @pl.when((ph == 1) & (step == nsteps - 1))
    def _():
        o_ref[...] = acc_ref[...]


def _moba_sample(qht, knew, vnew, cache_k, cache_v, page_table, bias):
    nb_, n_pages = page_table.shape
    assert n_pages % PAGES_PER_BLOCK == 0 and PAGES_PER_BLOCK == 2
    nblk = n_pages // 2
    assert nblk >= MOBA_TOPK and (nblk * 8) % 128 == 0
    nkeys = 2 * PAGE_SIZE * ATTN_HEADS
    grp = SAMPLE_BLOCKS_PER_STEP
    assert nblk % grp == 0
    nsteps = nblk // grp

    per_b = pl.BlockSpec((None, 64, HEAD_DIM), lambda b, p, j, pt: (b, 0, 0))
    cst = lambda r, c: pl.BlockSpec((r, c), lambda b, p, j, pt: (0, 0))
    hbm = pl.BlockSpec(memory_space=pl.ANY)
    nslots = (SAMPLE_LOOKAHEAD + 1) * 2 * grp
    grid_spec = pltpu.PrefetchScalarGridSpec(
        num_scalar_prefetch=1,
        grid=(nb_, 2, nsteps),
        in_specs=[per_b, hbm, hbm, per_b, per_b, cst(8, nkeys), cst(8, 64), cst(1, 128)],
        out_specs=per_b,
        scratch_shapes=[pltpu.VMEM((nblk, 8, nkeys), F32),
                        pltpu.VMEM((nblk * 8, HEAD_DIM), F32),
                        pltpu.VMEM((8, nblk * 8), F32),
                        pltpu.VMEM((8, nblk * 8), F32),
                        pltpu.VMEM((8, nblk * 8), F32),
                        pltpu.VMEM((8, 128), F32),
                        pltpu.VMEM((8, 128), F32),
                        pltpu.VMEM((64, HEAD_DIM), F32),
                        pltpu.VMEM((nslots, PAGE_SIZE, ATTN_HEADS, HEAD_DIM), F32),
                        pltpu.SemaphoreType.DMA((nslots,))])
    return pl.pallas_call(
        functools.partial(_moba_sample_kernel, nblk=nblk, grp=grp, nbatch=nb_),
        grid_spec=grid_spec,
        out_shape=jax.ShapeDtypeStruct((nb_, 64, HEAD_DIM), F32),
        compiler_params=_cparams(("arbitrary", "arbitrary", "arbitrary")),
        name="moba_sample",
    )(page_table, qht, cache_k, cache_v, knew, vnew, bias["last_s"], bias["own_s"], bias["far_s"])


def _bias_of_distance(dist, rel_bias):
    dist = jnp.maximum(dist, 0)
    exact = N_BUCKETS // 2
    log_ratio = jnp.log(jnp.maximum(dist, 1).astype(F32) / exact) / math.log(MAX_DISTANCE / exact)
    large = jnp.minimum(exact + (log_ratio * (N_BUCKETS - exact)).astype(jnp.int32), N_BUCKETS - 1)
    return rel_bias[jnp.where(dist < exact, dist, large)]


def _bias_tiles(rel_bias, past_len, dec_seq):
    blk = MOBA_BLOCK
    val = _bias_of_distance(jnp.arange(2 * blk), rel_bias)
    val_t = val.T

    def toeplitz(ext):
        n2 = ext.shape[1]
        n = n2 // 2
        return jnp.tile(ext, (1, n))[:, :n * (n2 - 1)].reshape(ext.shape[0], n, n2 - 1)[:, :, :n]

    own_t = toeplitz(val_t)
    prev_t = toeplitz(jnp.roll(val_t, -blk, axis=1))
    far_row = val[2 * blk - 1]
    far = jnp.broadcast_to(far_row[:, None, None], (ATTN_HEADS, 1, HEAD_DIM))
    assert past_len % blk == 0 and 2 * blk - 1 >= MAX_DISTANCE and dec_seq < blk
    last = jnp.stack([val[t + 1:t + 1 + blk][::-1] for t in range(dec_seq)])
    last_s = last.reshape(dec_seq, blk * ATTN_HEADS)
    own = jnp.stack([jnp.concatenate([val[:t + 1][::-1], jnp.zeros((dec_seq - 1 - t, ATTN_HEADS), F32)])
                     for t in range(dec_seq)])
    own_s = own.reshape(dec_seq, dec_seq * ATTN_HEADS)
    far_s = jnp.tile(far_row, 128 // ATTN_HEADS)[None, :]
    return dict(own_t=own_t, prev_t=prev_t, far=far, last_s=last_s, own_s=own_s, far_s=far_s)


def _np_consts():
    seg = np.arange(SEG_CHUNK) // RW_HEAD
    bones = (seg[:, None] == seg[None, :]).astype(np.float32)
    idx = np.arange(SEG_CHUNK) % RW_HEAD
    eye4 = (idx[:, None] == idx[None, :]).astype(np.float32)
    return bones, eye4


def _pad_rows(w, row0, total):
    return jnp.zeros((total, w.shape[1]), BF16).at[row0:row0 + w.shape[0]].set(w.astype(BF16))


def kernel(x_prompt, x_sample, cache_k, cache_v, page_table, state_wkv, state_shift, w_in, rel_bias, rw_mu, rw_w0, rw_w2, rw_a0, rw_a2, rw_g2, rw_k_k, rw_k_a, rw_r_k, rw_ln_g, rw_ln_b, w_up_attn, w_up_rwkv, w_o, ln1_g, ln1_b, w_ffn_gate, w_ffn_up, w_ffn_down, ln2_g, ln2_b):
    assert x_prompt.shape[0] == 1 and w_in.shape[0] == DEPTH == 1
    t_p = x_prompt.shape[1]
    nb_s, t_s, _ = x_sample.shape
    assert t_s == 8
    past_len = page_table.shape[1] * PAGE_SIZE

    w_main = w_in[0].astype(BF16)
    w_lora = jnp.pad(w_main[:, LORA_COL:GATE_COL], ((0, 0), (0, LORA_PAD - LORA_COLS)))
    w_gate = w_main[:, GATE_COL:]
    wb_up_attn, wb_up_rwkv, wb_o = w_up_attn[0].astype(BF16), w_up_rwkv[0].astype(BF16), w_o[0].astype(BF16)
    wb_gate, wb_up, wb_down = w_ffn_gate[0].astype(BF16), w_ffn_up[0].astype(BF16), w_ffn_down[0].astype(BF16)
    bones_np, eye_np = _np_consts()
    mu = rw_mu[0]
    prm = dict(
        mu=mu[None, :3 * RW_WIDTH],
        mul=jnp.pad(mu[3 * RW_WIDTH:], (0, LORA_PAD - LORA_COLS))[None, :],
        w0=rw_w0, a0=rw_a0,
        w2=_pad_rows(rw_w2[0], 0, LORA_PAD),
        a2=_pad_rows(rw_a2[0], DECAY_LORA, LORA_PAD),
        g2=_pad_rows(rw_g2[0], DECAY_LORA + AAA_LORA, LORA_PAD),
        k_k=rw_k_k, k_a=rw_k_a, r_k=rw_r_k.reshape(1, RW_WIDTH), ln_g=rw_ln_g, ln_b=rw_ln_b,
        bones=jnp.asarray(bones_np, BF16), eye4=jnp.asarray(eye_np, F32))
    bias = _bias_tiles(rel_bias, past_len, t_s)

    def group(x2d, nseq, tseq, zprev, zlprev, s0, attend, tt, tb):
        zq, new_k, new_v, zrkv = _proj_split(x2d, w_main, (ATTN_WIDTH, ATTN_WIDTH, ATTN_WIDTH, 3 * RW_WIDTH))
        zl = _proj(x2d, w_lora, 0, LORA_PAD)
        zg = _proj(x2d, w_gate, 0, 2 * D_MODEL)
        o_attn = attend(zq, new_k, new_v)
        r, w, k, v, kk, b, g = _rwkv_prep(zrkv.reshape(nseq, tseq, 3 * RW_WIDTH),
                                          zl.reshape(nseq, tseq, LORA_PAD), zprev, zlprev, prm, tt)
        if s0 is None:
            y, s_fin = _rwkv_chunked(r, w, k, v, kk, b)
        else:
            y, s_fin = _rwkv_scan(r, w, k, v, kk, b, s0, prm, tb)
            s_fin = _state_from_stacked(s_fin)
        flat = lambda u: u.reshape(nseq * tseq, RW_WIDTH)
        o_rwkv = _rwkv_post(flat(y), flat(r), flat(k), flat(v), flat(g), prm)
        mixed = _merge(o_attn, o_rwkv, zg, wb_up_attn, wb_up_rwkv)
        h = _proj_ln(mixed, wb_o, x2d, ln1_g, ln1_b, tm=1024)
        act = _ffn_up(h, wb_gate, wb_up)
        out = _proj_ln(act, wb_down, h, ln2_g, ln2_b, tm=1024)
        return out, new_k, new_v, s_fin

    xp = x_prompt[0]
    yp, kp, vp, sp = group(
        xp, 1, t_p,
        jnp.zeros((1, 1, 3 * RW_WIDTH), F32), jnp.zeros((1, 1, LORA_PAD), F32),
        None, lambda zq, zk, zv: _moba_prompt(zq, zk, zv, bias), 256, 8)

    xs = x_sample.reshape(nb_s * t_s, D_MODEL)
    sh = state_shift[0]
    zprev = _proj(sh, w_main, RKV_COL, 3 * RW_WIDTH)[:, None, :]
    zlprev = _proj(sh, w_lora, 0, LORA_PAD)[:, None, :]

    def attend_sample(zq, zk, zv):
        q = zq.reshape(nb_s, t_s, ATTN_HEADS, HEAD_DIM)
        qht = jnp.transpose(q, (0, 2, 1, 3)).reshape(nb_s, ATTN_HEADS * t_s, HEAD_DIM)
        kn = zk.reshape(nb_s, t_s * ATTN_HEADS, HEAD_DIM)
        vn = zv.reshape(nb_s, t_s * ATTN_HEADS, HEAD_DIM)
        o = _moba_sample(qht, kn, vn, cache_k, cache_v, page_table, bias)
        o = jnp.transpose(o.reshape(nb_s, ATTN_HEADS, t_s, HEAD_DIM), (0, 2, 1, 3))
        return o.reshape(nb_s * t_s, ATTN_WIDTH).astype(BF16)

    ys, ks, vs, ss = group(xs, nb_s, t_s, zprev, zlprev, _state_to_stacked(state_wkv[0]),
                           attend_sample, 8, 8)

    return (yp[None], ys.reshape(nb_s, t_s, D_MODEL),
            kp.reshape(1, 1, t_p, ATTN_HEADS, HEAD_DIM), vp.reshape(1, 1, t_p, ATTN_HEADS, HEAD_DIM),
            sp[None], xp[None, -1:, :],
            ks.reshape(1, nb_s, t_s, ATTN_HEADS, HEAD_DIM), vs.reshape(1, nb_s, t_s, ATTN_HEADS, HEAD_DIM),
            ss[None], x_sample[None, :, -1, :])
```

```python
import functools
import math

import numpy as np
import jax
import jax.numpy as jnp
from jax import lax
from jax.experimental import pallas as pl
from jax.experimental.pallas import tpu as pltpu

F32 = jnp.float32
BF16 = jnp.bfloat16

D_MODEL = 2048
HEAD_DIM = 128
ATTN_HEADS = D_MODEL // 256
ATTN_WIDTH = ATTN_HEADS * HEAD_DIM
MOBA_BLOCK = 256
MOBA_TOPK = 3
N_BUCKETS = 32
MAX_DISTANCE = 128
PAGE_SIZE = 128
RW_HEAD = 64
RW_HEADS = D_MODEL // 128
RW_WIDTH = RW_HEADS * RW_HEAD
DECAY_LORA = 96
AAA_LORA = 96
GATE_LORA = 256
LORA_COLS = DECAY_LORA + AAA_LORA + GATE_LORA
LORA_PAD = 512
GN_EPS = RW_HEAD * 1e-5
D_FF = 5632
LN_EPS = 1e-5
SMALL_M = 256
WIDE_TILE_BYTES = 8 * 1024 * 1024
LN_ROWS = 128
DEPTH = 1
DEEPNORM_ALPHA = (2 * DEPTH) ** 0.25

RKV_COL = 3 * ATTN_WIDTH
LORA_COL = RKV_COL + 3 * RW_WIDTH
GATE_COL = LORA_COL + LORA_COLS
MAIN_COLS = LORA_COL

NEG = -1e30
LOG2E = math.log2(math.e)
FAR_PARTS = 4
ONES_ROWS = 8
SEG_CHUNK = 256
PAGES_PER_BLOCK = MOBA_BLOCK // PAGE_SIZE
SAMPLE_BLOCKS_PER_STEP = 8
SAMPLE_LOOKAHEAD = 2
RW_CHUNK = 64
SCAN_SEQS_PER_STEP = 4
RW_GROUP = 2
CHUNKS_PER_STEP = 2
VMEM_LIMIT = 56 * 1024 * 1024


def _cparams(sem):
    return pltpu.CompilerParams(dimension_semantics=sem, vmem_limit_bytes=VMEM_LIMIT)


def _dot(a, b):
    return jnp.dot(a, b, preferred_element_type=F32)


def _dot_nt(a, b):
    return lax.dot_general(a, b, (((1,), (1,)), ((), ())), preferred_element_type=F32)


def _split(x):
    hi = x.astype(BF16)
    lo = (x - hi.astype(F32)).astype(BF16)
    return hi, lo


def _sigmoid(x):
    return 1.0 / (1.0 + jnp.exp(-x))


def _tile(m, pref):
    t = min(m, pref)
    assert m % t == 0, (m, pref)
    return t


def _wide_tn(m, tn, n, k=D_MODEL):
    if m > SMALL_M:
        return tn
    for cand in (2048, 1024, 512):
        if cand >= tn and n % cand == 0 and k * cand * 2 <= WIDE_TILE_BYTES:
            return cand
    return tn


def _proj_kernel(x_ref, w_ref, o_ref, xb_ref):
    @pl.when(pl.program_id(1) == 0)
    def _():
        xb_ref[...] = x_ref[...].astype(BF16)

    o_ref[...] = _dot(xb_ref[...], w_ref[...].astype(BF16)).astype(o_ref.dtype)


def _proj(x, w, col0, ncols, tm=1024, tn=512):
    m, k = x.shape
    tm = _tile(m, tm)
    tn = _tile(ncols, _wide_tn(m, tn, ncols))
    assert col0 % tn == 0
    c0 = col0 // tn
    return pl.pallas_call(
        _proj_kernel,
        grid=(m // tm, ncols // tn),
        in_specs=[pl.BlockSpec((tm, k), lambda i, j: (i, 0)),
                  pl.BlockSpec((k, tn), lambda i, j: (0, c0 + j))],
        out_specs=pl.BlockSpec((tm, tn), lambda i, j: (i, j)),
        out_shape=jax.ShapeDtypeStruct((m, ncols), F32),
        scratch_shapes=[pltpu.VMEM((tm, k), BF16)],
        compiler_params=_cparams(("parallel", "arbitrary")),
        name="proj",
    )(x, w)


def _proj_split_kernel(x_ref, w_ref, *refs, bounds):
    o_refs, xb_ref = refs[:-1], refs[-1]
    j = pl.program_id(1)

    @pl.when(j == 0)
    def _():
        xb_ref[...] = x_ref[...].astype(BF16)

    acc = _dot(xb_ref[...], w_ref[...].astype(BF16))
    for o_ref, (lo, hi) in zip(o_refs, bounds):
        @pl.when((j >= lo) & (j < hi))
        def _():
            o_ref[...] = acc


def _proj_split(x, w, widths, tm=1024, tn=512):
    m, k = x.shape
    tm = _tile(m, tm)
    tn = _wide_tn(m, tn, math.gcd(*widths))
    assert all(wd % tn == 0 for wd in widths)
    edges = np.cumsum([0] + [wd // tn for wd in widths])
    bounds = [(int(edges[g]), int(edges[g + 1])) for g in range(len(widths))]

    def out_map(lo, hi):
        return lambda i, j: (i, jnp.clip(j - lo, 0, hi - lo - 1))

    return pl.pallas_call(
        functools.partial(_proj_split_kernel, bounds=bounds),
        grid=(m // tm, int(edges[-1])),
        in_specs=[pl.BlockSpec((tm, k), lambda i, j: (i, 0)),
                  pl.BlockSpec((k, tn), lambda i, j: (0, j))],
        out_specs=[pl.BlockSpec((tm, tn), out_map(lo, hi)) for lo, hi in bounds],
        out_shape=[jax.ShapeDtypeStruct((m, wd), F32) for wd in widths],
        scratch_shapes=[pltpu.VMEM((tm, k), BF16)],
        compiler_params=_cparams(("parallel", "arbitrary")),
        name="proj_split",
    )(x, w)


def _merge_kernel(oa_ref, or_ref, wa_ref, wr_ref, ga_ref, gr_ref, o_ref):
    a = _dot(oa_ref[...], wa_ref[...].astype(BF16))
    r = _dot(or_ref[...], wr_ref[...].astype(BF16))
    o_ref[...] = (_sigmoid(ga_ref[...]) * a + _sigmoid(gr_ref[...]) * r).astype(o_ref.dtype)


def _merge(o_attn, o_rwkv, zg, w_up_attn, w_up_rwkv, tm=1024, tn=512):
    m = o_attn.shape[0]
    tm = _tile(m, tm)
    tn = _wide_tn(m, tn, D_MODEL)
    nj = D_MODEL // tn
    return pl.pallas_call(
        _merge_kernel,
        grid=(m // tm, nj),
        in_specs=[pl.BlockSpec((tm, ATTN_WIDTH), lambda i, j: (i, 0)),
                  pl.BlockSpec((tm, RW_WIDTH), lambda i, j: (i, 0)),
                  pl.BlockSpec((ATTN_WIDTH, tn), lambda i, j: (0, j)),
                  pl.BlockSpec((RW_WIDTH, tn), lambda i, j: (0, j)),
                  pl.BlockSpec((tm, tn), lambda i, j: (i, j)),
                  pl.BlockSpec((tm, tn), lambda i, j: (i, j + nj))],
        out_specs=pl.BlockSpec((tm, tn), lambda i, j: (i, j)),
        out_shape=jax.ShapeDtypeStruct((m, D_MODEL), BF16),
        compiler_params=_cparams(("parallel", "arbitrary")),
        name="merge",
    )(o_attn, o_rwkv, w_up_attn, w_up_rwkv, zg, zg)


def _proj_ln_kernel(m_ref, w_ref, x_ref, g_ref, b_ref, o_ref, *, tn, nj):
    j = pl.program_id(1)
    col = pl.multiple_of(j * tn, tn)
    mb = m_ref[...].astype(BF16)
    o_ref[:, pl.ds(col, tn)] = DEEPNORM_ALPHA * x_ref[...] + _dot(mb, w_ref[...].astype(BF16))

    @pl.when(j == nj - 1)
    def _():
        ln_rows = min(LN_ROWS, o_ref.shape[0])
        assert o_ref.shape[0] % ln_rows == 0

        def norm_rows(c, carry):
            rows = pl.ds(pl.multiple_of(c * ln_rows, ln_rows), ln_rows)
            y = o_ref[rows, :]
            mu = jnp.mean(y, axis=-1, keepdims=True)
            d = y - mu
            var = jnp.mean(d * d, axis=-1, keepdims=True)
            o_ref[rows, :] = d * lax.rsqrt(var + LN_EPS) * g_ref[...] + b_ref[...]
            return carry
        lax.fori_loop(0, o_ref.shape[0] // ln_rows, norm_rows, 0)


def _proj_ln(mat, w, x, g, b, tm=512, tn=256):
    m, k = mat.shape
    tm = _tile(m, tm)
    tn = _wide_tn(m, tn, D_MODEL, k)
    nj = D_MODEL // tn
    return pl.pallas_call(
        functools.partial(_proj_ln_kernel, tn=tn, nj=nj),
        grid=(m // tm, nj),
        in_specs=[pl.BlockSpec((tm, k), lambda i, j: (i, 0)),
                  pl.BlockSpec((k, tn), lambda i, j: (0, j)),
                  pl.BlockSpec((tm, tn), lambda i, j: (i, j)),
                  pl.BlockSpec((1, D_MODEL), lambda i, j: (0, 0)),
                  pl.BlockSpec((1, D_MODEL), lambda i, j: (0, 0))],
        out_specs=pl.BlockSpec((tm, D_MODEL), lambda i, j: (i, 0)),
        out_shape=jax.ShapeDtypeStruct((m, D_MODEL), F32),
        compiler_params=_cparams(("parallel", "arbitrary")),
        name="proj_ln",
    )(mat, w, x, g, b)


def _ffn_up_kernel(h_ref, wg_ref, wu_ref, o_ref, hb_ref):
    @pl.when(pl.program_id(1) == 0)
    def _():
        hb_ref[...] = h_ref[...].astype(BF16)

    hb = hb_ref[...]
    a = _dot(hb, wg_ref[...].astype(BF16))
    u = _dot(hb, wu_ref[...].astype(BF16))
    o_ref[...] = (a * _sigmoid(a) * u).astype(o_ref.dtype)


def _ffn_up(h, wg, wu, tm=1024, tn=256):
    m = h.shape[0]
    tm = _tile(m, tm)
    tn = _wide_tn(m, tn, D_FF)
    return pl.pallas_call(
        _ffn_up_kernel,
        grid=(m // tm, D_FF // tn),
        in_specs=[pl.BlockSpec((tm, D_MODEL), lambda i, j: (i, 0)),
                  pl.BlockSpec((D_MODEL, tn), lambda i, j: (0, j)),
                  pl.BlockSpec((D_MODEL, tn), lambda i, j: (0, j))],
        out_specs=pl.BlockSpec((tm, tn), lambda i, j: (i, j)),
        out_shape=jax.ShapeDtypeStruct((m, D_FF), BF16),
        scratch_shapes=[pltpu.VMEM((tm, D_MODEL), BF16)],
        compiler_params=_cparams(("parallel", "arbitrary")),
        name="ffn_up",
    )(h, wg, wu)


def _seg_sum(x, bones):
    outs = []
    for c in range(x.shape[1] // SEG_CHUNK):
        hi, lo = _split(x[:, c * SEG_CHUNK:(c + 1) * SEG_CHUNK])
        outs.append(_dot(hi, bones) + _dot(lo, bones))
    return jnp.concatenate(outs, axis=1)


def _shifted(z, prev):
    zs = pltpu.roll(z, 1, 0)
    row = lax.broadcasted_iota(jnp.int32, z.shape, 0)
    return jnp.where(row == 0, prev, zs)


def _rwkv_prep_kernel(z_ref, zl_ref, zp_ref, zlp_ref, mu_ref, mul_ref, w0_ref, a0_ref, w2_ref, a2_ref,
                      g2_ref, kkw_ref, kaw_ref, bones_ref,
                      r_o, w_o, k_o, v_o, kk_o, b_o, g_o, prev_ref, prevl_ref):
    @pl.when(pl.program_id(1) == 0)
    def _():
        prev_ref[...] = zp_ref[...]
        prevl_ref[...] = zlp_ref[...]

    z = z_ref[...]
    zl = zl_ref[...]
    tt = z.shape[0]
    zm = z + (_shifted(z, prev_ref[...]) - z) * mu_ref[...]
    zlm = zl + (_shifted(zl, prevl_ref[...]) - zl) * mul_ref[...]
    prev_ref[...] = z[tt - 1:tt, :]
    prevl_ref[...] = zl[tt - 1:tt, :]

    r = zm[:, 0:RW_WIDTH]
    k = zm[:, RW_WIDTH:2 * RW_WIDTH]
    v = zm[:, 2 * RW_WIDTH:3 * RW_WIDTH]
    xw = w0_ref[...] + _dot(jnp.tanh(zlm).astype(BF16), w2_ref[...])
    nx = -xw
    softplus = jnp.maximum(nx, 0.0) + jnp.log(1.0 + jnp.exp(-jnp.abs(nx)))
    log_decay = -jnp.exp(-softplus - 0.5)
    a = _sigmoid(a0_ref[...] + _dot(zlm.astype(BF16), a2_ref[...]))
    g = _dot(_sigmoid(zlm).astype(BF16), g2_ref[...])
    kk = k * kkw_ref[...]
    ssq = _seg_sum(kk * kk, bones_ref[...])
    kk = kk / jnp.maximum(jnp.sqrt(ssq), 1e-12)
    r_o[...] = r
    w_o[...] = log_decay
    k_o[...] = k * (1.0 + (a - 1.0) * kaw_ref[...])
    v_o[...] = v
    kk_o[...] = kk
    b_o[...] = kk * a
    g_o[...] = g


def _rwkv_prep(z1, zl, zprev, zlprev, prm, tt):
    n, t, _ = z1.shape
    tt = _tile(t, tt)
    row = lambda c: pl.BlockSpec((1, c), lambda i, j: (0, 0))
    mat = lambda r, c: pl.BlockSpec((r, c), lambda i, j: (0, 0))
    seq = lambda c: pl.BlockSpec((None, tt, c), lambda i, j: (i, j, 0))
    outs = pl.pallas_call(
        _rwkv_prep_kernel,
        grid=(n, t // tt),
        in_specs=[seq(3 * RW_WIDTH),
                  seq(LORA_PAD),
                  pl.BlockSpec((None, 1, 3 * RW_WIDTH), lambda i, j: (i, 0, 0)),
                  pl.BlockSpec((None, 1, LORA_PAD), lambda i, j: (i, 0, 0)),
                  row(3 * RW_WIDTH), row(LORA_PAD), row(RW_WIDTH), row(RW_WIDTH),
                  mat(LORA_PAD, RW_WIDTH), mat(LORA_PAD, RW_WIDTH), mat(LORA_PAD, RW_WIDTH),
                  row(RW_WIDTH), row(RW_WIDTH), mat(SEG_CHUNK, SEG_CHUNK)],
        out_specs=[seq(RW_WIDTH)] * 7,
        out_shape=[jax.ShapeDtypeStruct((n, t, RW_WIDTH), F32)] * 7,
        scratch_shapes=[pltpu.VMEM((1, 3 * RW_WIDTH), F32), pltpu.VMEM((1, LORA_PAD), F32)],
        compiler_params=_cparams(("parallel", "arbitrary")),
        name="rwkv_prep",
    )(z1, zl, zprev, zlprev, prm["mu"], prm["mul"], prm["w0"], prm["a0"], prm["w2"], prm["a2"],
      prm["g2"], prm["k_k"], prm["k_a"], prm["bones"])
    return outs


def _rwkv_scan_kernel(r_ref, w_ref, k_ref, v_ref, kk_ref, b_ref, s0_ref, bones_ref, eye_ref,
                      y_ref, sout_ref, s_ref, *, tb, nblk, nseq):
    t = pl.program_id(1)

    @pl.when(t == 0)
    def _():
        s_ref[...] = s0_ref[...]

    bones = bones_ref[...]
    eye = eye_ref[...]
    seqs = range(nseq)

    def bc(ref, q, s):
        return jnp.concatenate(
            [jnp.broadcast_to(ref[q, pl.ds(s, 1), c * SEG_CHUNK:(c + 1) * SEG_CHUNK], (RW_HEAD, SEG_CHUNK))
             for c in range(4)], axis=0)

    def step(s, carry):
        st = [s_ref[q] for q in seqs]
        split = [_split(st[q] * bc(kk_ref, q, s)) for q in seqs]
        sa = [_dot(split[q][0], bones) + _dot(split[q][1], bones) for q in seqs]
        vcol = [_dot((bc(v_ref, q, s) * eye).astype(BF16), bones) for q in seqs]
        sn = [st[q] * jnp.exp(bc(w_ref, q, s)) - sa[q] * bc(b_ref, q, s) + vcol[q] * bc(k_ref, q, s) for q in seqs]
        for q in seqs:
            s_ref[q] = sn[q]
        yb = [_dot((sn[q] * bc(r_ref, q, s)).astype(BF16), bones) for q in seqs]
        for q in seqs:
            y4 = jnp.sum((yb[q] * eye).reshape(4, RW_HEAD, SEG_CHUNK), axis=1)
            for c in range(4):
                y_ref[q, pl.ds(s, 1), c * SEG_CHUNK:(c + 1) * SEG_CHUNK] = y4[c:c + 1, :]
        return carry

    lax.fori_loop(0, tb, step, 0)

    @pl.when(t == nblk - 1)
    def _():
        sout_ref[...] = s_ref[...]


def _rwkv_scan(r, w, k, v, kk, b, s0, prm, tb=8):
    n, t, _ = r.shape
    tb = _tile(t, tb)
    nblk = t // tb
    nseq = _tile(n, SCAN_SEQS_PER_STEP)
    seq = pl.BlockSpec((nseq, tb, RW_WIDTH), lambda i, j: (i, j, 0))
    st = pl.BlockSpec((nseq, 4 * RW_HEAD, SEG_CHUNK), lambda i, j: (i, 0, 0))
    cst = pl.BlockSpec((SEG_CHUNK, SEG_CHUNK), lambda i, j: (0, 0))
    return pl.pallas_call(
        functools.partial(_rwkv_scan_kernel, tb=tb, nblk=nblk, nseq=nseq),
        grid=(n // nseq, nblk),
        in_specs=[seq] * 6 + [st, cst, cst],
        out_specs=[seq, st],
        out_shape=[jax.ShapeDtypeStruct((n, t, RW_WIDTH), F32),
                   jax.ShapeDtypeStruct((n, 4 * RW_HEAD, SEG_CHUNK), F32)],
        scratch_shapes=[pltpu.VMEM((nseq, 4 * RW_HEAD, SEG_CHUNK), F32)],
        compiler_params=_cparams(("parallel", "arbitrary")),
        name="rwkv_scan",
    )(r, w, k, v, kk, b, s0, prm["bones"], prm["eye4"])


def _rwkv_chunk_kernel(r_ref, lw_ref, k_ref, v_ref, kk_ref, b_ref, y_ref, hout_ref, h_ref, *, nchunks):
    t = pl.program_id(1)
    cs = RW_CHUNK
    n = RW_GROUP * RW_HEAD
    ngroups = RW_HEADS // RW_GROUP

    @pl.when(t == 0)
    def _():
        h_ref[...] = jnp.zeros(h_ref.shape, F32)

    row = lax.broadcasted_iota(jnp.int32, (n, n), 0)
    lane = lax.broadcasted_iota(jnp.int32, (n, n), 1)
    same_head = (row // cs) == (lane // cs)
    strict = same_head & ((lane % cs) < (row % cs))
    incl = same_head & ((lane % cs) <= (row % cs))
    eye = row == lane
    lane_head = lax.broadcasted_iota(jnp.int32, (cs, n), 1) // RW_HEAD
    row_in = lax.broadcasted_iota(jnp.int32, (cs, n), 0)

    def stack(x):
        return jnp.concatenate([jnp.where(lane_head == hl, x, 0.0) for hl in range(RW_GROUP)], axis=0)

    def tile4(x):
        return jnp.concatenate([x] * RW_GROUP, axis=1)

    def cumsum_rows(x):
        for sh in (1, 2, 4, 8, 16, 32):
            x = x + jnp.where(row_in >= sh, pltpu.roll(x, sh, 0), 0.0)
        return x

    units = [(ci, c) for ci in range(CHUNKS_PER_STEP) for c in range(ngroups)]
    pre = {}
    for ci, c in units:
        sl = slice(c * n, (c + 1) * n)
        rs = slice(ci * cs, (ci + 1) * cs)
        lw = lw_ref[rs, sl]
        l_in = cumsum_rows(lw)
        l_end = l_in[cs - 1:cs, :]
        e_neg = jnp.exp(-l_in)
        e_tail = jnp.exp(l_end - l_in)
        kq = stack(kk_ref[rs, sl] * jnp.exp(l_in - lw)).astype(BF16)
        rq = stack(r_ref[rs, sl] * jnp.exp(l_in)).astype(BF16)
        k_c = k_ref[rs, sl]
        b_c = b_ref[rs, sl]
        kh = (k_c * e_neg).astype(BF16)
        bh = (b_c * e_neg).astype(BF16)
        tail = jnp.concatenate([stack(k_c * e_tail), -stack(b_c * e_tail)], axis=0)
        pre[ci, c] = dict(
            sl=sl, rs=rs, kq=kq, rq=rq,
            a_k=jnp.where(strict, tile4(_dot_nt(kq, kh)), 0.0).astype(BF16),
            a_b=jnp.where(strict, tile4(_dot_nt(kq, bh)), 0.0),
            r_k=jnp.where(incl, tile4(_dot_nt(rq, kh)), 0.0).astype(BF16),
            r_b=jnp.where(incl, tile4(_dot_nt(rq, bh)), 0.0).astype(BF16),
            v_bd=stack(v_ref[rs, sl]).astype(BF16),
            tail_t=tail.T.astype(BF16),
            decay_col=jnp.sum(jnp.where(eye, jnp.broadcast_to(jnp.exp(l_end), (n, n)), 0.0), axis=1,
                              keepdims=True))
    pw = {u_: pre[u_]["a_b"] for u_ in units}
    inv = {u_: jnp.where(eye, 1.0, 0.0) - pre[u_]["a_b"] for u_ in units}
    for _ in range(5):
        for u_ in units:
            pb = pw[u_].astype(BF16)
            pw[u_] = _dot(pb, pb)
        for u_ in units:
            inv[u_] = inv[u_] + _dot(inv[u_].astype(BF16), pw[u_].astype(BF16))
    av = {u_: _dot(pre[u_]["a_k"], pre[u_]["v_bd"]) for u_ in units}
    invb = {u_: inv[u_].astype(BF16) for u_ in units}
    yv = {u_: _dot(pre[u_]["r_k"], pre[u_]["v_bd"]) for u_ in units}
    groups = range(ngroups)
    for ci in range(CHUNKS_PER_STEP):
        h0 = {c: h_ref[c] for c in groups}
        h0b = {c: h0[c].astype(BF16) for c in groups}
        rhs = {c: (_dot(pre[ci, c]["kq"], h0b[c]) + av[ci, c]).astype(BF16) for c in groups}
        yh = {c: _dot(pre[ci, c]["rq"], h0b[c]) for c in groups}
        ub = {c: _dot(invb[ci, c], rhs[c]).astype(BF16) for c in groups}
        for c in groups:
            p_ = pre[ci, c]
            y = yh[c] + yv[ci, c] - _dot(p_["r_b"], ub[c])
            y_ref[p_["rs"], p_["sl"]] = sum(y[hl * cs:(hl + 1) * cs] for hl in range(1, RW_GROUP)) + y[0:cs]
        for c in groups:
            p_ = pre[ci, c]
            upd = _dot(p_["tail_t"], jnp.concatenate([p_["v_bd"], ub[c]], axis=0))
            h_ref[c] = p_["decay_col"] * h0[c] + upd

    @pl.when(t == nchunks - 1)
    def _():
        hout_ref[...] = h_ref[...]


def _rwkv_chunked(r, lw, k, v, kk, b):
    n, t, _ = r.shape
    rows = RW_CHUNK * CHUNKS_PER_STEP
    assert t % rows == 0 and RW_CHUNK == RW_HEAD
    nchunks = t // rows
    seq = pl.BlockSpec((None, rows, RW_WIDTH), lambda i, j: (i, j, 0))
    gw = RW_GROUP * RW_HEAD
    ngroups = RW_HEADS // RW_GROUP
    st = pl.BlockSpec((None, ngroups, gw, gw), lambda i, j: (i, 0, 0, 0))
    y, h = pl.pallas_call(
        functools.partial(_rwkv_chunk_kernel, nchunks=nchunks),
        grid=(n, nchunks),
        in_specs=[seq] * 6,
        out_specs=[seq, st],
        out_shape=[jax.ShapeDtypeStruct((n, t, RW_WIDTH), F32),
                   jax.ShapeDtypeStruct((n, ngroups, gw, gw), F32)],
        scratch_shapes=[pltpu.VMEM((ngroups, gw, gw), F32)],
        compiler_params=_cparams(("parallel", "arbitrary")),
        name="rwkv_chunked",
    )(r, lw, k, v, kk, b)
    h = h.reshape(n, ngroups, RW_GROUP, RW_HEAD, RW_GROUP, RW_HEAD)
    hd = jnp.stack([h[:, :, hl, :, hl, :] for hl in range(RW_GROUP)], axis=2)
    return y, jnp.swapaxes(hd, 3, 4).reshape(n, RW_HEADS, RW_HEAD, RW_HEAD)


def _rwkv_post_kernel(y_ref, r_ref, k_ref, v_ref, g_ref, rk_ref, lg_ref, lb_ref, bones_ref, o_ref):
    bones = bones_ref[...]
    y = y_ref[...]
    d = y - _seg_sum(y, bones) * (1.0 / RW_HEAD)
    var = _seg_sum(d * d, bones) * (1.0 / RW_HEAD)
    yn = d * lax.rsqrt(var + GN_EPS) * lg_ref[...] + lb_ref[...]
    bonus = _seg_sum(r_ref[...] * k_ref[...] * rk_ref[...], bones) * v_ref[...]
    o_ref[...] = ((yn + bonus) * g_ref[...]).astype(o_ref.dtype)


def _rwkv_post(y, r, k, v, g, prm, tm=256):
    m = y.shape[0]
    tm = _tile(m, tm)
    big = pl.BlockSpec((tm, RW_WIDTH), lambda i: (i, 0))
    row = pl.BlockSpec((1, RW_WIDTH), lambda i: (0, 0))
    return pl.pallas_call(
        _rwkv_post_kernel,
        grid=(m // tm,),
        in_specs=[big] * 5 + [row] * 3 + [pl.BlockSpec((SEG_CHUNK, SEG_CHUNK), lambda i: (0, 0))],
        out_specs=big,
        out_shape=jax.ShapeDtypeStruct((m, RW_WIDTH), BF16),
        compiler_params=_cparams(("parallel",)),
        name="rwkv_post",
    )(y, r, k, v, g, prm["r_k"], prm["ln_g"], prm["ln_b"], prm["bones"])


def _state_to_stacked(s):
    n = s.shape[0]
    s = s.reshape(n, 4, 4, RW_HEAD, RW_HEAD)
    return jnp.transpose(s, (0, 1, 3, 2, 4)).reshape(n, 4 * RW_HEAD, SEG_CHUNK)


def _state_from_stacked(s):
    n = s.shape[0]
    s = s.reshape(n, 4, RW_HEAD, 4, RW_HEAD)
    return jnp.transpose(s, (0, 1, 3, 2, 4)).reshape(n, RW_HEADS, RW_HEAD, RW_HEAD)


def _top3_rows(sc, idx, nvalid_mask):
    big = float(sc.shape[0])
    sc = jnp.where(nvalid_mask, sc, NEG)
    sel = jnp.zeros(sc.shape, F32)
    for _ in range(MOBA_TOPK):
        mx = jnp.max(sc, axis=0, keepdims=True)
        first = jnp.min(jnp.where(sc == mx, idx, big), axis=0, keepdims=True)
        hit = (idx == first) & (mx > 0.5 * NEG)
        sel = jnp.where(hit, 1.0, sel)
        sc = jnp.where(hit, NEG, sc)
    return sel


def _moba_prompt_kernel(q_ref, k_ref, v_ref, bown_ref, bprev_ref, bfar_ref, o_ref,
                        kb_ref, vt_ref, kmh_ref, kml_ref, sel_ref, m_ref, acc_ref, s_ref, *, nb):
    i = pl.program_id(1)
    blk = MOBA_BLOCK

    @pl.when(i == 0)
    def _():
        def prep(j, c):
            rows = pl.ds(pl.multiple_of(j * blk, blk), blk)
            kj = k_ref[rows, :]
            kb_ref[rows, :] = kj.astype(BF16)
            km = jnp.sum(kj, axis=0, keepdims=True) * (1.0 / blk)
            hi, lo = _split(km)
            kmh_ref[pl.ds(j, 1), :] = hi.astype(F32)
            kml_ref[pl.ds(j, 1), :] = lo.astype(F32)
            vt_ref[j] = v_ref[rows, :].T.astype(BF16)
            return c
        lax.fori_loop(0, nb, prep, 0)

    qt = q_ref[...].T
    qh, ql = _split(qt)
    kmh = kmh_ref[...].astype(BF16)
    kml = kml_ref[...].astype(BF16)
    sc = _dot(kmh, qh) + _dot(kmh, ql) + _dot(kml, qh)
    bidx = lax.broadcasted_iota(jnp.int32, sc.shape, 0)
    sel_ref[...] = (_top3_rows(sc, bidx.astype(F32), bidx < i) - 1.0) * (-NEG)

    qs = (qt * (HEAD_DIM ** -0.5 * LOG2E)).astype(BF16)

    def keys(j, n):
        return kb_ref[pl.ds(pl.multiple_of(j * blk, blk), n * blk), :]

    def values_t(j, n):
        return vt_ref[j] if n == 1 else jnp.concatenate([vt_ref[j + g] for g in range(n)], axis=1)

    bfar = bfar_ref[0:1, 0:1]
    kidx = lax.broadcasted_iota(jnp.int32, (blk, blk), 0)
    qidx = lax.broadcasted_iota(jnp.int32, (blk, blk), 1)
    causal = kidx <= qidx

    def mask_rows(j, n):
        rows = [jnp.broadcast_to(sel_ref[pl.ds(j + g, 1), :], (blk, blk)) for g in range(n)]
        return rows[0] if n == 1 else jnp.concatenate(rows, axis=0)

    def far_scores(j, n):
        return _dot(keys(j, n), qs) + mask_rows(j, n)

    def near(ref):
        return (ref[...] - bfar) * LOG2E

    n_far = jnp.maximum(i - 1, 0)
    n_quads = n_far // 4
    rem0 = n_quads * 4

    ip = jnp.maximum(i - 1, 0)
    jr = [jnp.minimum(rem0 + r, nb - 1) for r in range(3)]
    absent = [jnp.where(rem0 + r < n_far, 0.0, NEG) for r in range(3)]
    bias_a = jnp.concatenate([near(bprev_ref) + mask_rows(ip, 1), jnp.where(causal, near(bown_ref), NEG)], axis=0)
    scores = [_dot(jnp.concatenate([keys(ip, 1), keys(i, 1)], axis=0), qs) + bias_a]
    scores += [_dot(keys(jr[r], 1), qs) + (mask_rows(jr[r], 1) + absent[r]) for r in range(3)]
    s_ref[...] = far_scores(0, 4)
    vts = [jnp.concatenate([vt_ref[ip], vt_ref[i]], axis=1)] + [vt_ref[jr[r]] for r in range(3)]
    m_loc = [jnp.max(s, axis=0, keepdims=True) for s in scores]
    prob = [jnp.exp2(s - m) for s, m in zip(scores, m_loc)]
    l_loc = [jnp.broadcast_to(jnp.sum(p, axis=0, keepdims=True), (ONES_ROWS, blk)) for p in prob]
    pv = [_dot(vt, p.astype(BF16)) for vt, p in zip(vts, prob)]
    m_new = jnp.maximum(jnp.maximum(m_loc[0], m_loc[1]), jnp.maximum(m_loc[2], m_loc[3]))
    acc = None
    for m, o, l in zip(m_loc, pv, l_loc):
        term = jnp.exp2(m - m_new) * jnp.concatenate([o, l], axis=0)
        acc = term if acc is None else acc + term
    acc_ref[...] = acc
    m_ref[...] = m_new


    def far_quad(jq, c):
        s_next = far_scores(4 * jnp.minimum(jq + 1, n_quads - 1), 4)
        per = 4 // FAR_PARTS
        hks = range(FAR_PARTS)
        sp = [s_ref[hk * per * blk:(hk + 1) * per * blk, :] for hk in hks]
        m_loc = [jnp.max(s, axis=0, keepdims=True) for s in sp]
        prob = [jnp.exp2(s - m) for s, m in zip(sp, m_loc)]
        l_loc = [jnp.broadcast_to(jnp.sum(p, axis=0, keepdims=True), (ONES_ROWS, blk)) for p in prob]
        pv = [_dot(values_t(4 * jq + per * hk, per), prob[hk].astype(BF16)) for hk in hks]
        m_old = m_ref[...]
        m_new = m_old
        for m in m_loc:
            m_new = jnp.maximum(m_new, m)
        acc = jnp.exp2(m_old - m_new) * acc_ref[...]
        for m, o, l in zip(m_loc, pv, l_loc):
            acc = acc + jnp.exp2(m - m_new) * jnp.concatenate([o, l], axis=0)
        acc_ref[...] = acc
        m_ref[...] = m_new
        s_ref[...] = s_next
        return c
    lax.fori_loop(0, n_quads, far_quad, 0)

    acc = acc_ref[...]
    o_ref[...] = (acc[0:HEAD_DIM] / acc[HEAD_DIM:HEAD_DIM + 1]).T.astype(o_ref.dtype)


def _moba_prompt(q, k, v, bias):
    t = q.shape[0]
    assert t % MOBA_BLOCK == 0
    nb = t // MOBA_BLOCK
    assert nb >= 4
    blk = MOBA_BLOCK
    h8 = ATTN_HEADS
    tile = pl.BlockSpec((None, blk, blk), lambda h, i: (h, 0, 0))
    return pl.pallas_call(
        functools.partial(_moba_prompt_kernel, nb=nb),
        grid=(h8, nb),
        in_specs=[pl.BlockSpec((blk, HEAD_DIM), lambda h, i: (i, h)),
                  pl.BlockSpec((t, HEAD_DIM), lambda h, i: (0, h)),
                  pl.BlockSpec((t, HEAD_DIM), lambda h, i: (0, h)),
                  tile, tile,
                  pl.BlockSpec((None, 1, HEAD_DIM), lambda h, i: (h, 0, 0))],
        out_specs=pl.BlockSpec((blk, HEAD_DIM), lambda h, i: (i, h)),
        out_shape=jax.ShapeDtypeStruct((t, ATTN_WIDTH), BF16),
        scratch_shapes=[pltpu.VMEM((t, HEAD_DIM), BF16),
                        pltpu.VMEM((nb, HEAD_DIM, blk), BF16),
                        pltpu.VMEM((nb, HEAD_DIM), F32),
                        pltpu.VMEM((nb, HEAD_DIM), F32),
                        pltpu.VMEM((nb, blk), F32),
                        pltpu.VMEM((1, blk), F32),
                        pltpu.VMEM((HEAD_DIM + ONES_ROWS, blk), F32),
                        pltpu.VMEM((4 * blk, blk), F32)],
        compiler_params=_cparams(("arbitrary", "arbitrary")),
        name="moba_prompt",
    )(q, k, v, bias["own_t"], bias["prev_t"], bias["far"])


def _diag_extract(s):
    lane = lax.broadcasted_iota(jnp.int32, (ATTN_HEADS, s.shape[1]), 1) % ATTN_HEADS
    out = jnp.zeros((ATTN_HEADS, s.shape[1]), F32)
    for hp in range(ATTN_HEADS):
        out = out + jnp.where(lane == hp, s[hp * 8:(hp + 1) * 8, :], 0.0)
    return out


def _diag_expand(p):
    lane = lax.broadcasted_iota(jnp.int32, p.shape, 1) % ATTN_HEADS
    return jnp.concatenate([jnp.where(lane == hp, p, 0.0) for hp in range(ATTN_HEADS)], axis=0)


def _class_allreduce(x, op):
    for sh in (8, 16, 32, 64):
        x = op(x, pltpu.roll(x, sh, 1))
    return x


def _class_allreduce_many(xs, op):
    for sh in (8, 16, 32, 64):
        xs = [op(x, pltpu.roll(x, sh, 1)) for x in xs]
    return xs


def _fold_tiles(x, op):
    out = x[:, 0:128]
    for c in range(1, x.shape[1] // 128):
        out = op(out, x[:, c * 128:(c + 1) * 128])
    return out


def _moba_sample_kernel(pt_ref, q_ref, ck_ref, cv_ref, kn_ref, vn_ref, blast_ref, bown_ref, bfar_ref, o_ref,
                        sc_ref, bsum_ref, bmax_ref, bexp_ref, selx_ref, m_ref, li_ref, acc_ref,
                        page_ref, sem_ref, *, nblk, grp, nbatch):
    ph = pl.program_id(1)
    step = pl.program_id(2)
    nsteps = nblk // grp
    npages = 2 * grp
    ring = SAMPLE_LOOKAHEAD + 1
    total = nbatch * 2 * nsteps
    g_lin = (pl.program_id(0) * 2 + ph) * nsteps + step

    def page_copy(src_ref, page, slot):
        return pltpu.make_async_copy(src_ref.at[0, page], page_ref.at[slot], sem_ref.at[slot])

    def request(gl):
        b2 = gl // (2 * nsteps)
        r2 = gl % (2 * nsteps)
        ph2 = r2 // nsteps
        p0 = (r2 % nsteps) * npages
        base = (gl % ring) * npages
        for src_ref, which in ((ck_ref, 0), (cv_ref, 1)):
            @pl.when(ph2 == which)
            def _():
                for u in range(npages):
                    page_copy(src_ref, pt_ref[b2, p0 + u], base + u).start()

    @pl.when(g_lin == 0)
    def _():
        for gl in range(min(SAMPLE_LOOKAHEAD, total)):
            request(gl)

    @pl.when(g_lin + SAMPLE_LOOKAHEAD < total)
    def _():
        request(g_lin + SAMPLE_LOOKAHEAD)

    slot0 = (g_lin % ring) * npages
    for u in range(npages):
        page_copy(ck_ref, 0, slot0 + u).wait()
    k_refs = v_refs = [page_ref.at[slot0 + u] for u in range(npages)]
    ntok = 8
    rows = PAGE_SIZE * ATTN_HEADS
    nkeys = 2 * rows
    far16 = jnp.concatenate([bfar_ref[...]] * (nkeys // 128), axis=1)
    lane_blk = lax.broadcasted_iota(jnp.int32, (ntok, 128), 1) // 8

    def block_bias(j):
        return jnp.where(j == nblk - 1, blast_ref[...], far16)

    def tile16(x):
        return jnp.concatenate([x] * (nkeys // 128), axis=1)

    def compact_tile(j):
        return pl.ds(pl.multiple_of((j // 16) * 128, 128), 128)

    def put_compact(ref, j, x):
        ref[:, compact_tile(j)] = jnp.where(lane_blk == j % 16, x, ref[:, compact_tile(j)])

    @pl.when((ph == 0) & (step == 0))
    def _():
        bmax_ref[...] = jnp.zeros(bmax_ref.shape, F32)
        bexp_ref[...] = jnp.zeros(bexp_ref.shape, F32)

    @pl.when(ph == 0)
    def _():
        qs = (q_ref[...] * (HEAD_DIM ** -0.5)).astype(BF16)
        gs = range(grp)
        js = [step * grp + g for g in gs]
        raw = []
        for g in gs:
            k0 = k_refs[2 * g][...]
            k1 = k_refs[2 * g + 1][...]
            bsum_ref[pl.ds(pl.multiple_of(js[g] * 8, 8), 8), :] = jnp.sum(k0, axis=0) + jnp.sum(k1, axis=0)
            k2 = jnp.concatenate([k0.reshape(rows, HEAD_DIM), k1.reshape(rows, HEAD_DIM)], axis=0).astype(BF16)
            raw.append(_dot_nt(qs, k2))
        sc = [_diag_extract(raw[g]) + block_bias(js[g]) for g in gs]
        for g in gs:
            sc_ref[js[g]] = sc[g]
        bm = _class_allreduce_many([_fold_tiles(sc[g], jnp.maximum) for g in gs], jnp.maximum)
        be = _class_allreduce_many([_fold_tiles(jnp.exp(sc[g] - tile16(bm[g])), jnp.add) for g in gs], jnp.add)
        for g in gs:
            put_compact(bmax_ref, js[g], bm[g])
            put_compact(bexp_ref, js[g], be[g])

    @pl.when((ph == 0) & (step == nsteps - 1))
    def _():
        q = q_ref[...]
        qh, ql = _split(q)
        bh, bl = _split(bsum_ref[...] * (1.0 / MOBA_BLOCK))
        scx = _diag_extract(_dot_nt(qh, bh) + _dot_nt(qh, bl) + _dot_nt(ql, bh))
        width = nblk * 8
        jidx = (lax.broadcasted_iota(jnp.int32, (ntok, width), 1) // 8).astype(F32)

        def creduce(x, op):
            y = _class_allreduce(_fold_tiles(x, op), op)
            return jnp.concatenate([y] * (width // 128), axis=1)

        selx = jnp.zeros((ntok, width), F32)
        for _ in range(MOBA_TOPK):
            mx = creduce(scx, jnp.maximum)
            first = creduce(jnp.where(scx == mx, jidx, float(nblk)), jnp.minimum)
            hit = jidx == first
            selx = jnp.where(hit, 1.0, selx)
            scx = jnp.where(hit, NEG, scx)
        selx_ref[...] = selx
        picked = selx > 0.5

        qs = (q * (HEAD_DIM ** -0.5)).astype(BF16)
        so = _diag_extract(_dot_nt(qs, kn_ref[...].astype(BF16))) + bown_ref[...]
        tq = lax.broadcasted_iota(jnp.int32, so.shape, 0)
        tk = lax.broadcasted_iota(jnp.int32, so.shape, 1) // 8
        so = jnp.concatenate([jnp.where(tk <= tq, so, NEG), jnp.full((ntok, 64), NEG, F32)], axis=1)

        bmax = bmax_ref[...]
        m = jnp.maximum(_class_allreduce(so, jnp.maximum),
                        creduce(jnp.where(picked, bmax, NEG), jnp.maximum)[:, 0:128])
        mw = jnp.concatenate([m] * (width // 128), axis=1)
        l_blocks = creduce(jnp.where(picked, bexp_ref[...] * jnp.exp(bmax - mw), 0.0), jnp.add)[:, 0:128]
        po = jnp.exp(so - m)
        li = 1.0 / (_class_allreduce(po, jnp.add) + l_blocks)
        m_ref[...] = m
        li_ref[...] = li
        pfull = _diag_expand((po * li)[:, 0:64]).astype(BF16)
        acc_ref[...] = _dot(pfull, vn_ref[...].astype(BF16))

    @pl.when(ph == 1)
    def _():
        m16 = tile16(m_ref[...])
        li16 = tile16(li_ref[...])
        gs = range(grp)
        js = [step * grp + g for g in gs]
        mine = [jnp.where(lane_blk == js[g] % 16, selx_ref[:, compact_tile(js[g])], 0.0) for g in gs]
        picked = _class_allreduce_many(mine, jnp.add)
        p = [jnp.where(tile16(picked[g]) > 0.5, jnp.exp(sc_ref[js[g]] - m16) * li16, 0.0) for g in gs]
        pv = []
        for g in gs:
            v2 = jnp.concatenate([v_refs[2 * g][...].reshape(rows, HEAD_DIM),
                                  v_refs[2 * g + 1][...].reshape(rows, HEAD_DIM)], axis=0).astype(BF16)
            pv.append(_dot(_diag_expand(p[g]).astype(BF16), v2))
        acc = acc_ref[...]
        for g in gs:
            acc = acc + pv[g]
        acc_ref[...] = acc

    @pl.when((ph == 1) & (step == nsteps - 1))
    def _():
        o_ref[...] = acc_ref[...]


def _moba_sample(qht, knew, vnew, cache_k, cache_v, page_table, bias):
    nb_, n_pages = page_table.shape
    assert n_pages % PAGES_PER_BLOCK == 0 and PAGES_PER_BLOCK == 2
    nblk = n_pages // 2
    assert nblk >= MOBA_TOPK and (nblk * 8) % 128 == 0
    nkeys = 2 * PAGE_SIZE * ATTN_HEADS
    grp = SAMPLE_BLOCKS_PER_STEP
    assert nblk % grp == 0
    nsteps = nblk // grp

    per_b = pl.BlockSpec((None, 64, HEAD_DIM), lambda b, p, j, pt: (b, 0, 0))
    cst = lambda r, c: pl.BlockSpec((r, c), lambda b, p, j, pt: (0, 0))
    hbm = pl.BlockSpec(memory_space=pl.ANY)
    nslots = (SAMPLE_LOOKAHEAD + 1) * 2 * grp
    grid_spec = pltpu.PrefetchScalarGridSpec(
        num_scalar_prefetch=1,
        grid=(nb_, 2, nsteps),
        in_specs=[per_b, hbm, hbm, per_b, per_b, cst(8, nkeys), cst(8, 64), cst(1, 128)],
        out_specs=per_b,
        scratch_shapes=[pltpu.VMEM((nblk, 8, nkeys), F32),
                        pltpu.VMEM((nblk * 8, HEAD_DIM), F32),
                        pltpu.VMEM((8, nblk * 8), F32),
                        pltpu.VMEM((8, nblk * 8), F32),
                        pltpu.VMEM((8, nblk * 8), F32),
                        pltpu.VMEM((8, 128), F32),
                        pltpu.VMEM((8, 128), F32),
                        pltpu.VMEM((64, HEAD_DIM), F32),
                        pltpu.VMEM((nslots, PAGE_SIZE, ATTN_HEADS, HEAD_DIM), F32),
                        pltpu.SemaphoreType.DMA((nslots,))])
    return pl.pallas_call(
        functools.partial(_moba_sample_kernel, nblk=nblk, grp=grp, nbatch=nb_),
        grid_spec=grid_spec,
        out_shape=jax.ShapeDtypeStruct((nb_, 64, HEAD_DIM), F32),
        compiler_params=_cparams(("arbitrary", "arbitrary", "arbitrary")),
        name="moba_sample",
    )(page_table, qht, cache_k, cache_v, knew, vnew, bias["last_s"], bias["own_s"], bias["far_s"])


def _bias_of_distance(dist, rel_bias):
    dist = jnp.maximum(dist, 0)
    exact = N_BUCKETS // 2
    log_ratio = jnp.log(jnp.maximum(dist, 1).astype(F32) / exact) / math.log(MAX_DISTANCE / exact)
    large = jnp.minimum(exact + (log_ratio * (N_BUCKETS - exact)).astype(jnp.int32), N_BUCKETS - 1)
    return rel_bias[jnp.where(dist < exact, dist, large)]


def _bias_tiles(rel_bias, past_len, dec_seq):
    blk = MOBA_BLOCK
    val = _bias_of_distance(jnp.arange(2 * blk), rel_bias)
    val_t = val.T

    def toeplitz(ext):
        n2 = ext.shape[1]
        n = n2 // 2
        return jnp.tile(ext, (1, n))[:, :n * (n2 - 1)].reshape(ext.shape[0], n, n2 - 1)[:, :, :n]

    own_t = toeplitz(val_t)
    prev_t = toeplitz(jnp.roll(val_t, -blk, axis=1))
    far_row = val[2 * blk - 1]
    far = jnp.broadcast_to(far_row[:, None, None], (ATTN_HEADS, 1, HEAD_DIM))
    assert past_len % blk == 0 and 2 * blk - 1 >= MAX_DISTANCE and dec_seq < blk
    last = jnp.stack([val[t + 1:t + 1 + blk][::-1] for t in range(dec_seq)])
    last_s = last.reshape(dec_seq, blk * ATTN_HEADS)
    own = jnp.stack([jnp.concatenate([val[:t + 1][::-1], jnp.zeros((dec_seq - 1 - t, ATTN_HEADS), F32)])
                     for t in range(dec_seq)])
    own_s = own.reshape(dec_seq, dec_seq * ATTN_HEADS)
    far_s = jnp.tile(far_row, 128 // ATTN_HEADS)[None, :]
    return dict(own_t=own_t, prev_t=prev_t, far=far, last_s=last_s, own_s=own_s, far_s=far_s)


def _np_consts():
    seg = np.arange(SEG_CHUNK) // RW_HEAD
    bones = (seg[:, None] == seg[None, :]).astype(np.float32)
    idx = np.arange(SEG_CHUNK) % RW_HEAD
    eye4 = (idx[:, None] == idx[None, :]).astype(np.float32)
    return bones, eye4


def _pad_rows(w, row0, total):
    return jnp.zeros((total, w.shape[1]), BF16).at[row0:row0 + w.shape[0]].set(w.astype(BF16))


def kernel(x_prompt, x_sample, cache_k, cache_v, page_table, state_wkv, state_shift, w_in, rel_bias, rw_mu, rw_w0, rw_w2, rw_a0, rw_a2, rw_g2, rw_k_k, rw_k_a, rw_r_k, rw_ln_g, rw_ln_b, w_up_attn, w_up_rwkv, w_o, ln1_g, ln1_b, w_ffn_gate, w_ffn_up, w_ffn_down, ln2_g, ln2_b):
    assert x_prompt.shape[0] == 1 and w_in.shape[0] == DEPTH == 1
    t_p = x_prompt.shape[1]
    nb_s, t_s, _ = x_sample.shape
    assert t_s == 8
    past_len = page_table.shape[1] * PAGE_SIZE

    w_main = w_in[0].astype(BF16)
    w_lora = jnp.pad(w_main[:, LORA_COL:GATE_COL], ((0, 0), (0, LORA_PAD - LORA_COLS)))
    w_gate = w_main[:, GATE_COL:]
    wb_up_attn, wb_up_rwkv, wb_o = w_up_attn[0].astype(BF16), w_up_rwkv[0].astype(BF16), w_o[0].astype(BF16)
    wb_gate, wb_up, wb_down = w_ffn_gate[0].astype(BF16), w_ffn_up[0].astype(BF16), w_ffn_down[0].astype(BF16)
    bones_np, eye_np = _np_consts()
    mu = rw_mu[0]
    prm = dict(
        mu=mu[None, :3 * RW_WIDTH],
        mul=jnp.pad(mu[3 * RW_WIDTH:], (0, LORA_PAD - LORA_COLS))[None, :],
        w0=rw_w0, a0=rw_a0,
        w2=_pad_rows(rw_w2[0], 0, LORA_PAD),
        a2=_pad_rows(rw_a2[0], DECAY_LORA, LORA_PAD),
        g2=_pad_rows(rw_g2[0], DECAY_LORA + AAA_LORA, LORA_PAD),
        k_k=rw_k_k, k_a=rw_k_a, r_k=rw_r_k.reshape(1, RW_WIDTH), ln_g=rw_ln_g, ln_b=rw_ln_b,
        bones=jnp.asarray(bones_np, BF16), eye4=jnp.asarray(eye_np, F32))
    bias = _bias_tiles(rel_bias, past_len, t_s)

    def group(x2d, nseq, tseq, zprev, zlprev, s0, attend, tt, tb):
        zq, new_k, new_v, zrkv = _proj_split(x2d, w_main, (ATTN_WIDTH, ATTN_WIDTH, ATTN_WIDTH, 3 * RW_WIDTH))
        zl = _proj(x2d, w_lora, 0, LORA_PAD)
        zg = _proj(x2d, w_gate, 0, 2 * D_MODEL)
        o_attn = attend(zq, new_k, new_v)
        r, w, k, v, kk, b, g = _rwkv_prep(zrkv.reshape(nseq, tseq, 3 * RW_WIDTH),
                                          zl.reshape(nseq, tseq, LORA_PAD), zprev, zlprev, prm, tt)
        if s0 is None:
            y, s_fin = _rwkv_chunked(r, w, k, v, kk, b)
        else:
            y, s_fin = _rwkv_scan(r, w, k, v, kk, b, s0, prm, tb)
            s_fin = _state_from_stacked(s_fin)
        flat = lambda u: u.reshape(nseq * tseq, RW_WIDTH)
        o_rwkv = _rwkv_post(flat(y), flat(r), flat(k), flat(v), flat(g), prm)
        mixed = _merge(o_attn, o_rwkv, zg, wb_up_attn, wb_up_rwkv, tn=1024)
        h = _proj_ln(mixed, wb_o, x2d, ln1_g, ln1_b, tm=1024, tn=512)
        act = _ffn_up(h, wb_gate, wb_up, tn=512)
        out = _proj_ln(act, wb_down, h, ln2_g, ln2_b, tm=1024)
        return out, new_k, new_v, s_fin

    xp = x_prompt[0]
    yp, kp, vp, sp = group(
        xp, 1, t_p,
        jnp.zeros((1, 1, 3 * RW_WIDTH), F32), jnp.zeros((1, 1, LORA_PAD), F32),
        None, lambda zq, zk, zv: _moba_prompt(zq, zk, zv, bias), 256, 8)

    xs = x_sample.reshape(nb_s * t_s, D_MODEL)
    sh = state_shift[0]
    zprev = _proj(sh, w_main, RKV_COL, 3 * RW_WIDTH)[:, None, :]
    zlprev = _proj(sh, w_lora, 0, LORA_PAD)[:, None, :]

    def attend_sample(zq, zk, zv):
        q = zq.reshape(nb_s, t_s, ATTN_HEADS, HEAD_DIM)
        qht = jnp.transpose(q, (0, 2, 1, 3)).reshape(nb_s, ATTN_HEADS * t_s, HEAD_DIM)
        kn = zk.reshape(nb_s, t_s * ATTN_HEADS, HEAD_DIM)
        vn = zv.reshape(nb_s, t_s * ATTN_HEADS, HEAD_DIM)
        o = _moba_sample(qht, kn, vn, cache_k, cache_v, page_table, bias)
        o = jnp.transpose(o.reshape(nb_s, ATTN_HEADS, t_s, HEAD_DIM), (0, 2, 1, 3))
        return o.reshape(nb_s * t_s, ATTN_WIDTH).astype(BF16)

    ys, ks, vs, ss = group(xs, nb_s, t_s, zprev, zlprev, _state_to_stacked(state_wkv[0]),
                           attend_sample, 8, 8)

    return (yp[None], ys.reshape(nb_s, t_s, D_MODEL),
            kp.reshape(1, 1, t_p, ATTN_HEADS, HEAD_DIM), vp.reshape(1, 1, t_p, ATTN_HEADS, HEAD_DIM),
            sp[None], xp[None, -1:, :],
            ks.reshape(1, nb_s, t_s, ATTN_HEADS, HEAD_DIM), vs.reshape(1, nb_s, t_s, ATTN_HEADS, HEAD_DIM),
            ss[None], x_sample[None, :, -1, :])
```

```python
import functools
import math

import numpy as np
import jax
import jax.numpy as jnp
from jax import lax
from jax.experimental import pallas as pl
from jax.experimental.pallas import tpu as pltpu

F32 = jnp.float32
BF16 = jnp.bfloat16

D_MODEL = 2048
HEAD_DIM = 128
ATTN_HEADS = D_MODEL // 256
ATTN_WIDTH = ATTN_HEADS * HEAD_DIM
MOBA_BLOCK = 256
MOBA_TOPK = 3
N_BUCKETS = 32
MAX_DISTANCE = 128
PAGE_SIZE = 128
RW_HEAD = 64
RW_HEADS = D_MODEL // 128
RW_WIDTH = RW_HEADS * RW_HEAD
DECAY_LORA = 96
AAA_LORA = 96
GATE_LORA = 256
LORA_COLS = DECAY_LORA + AAA_LORA + GATE_LORA
LORA_PAD = 512
GN_EPS = RW_HEAD * 1e-5
D_FF = 5632
LN_EPS = 1e-5
SMALL_M = 256
WIDE_TILE_BYTES = 8 * 1024 * 1024
LN_ROWS = 128
DEPTH = 1
DEEPNORM_ALPHA = (2 * DEPTH) ** 0.25

RKV_COL = 3 * ATTN_WIDTH
LORA_COL = RKV_COL + 3 * RW_WIDTH
GATE_COL = LORA_COL + LORA_COLS
MAIN_COLS = LORA_COL

NEG = -1e30
LOG2E = math.log2(math.e)
FAR_PARTS = 4
ONES_ROWS = 8
SEG_CHUNK = 256
PAGES_PER_BLOCK = MOBA_BLOCK // PAGE_SIZE
SAMPLE_BLOCKS_PER_STEP = 8
SAMPLE_LOOKAHEAD = 2
RW_CHUNK = 64
SCAN_SEQS_PER_STEP = 8
RW_GROUP = 2
CHUNKS_PER_STEP = 2
VMEM_LIMIT = 56 * 1024 * 1024


def _cparams(sem):
    return pltpu.CompilerParams(dimension_semantics=sem, vmem_limit_bytes=VMEM_LIMIT)


def _dot(a, b):
    return jnp.dot(a, b, preferred_element_type=F32)


def _dot_nt(a, b):
    return lax.dot_general(a, b, (((1,), (1,)), ((), ())), preferred_element_type=F32)


def _split(x):
    hi = x.astype(BF16)
    lo = (x - hi.astype(F32)).astype(BF16)
    return hi, lo


def _sigmoid(x):
    return 1.0 / (1.0 + jnp.exp(-x))


def _tile(m, pref):
    t = min(m, pref)
    assert m % t == 0, (m, pref)
    return t


def _wide_tn(m, tn, n, k=D_MODEL):
    if m > SMALL_M:
        return tn
    for cand in (2048, 1024, 512):
        if cand >= tn and n % cand == 0 and k * cand * 2 <= WIDE_TILE_BYTES:
            return cand
    return tn


def _proj_kernel(x_ref, w_ref, o_ref, xb_ref):
    @pl.when(pl.program_id(1) == 0)
    def _():
        xb_ref[...] = x_ref[...].astype(BF16)

    o_ref[...] = _dot(xb_ref[...], w_ref[...].astype(BF16)).astype(o_ref.dtype)


def _proj(x, w, col0, ncols, tm=1024, tn=512):
    m, k = x.shape
    tm = _tile(m, tm)
    tn = _tile(ncols, _wide_tn(m, tn, ncols))
    assert col0 % tn == 0
    c0 = col0 // tn
    return pl.pallas_call(
        _proj_kernel,
        grid=(m // tm, ncols // tn),
        in_specs=[pl.BlockSpec((tm, k), lambda i, j: (i, 0)),
                  pl.BlockSpec((k, tn), lambda i, j: (0, c0 + j))],
        out_specs=pl.BlockSpec((tm, tn), lambda i, j: (i, j)),
        out_shape=jax.ShapeDtypeStruct((m, ncols), F32),
        scratch_shapes=[pltpu.VMEM((tm, k), BF16)],
        compiler_params=_cparams(("parallel", "arbitrary")),
        name="proj",
    )(x, w)


def _proj_split_kernel(x_ref, w_ref, *refs, bounds):
    o_refs, xb_ref = refs[:-1], refs[-1]
    j = pl.program_id(1)

    @pl.when(j == 0)
    def _():
        xb_ref[...] = x_ref[...].astype(BF16)

    acc = _dot(xb_ref[...], w_ref[...].astype(BF16))
    for o_ref, (lo, hi) in zip(o_refs, bounds):
        @pl.when((j >= lo) & (j < hi))
        def _():
            o_ref[...] = acc


def _proj_split(x, w, widths, tm=1024, tn=512):
    m, k = x.shape
    tm = _tile(m, tm)
    tn = _wide_tn(m, tn, math.gcd(*widths))
    assert all(wd % tn == 0 for wd in widths)
    edges = np.cumsum([0] + [wd // tn for wd in widths])
    bounds = [(int(edges[g]), int(edges[g + 1])) for g in range(len(widths))]

    def out_map(lo, hi):
        return lambda i, j: (i, jnp.clip(j - lo, 0, hi - lo - 1))

    return pl.pallas_call(
        functools.partial(_proj_split_kernel, bounds=bounds),
        grid=(m // tm, int(edges[-1])),
        in_specs=[pl.BlockSpec((tm, k), lambda i, j: (i, 0)),
                  pl.BlockSpec((k, tn), lambda i, j: (0, j))],
        out_specs=[pl.BlockSpec((tm, tn), out_map(lo, hi)) for lo, hi in bounds],
        out_shape=[jax.ShapeDtypeStruct((m, wd), F32) for wd in widths],
        scratch_shapes=[pltpu.VMEM((tm, k), BF16)],
        compiler_params=_cparams(("parallel", "arbitrary")),
        name="proj_split",
    )(x, w)


def _merge_kernel(oa_ref, or_ref, wa_ref, wr_ref, ga_ref, gr_ref, o_ref):
    a = _dot(oa_ref[...], wa_ref[...].astype(BF16))
    r = _dot(or_ref[...], wr_ref[...].astype(BF16))
    o_ref[...] = (_sigmoid(ga_ref[...]) * a + _sigmoid(gr_ref[...]) * r).astype(o_ref.dtype)


def _merge(o_attn, o_rwkv, zg, w_up_attn, w_up_rwkv, tm=1024, tn=512):
    m = o_attn.shape[0]
    tm = _tile(m, tm)
    tn = _wide_tn(m, tn, D_MODEL)
    nj = D_MODEL // tn
    return pl.pallas_call(
        _merge_kernel,
        grid=(m // tm, nj),
        in_specs=[pl.BlockSpec((tm, ATTN_WIDTH), lambda i, j: (i, 0)),
                  pl.BlockSpec((tm, RW_WIDTH), lambda i, j: (i, 0)),
                  pl.BlockSpec((ATTN_WIDTH, tn), lambda i, j: (0, j)),
                  pl.BlockSpec((RW_WIDTH, tn), lambda i, j: (0, j)),
                  pl.BlockSpec((tm, tn), lambda i, j: (i, j)),
                  pl.BlockSpec((tm, tn), lambda i, j: (i, j + nj))],
        out_specs=pl.BlockSpec((tm, tn), lambda i, j: (i, j)),
        out_shape=jax.ShapeDtypeStruct((m, D_MODEL), BF16),
        compiler_params=_cparams(("parallel", "arbitrary")),
        name="merge",
    )(o_attn, o_rwkv, w_up_attn, w_up_rwkv, zg, zg)


def _proj_ln_kernel(m_ref, w_ref, x_ref, g_ref, b_ref, o_ref, *, tn, nj):
    j = pl.program_id(1)
    col = pl.multiple_of(j * tn, tn)
    mb = m_ref[...].astype(BF16)
    o_ref[:, pl.ds(col, tn)] = DEEPNORM_ALPHA * x_ref[...] + _dot(mb, w_ref[...].astype(BF16))

    @pl.when(j == nj - 1)
    def _():
        ln_rows = min(LN_ROWS, o_ref.shape[0])
        assert o_ref.shape[0] % ln_rows == 0

        def norm_rows(c, carry):
            rows = pl.ds(pl.multiple_of(c * ln_rows, ln_rows), ln_rows)
            y = o_ref[rows, :]
            mu = jnp.mean(y, axis=-1, keepdims=True)
            d = y - mu
            var = jnp.mean(d * d, axis=-1, keepdims=True)
            o_ref[rows, :] = d * lax.rsqrt(var + LN_EPS) * g_ref[...] + b_ref[...]
            return carry
        lax.fori_loop(0, o_ref.shape[0] // ln_rows, norm_rows, 0)


def _proj_ln(mat, w, x, g, b, tm=512, tn=256):
    m, k = mat.shape
    tm = _tile(m, tm)
    tn = _wide_tn(m, tn, D_MODEL, k)
    nj = D_MODEL // tn
    return pl.pallas_call(
        functools.partial(_proj_ln_kernel, tn=tn, nj=nj),
        grid=(m // tm, nj),
        in_specs=[pl.BlockSpec((tm, k), lambda i, j: (i, 0)),
                  pl.BlockSpec((k, tn), lambda i, j: (0, j)),
                  pl.BlockSpec((tm, tn), lambda i, j: (i, j)),
                  pl.BlockSpec((1, D_MODEL), lambda i, j: (0, 0)),
                  pl.BlockSpec((1, D_MODEL), lambda i, j: (0, 0))],
        out_specs=pl.BlockSpec((tm, D_MODEL), lambda i, j: (i, 0)),
        out_shape=jax.ShapeDtypeStruct((m, D_MODEL), F32),
        compiler_params=_cparams(("parallel", "arbitrary")),
        name="proj_ln",
    )(mat, w, x, g, b)


def _ffn_up_kernel(h_ref, wg_ref, wu_ref, o_ref, hb_ref):
    @pl.when(pl.program_id(1) == 0)
    def _():
        hb_ref[...] = h_ref[...].astype(BF16)

    hb = hb_ref[...]
    a = _dot(hb, wg_ref[...].astype(BF16))
    u = _dot(hb, wu_ref[...].astype(BF16))
    o_ref[...] = (a * _sigmoid(a) * u).astype(o_ref.dtype)


def _ffn_up(h, wg, wu, tm=1024, tn=256):
    m = h.shape[0]
    tm = _tile(m, tm)
    tn = _wide_tn(m, tn, D_FF)
    return pl.pallas_call(
        _ffn_up_kernel,
        grid=(m // tm, D_FF // tn),
        in_specs=[pl.BlockSpec((tm, D_MODEL), lambda i, j: (i, 0)),
                  pl.BlockSpec((D_MODEL, tn), lambda i, j: (0, j)),
                  pl.BlockSpec((D_MODEL, tn), lambda i, j: (0, j))],
        out_specs=pl.BlockSpec((tm, tn), lambda i, j: (i, j)),
        out_shape=jax.ShapeDtypeStruct((m, D_FF), BF16),
        scratch_shapes=[pltpu.VMEM((tm, D_MODEL), BF16)],
        compiler_params=_cparams(("parallel", "arbitrary")),
        name="ffn_up",
    )(h, wg, wu)


def _seg_sum(x, bones):
    outs = []
    for c in range(x.shape[1] // SEG_CHUNK):
        hi, lo = _split(x[:, c * SEG_CHUNK:(c + 1) * SEG_CHUNK])
        outs.append(_dot(hi, bones) + _dot(lo, bones))
    return jnp.concatenate(outs, axis=1)


def _shifted(z, prev):
    zs = pltpu.roll(z, 1, 0)
    row = lax.broadcasted_iota(jnp.int32, z.shape, 0)
    return jnp.where(row == 0, prev, zs)


def _rwkv_prep_kernel(z_ref, zl_ref, zp_ref, zlp_ref, mu_ref, mul_ref, w0_ref, a0_ref, w2_ref, a2_ref,
                      g2_ref, kkw_ref, kaw_ref, bones_ref,
                      r_o, w_o, k_o, v_o, kk_o, b_o, g_o, prev_ref, prevl_ref):
    @pl.when(pl.program_id(1) == 0)
    def _():
        prev_ref[...] = zp_ref[...]
        prevl_ref[...] = zlp_ref[...]

    z = z_ref[...]
    zl = zl_ref[...]
    tt = z.shape[0]
    zm = z + (_shifted(z, prev_ref[...]) - z) * mu_ref[...]
    zlm = zl + (_shifted(zl, prevl_ref[...]) - zl) * mul_ref[...]
    prev_ref[...] = z[tt - 1:tt, :]
    prevl_ref[...] = zl[tt - 1:tt, :]

    r = zm[:, 0:RW_WIDTH]
    k = zm[:, RW_WIDTH:2 * RW_WIDTH]
    v = zm[:, 2 * RW_WIDTH:3 * RW_WIDTH]
    xw = w0_ref[...] + _dot(jnp.tanh(zlm).astype(BF16), w2_ref[...])
    nx = -xw
    softplus = jnp.maximum(nx, 0.0) + jnp.log(1.0 + jnp.exp(-jnp.abs(nx)))
    log_decay = -jnp.exp(-softplus - 0.5)
    a = _sigmoid(a0_ref[...] + _dot(zlm.astype(BF16), a2_ref[...]))
    g = _dot(_sigmoid(zlm).astype(BF16), g2_ref[...])
    kk = k * kkw_ref[...]
    ssq = _seg_sum(kk * kk, bones_ref[...])
    kk = kk / jnp.maximum(jnp.sqrt(ssq), 1e-12)
    r_o[...] = r
    w_o[...] = log_decay
    k_o[...] = k * (1.0 + (a - 1.0) * kaw_ref[...])
    v_o[...] = v
    kk_o[...] = kk
    b_o[...] = kk * a
    g_o[...] = g


def _rwkv_prep(z1, zl, zprev, zlprev, prm, tt):
    n, t, _ = z1.shape
    tt = _tile(t, tt)
    row = lambda c: pl.BlockSpec((1, c), lambda i, j: (0, 0))
    mat = lambda r, c: pl.BlockSpec((r, c), lambda i, j: (0, 0))
    seq = lambda c: pl.BlockSpec((None, tt, c), lambda i, j: (i, j, 0))
    outs = pl.pallas_call(
        _rwkv_prep_kernel,
        grid=(n, t // tt),
        in_specs=[seq(3 * RW_WIDTH),
                  seq(LORA_PAD),
                  pl.BlockSpec((None, 1, 3 * RW_WIDTH), lambda i, j: (i, 0, 0)),
                  pl.BlockSpec((None, 1, LORA_PAD), lambda i, j: (i, 0, 0)),
                  row(3 * RW_WIDTH), row(LORA_PAD), row(RW_WIDTH), row(RW_WIDTH),
                  mat(LORA_PAD, RW_WIDTH), mat(LORA_PAD, RW_WIDTH), mat(LORA_PAD, RW_WIDTH),
                  row(RW_WIDTH), row(RW_WIDTH), mat(SEG_CHUNK, SEG_CHUNK)],
        out_specs=[seq(RW_WIDTH)] * 7,
        out_shape=[jax.ShapeDtypeStruct((n, t, RW_WIDTH), F32)] * 7,
        scratch_shapes=[pltpu.VMEM((1, 3 * RW_WIDTH), F32), pltpu.VMEM((1, LORA_PAD), F32)],
        compiler_params=_cparams(("parallel", "arbitrary")),
        name="rwkv_prep",
    )(z1, zl, zprev, zlprev, prm["mu"], prm["mul"], prm["w0"], prm["a0"], prm["w2"], prm["a2"],
      prm["g2"], prm["k_k"], prm["k_a"], prm["bones"])
    return outs


def _rwkv_scan_kernel(r_ref, w_ref, k_ref, v_ref, kk_ref, b_ref, s0_ref, bones_ref, eye_ref,
                      y_ref, sout_ref, s_ref, *, tb, nblk, nseq):
    t = pl.program_id(1)

    @pl.when(t == 0)
    def _():
        s_ref[...] = s0_ref[...]

    bones = bones_ref[...]
    eye = eye_ref[...]
    seqs = range(nseq)

    def bc(ref, q, s):
        return jnp.concatenate(
            [jnp.broadcast_to(ref[q, pl.ds(s, 1), c * SEG_CHUNK:(c + 1) * SEG_CHUNK], (RW_HEAD, SEG_CHUNK))
             for c in range(4)], axis=0)

    def step(s, carry):
        st = [s_ref[q] for q in seqs]
        split = [_split(st[q] * bc(kk_ref, q, s)) for q in seqs]
        sa = [_dot(split[q][0], bones) + _dot(split[q][1], bones) for q in seqs]
        vcol = [_dot((bc(v_ref, q, s) * eye).astype(BF16), bones) for q in seqs]
        sn = [st[q] * jnp.exp(bc(w_ref, q, s)) - sa[q] * bc(b_ref, q, s) + vcol[q] * bc(k_ref, q, s) for q in seqs]
        for q in seqs:
            s_ref[q] = sn[q]
        yb = [_dot((sn[q] * bc(r_ref, q, s)).astype(BF16), bones) for q in seqs]
        for q in seqs:
            y4 = jnp.sum((yb[q] * eye).reshape(4, RW_HEAD, SEG_CHUNK), axis=1)
            for c in range(4):
                y_ref[q, pl.ds(s, 1), c * SEG_CHUNK:(c + 1) * SEG_CHUNK] = y4[c:c + 1, :]
        return carry

    lax.fori_loop(0, tb, step, 0)

    @pl.when(t == nblk - 1)
    def _():
        sout_ref[...] = s_ref[...]


def _rwkv_scan(r, w, k, v, kk, b, s0, prm, tb=8):
    n, t, _ = r.shape
    tb = _tile(t, tb)
    nblk = t // tb
    nseq = _tile(n, SCAN_SEQS_PER_STEP)
    seq = pl.BlockSpec((nseq, tb, RW_WIDTH), lambda i, j: (i, j, 0))
    st = pl.BlockSpec((nseq, 4 * RW_HEAD, SEG_CHUNK), lambda i, j: (i, 0, 0))
    cst = pl.BlockSpec((SEG_CHUNK, SEG_CHUNK), lambda i, j: (0, 0))
    return pl.pallas_call(
        functools.partial(_rwkv_scan_kernel, tb=tb, nblk=nblk, nseq=nseq),
        grid=(n // nseq, nblk),
        in_specs=[seq] * 6 + [st, cst, cst],
        out_specs=[seq, st],
        out_shape=[jax.ShapeDtypeStruct((n, t, RW_WIDTH), F32),
                   jax.ShapeDtypeStruct((n, 4 * RW_HEAD, SEG_CHUNK), F32)],
        scratch_shapes=[pltpu.VMEM((nseq, 4 * RW_HEAD, SEG_CHUNK), F32)],
        compiler_params=_cparams(("parallel", "arbitrary")),
        name="rwkv_scan",
    )(r, w, k, v, kk, b, s0, prm["bones"], prm["eye4"])


def _rwkv_chunk_kernel(r_ref, lw_ref, k_ref, v_ref, kk_ref, b_ref, y_ref, hout_ref, h_ref, *, nchunks):
    t = pl.program_id(1)
    cs = RW_CHUNK
    n = RW_GROUP * RW_HEAD
    ngroups = RW_HEADS // RW_GROUP

    @pl.when(t == 0)
    def _():
        h_ref[...] = jnp.zeros(h_ref.shape, F32)

    row = lax.broadcasted_iota(jnp.int32, (n, n), 0)
    lane = lax.broadcasted_iota(jnp.int32, (n, n), 1)
    same_head = (row // cs) == (lane // cs)
    strict = same_head & ((lane % cs) < (row % cs))
    incl = same_head & ((lane % cs) <= (row % cs))
    eye = row == lane
    lane_head = lax.broadcasted_iota(jnp.int32, (cs, n), 1) // RW_HEAD
    row_in = lax.broadcasted_iota(jnp.int32, (cs, n), 0)

    def stack(x):
        return jnp.concatenate([jnp.where(lane_head == hl, x, 0.0) for hl in range(RW_GROUP)], axis=0)

    def tile4(x):
        return jnp.concatenate([x] * RW_GROUP, axis=1)

    def cumsum_rows(x):
        for sh in (1, 2, 4, 8, 16, 32):
            x = x + jnp.where(row_in >= sh, pltpu.roll(x, sh, 0), 0.0)
        return x

    units = [(ci, c) for ci in range(CHUNKS_PER_STEP) for c in range(ngroups)]
    pre = {}
    for ci, c in units:
        sl = slice(c * n, (c + 1) * n)
        rs = slice(ci * cs, (ci + 1) * cs)
        lw = lw_ref[rs, sl]
        l_in = cumsum_rows(lw)
        l_end = l_in[cs - 1:cs, :]
        e_neg = jnp.exp(-l_in)
        e_tail = jnp.exp(l_end - l_in)
        kq = stack(kk_ref[rs, sl] * jnp.exp(l_in - lw)).astype(BF16)
        rq = stack(r_ref[rs, sl] * jnp.exp(l_in)).astype(BF16)
        k_c = k_ref[rs, sl]
        b_c = b_ref[rs, sl]
        kh = (k_c * e_neg).astype(BF16)
        bh = (b_c * e_neg).astype(BF16)
        tail = jnp.concatenate([stack(k_c * e_tail), -stack(b_c * e_tail)], axis=0)
        pre[ci, c] = dict(
            sl=sl, rs=rs, kq=kq, rq=rq,
            a_k=jnp.where(strict, tile4(_dot_nt(kq, kh)), 0.0).astype(BF16),
            a_b=jnp.where(strict, tile4(_dot_nt(kq, bh)), 0.0),
            r_k=jnp.where(incl, tile4(_dot_nt(rq, kh)), 0.0).astype(BF16),
            r_b=jnp.where(incl, tile4(_dot_nt(rq, bh)), 0.0).astype(BF16),
            v_bd=stack(v_ref[rs, sl]).astype(BF16),
            tail_t=tail.T.astype(BF16),
            decay_col=jnp.sum(jnp.where(eye, jnp.broadcast_to(jnp.exp(l_end), (n, n)), 0.0), axis=1,
                              keepdims=True))
    pw = {u_: pre[u_]["a_b"] for u_ in units}
    inv = {u_: jnp.where(eye, 1.0, 0.0) - pre[u_]["a_b"] for u_ in units}
    for _ in range(5):
        for u_ in units:
            pb = pw[u_].astype(BF16)
            pw[u_] = _dot(pb, pb)
        for u_ in units:
            inv[u_] = inv[u_] + _dot(inv[u_].astype(BF16), pw[u_].astype(BF16))
    av = {u_: _dot(pre[u_]["a_k"], pre[u_]["v_bd"]) for u_ in units}
    invb = {u_: inv[u_].astype(BF16) for u_ in units}
    yv = {u_: _dot(pre[u_]["r_k"], pre[u_]["v_bd"]) for u_ in units}
    groups = range(ngroups)
    for ci in range(CHUNKS_PER_STEP):
        h0 = {c: h_ref[c] for c in groups}
        h0b = {c: h0[c].astype(BF16) for c in groups}
        rhs = {c: (_dot(pre[ci, c]["kq"], h0b[c]) + av[ci, c]).astype(BF16) for c in groups}
        yh = {c: _dot(pre[ci, c]["rq"], h0b[c]) for c in groups}
        ub = {c: _dot(invb[ci, c], rhs[c]).astype(BF16) for c in groups}
        for c in groups:
            p_ = pre[ci, c]
            y = yh[c] + yv[ci, c] - _dot(p_["r_b"], ub[c])
            y_ref[p_["rs"], p_["sl"]] = sum(y[hl * cs:(hl + 1) * cs] for hl in range(1, RW_GROUP)) + y[0:cs]
        for c in groups:
            p_ = pre[ci, c]
            upd = _dot(p_["tail_t"], jnp.concatenate([p_["v_bd"], ub[c]], axis=0))
            h_ref[c] = p_["decay_col"] * h0[c] + upd

    @pl.when(t == nchunks - 1)
    def _():
        hout_ref[...] = h_ref[...]


def _rwkv_chunked(r, lw, k, v, kk, b):
    n, t, _ = r.shape
    rows = RW_CHUNK * CHUNKS_PER_STEP
    assert t % rows == 0 and RW_CHUNK == RW_HEAD
    nchunks = t // rows
    seq = pl.BlockSpec((None, rows, RW_WIDTH), lambda i, j: (i, j, 0))
    gw = RW_GROUP * RW_HEAD
    ngroups = RW_HEADS // RW_GROUP
    st = pl.BlockSpec((None, ngroups, gw, gw), lambda i, j: (i, 0, 0, 0))
    y, h = pl.pallas_call(
        functools.partial(_rwkv_chunk_kernel, nchunks=nchunks),
        grid=(n, nchunks),
        in_specs=[seq] * 6,
        out_specs=[seq, st],
        out_shape=[jax.ShapeDtypeStruct((n, t, RW_WIDTH), F32),
                   jax.ShapeDtypeStruct((n, ngroups, gw, gw), F32)],
        scratch_shapes=[pltpu.VMEM((ngroups, gw, gw), F32)],
        compiler_params=_cparams(("parallel", "arbitrary")),
        name="rwkv_chunked",
    )(r, lw, k, v, kk, b)
    h = h.reshape(n, ngroups, RW_GROUP, RW_HEAD, RW_GROUP, RW_HEAD)
    hd = jnp.stack([h[:, :, hl, :, hl, :] for hl in range(RW_GROUP)], axis=2)
    return y, jnp.swapaxes(hd, 3, 4).reshape(n, RW_HEADS, RW_HEAD, RW_HEAD)


def _rwkv_post_kernel(y_ref, r_ref, k_ref, v_ref, g_ref, rk_ref, lg_ref, lb_ref, bones_ref, o_ref):
    bones = bones_ref[...]
    y = y_ref[...]
    d = y - _seg_sum(y, bones) * (1.0 / RW_HEAD)
    var = _seg_sum(d * d, bones) * (1.0 / RW_HEAD)
    yn = d * lax.rsqrt(var + GN_EPS) * lg_ref[...] + lb_ref[...]
    bonus = _seg_sum(r_ref[...] * k_ref[...] * rk_ref[...], bones) * v_ref[...]
    o_ref[...] = ((yn + bonus) * g_ref[...]).astype(o_ref.dtype)


def _rwkv_post(y, r, k, v, g, prm, tm=256):
    m = y.shape[0]
    tm = _tile(m, tm)
    big = pl.BlockSpec((tm, RW_WIDTH), lambda i: (i, 0))
    row = pl.BlockSpec((1, RW_WIDTH), lambda i: (0, 0))
    return pl.pallas_call(
        _rwkv_post_kernel,
        grid=(m // tm,),
        in_specs=[big] * 5 + [row] * 3 + [pl.BlockSpec((SEG_CHUNK, SEG_CHUNK), lambda i: (0, 0))],
        out_specs=big,
        out_shape=jax.ShapeDtypeStruct((m, RW_WIDTH), BF16),
        compiler_params=_cparams(("parallel",)),
        name="rwkv_post",
    )(y, r, k, v, g, prm["r_k"], prm["ln_g"], prm["ln_b"], prm["bones"])


def _state_to_stacked(s):
    n = s.shape[0]
    s = s.reshape(n, 4, 4, RW_HEAD, RW_HEAD)
    return jnp.transpose(s, (0, 1, 3, 2, 4)).reshape(n, 4 * RW_HEAD, SEG_CHUNK)


def _state_from_stacked(s):
    n = s.shape[0]
    s = s.reshape(n, 4, RW_HEAD, 4, RW_HEAD)
    return jnp.transpose(s, (0, 1, 3, 2, 4)).reshape(n, RW_HEADS, RW_HEAD, RW_HEAD)


def _top3_rows(sc, idx, nvalid_mask):
    big = float(sc.shape[0])
    sc = jnp.where(nvalid_mask, sc, NEG)
    sel = jnp.zeros(sc.shape, F32)
    for _ in range(MOBA_TOPK):
        mx = jnp.max(sc, axis=0, keepdims=True)
        first = jnp.min(jnp.where(sc == mx, idx, big), axis=0, keepdims=True)
        hit = (idx == first) & (mx > 0.5 * NEG)
        sel = jnp.where(hit, 1.0, sel)
        sc = jnp.where(hit, NEG, sc)
    return sel


def _moba_prompt_kernel(q_ref, k_ref, v_ref, bown_ref, bprev_ref, bfar_ref, o_ref,
                        kb_ref, vt_ref, kmh_ref, kml_ref, sel_ref, m_ref, acc_ref, s_ref, *, nb):
    i = pl.program_id(1)
    blk = MOBA_BLOCK

    @pl.when(i == 0)
    def _():
        def prep(j, c):
            rows = pl.ds(pl.multiple_of(j * blk, blk), blk)
            kj = k_ref[rows, :]
            kb_ref[rows, :] = kj.astype(BF16)
            km = jnp.sum(kj, axis=0, keepdims=True) * (1.0 / blk)
            hi, lo = _split(km)
            kmh_ref[pl.ds(j, 1), :] = hi.astype(F32)
            kml_ref[pl.ds(j, 1), :] = lo.astype(F32)
            vt_ref[j] = v_ref[rows, :].T.astype(BF16)
            return c
        lax.fori_loop(0, nb, prep, 0)

    qt = q_ref[...].T
    qh, ql = _split(qt)
    kmh = kmh_ref[...].astype(BF16)
    kml = kml_ref[...].astype(BF16)
    sc = _dot(kmh, qh) + _dot(kmh, ql) + _dot(kml, qh)
    bidx = lax.broadcasted_iota(jnp.int32, sc.shape, 0)
    sel_ref[...] = (_top3_rows(sc, bidx.astype(F32), bidx < i) - 1.0) * (-NEG)

    qs = (qt * (HEAD_DIM ** -0.5 * LOG2E)).astype(BF16)

    def keys(j, n):
        return kb_ref[pl.ds(pl.multiple_of(j * blk, blk), n * blk), :]

    def values_t(j, n):
        return vt_ref[j] if n == 1 else jnp.concatenate([vt_ref[j + g] for g in range(n)], axis=1)

    bfar = bfar_ref[0:1, 0:1]
    kidx = lax.broadcasted_iota(jnp.int32, (blk, blk), 0)
    qidx = lax.broadcasted_iota(jnp.int32, (blk, blk), 1)
    causal = kidx <= qidx

    def mask_rows(j, n):
        rows = [jnp.broadcast_to(sel_ref[pl.ds(j + g, 1), :], (blk, blk)) for g in range(n)]
        return rows[0] if n == 1 else jnp.concatenate(rows, axis=0)

    def far_scores(j, n):
        return _dot(keys(j, n), qs) + mask_rows(j, n)

    def near(ref):
        return (ref[...] - bfar) * LOG2E

    n_far = jnp.maximum(i - 1, 0)
    n_quads = n_far // 4
    rem0 = n_quads * 4

    ip = jnp.maximum(i - 1, 0)
    jr = [jnp.minimum(rem0 + r, nb - 1) for r in range(3)]
    absent = [jnp.where(rem0 + r < n_far, 0.0, NEG) for r in range(3)]
    bias_a = jnp.concatenate([near(bprev_ref) + mask_rows(ip, 1), jnp.where(causal, near(bown_ref), NEG)], axis=0)
    scores = [_dot(jnp.concatenate([keys(ip, 1), keys(i, 1)], axis=0), qs) + bias_a]
    scores += [_dot(keys(jr[r], 1), qs) + (mask_rows(jr[r], 1) + absent[r]) for r in range(3)]
    s_ref[...] = far_scores(0, 4)
    vts = [jnp.concatenate([vt_ref[ip], vt_ref[i]], axis=1)] + [vt_ref[jr[r]] for r in range(3)]
    m_loc = [jnp.max(s, axis=0, keepdims=True) for s in scores]
    prob = [jnp.exp2(s - m) for s, m in zip(scores, m_loc)]
    l_loc = [jnp.broadcast_to(jnp.sum(p, axis=0, keepdims=True), (ONES_ROWS, blk)) for p in prob]
    pv = [_dot(vt, p.astype(BF16)) for vt, p in zip(vts, prob)]
    m_new = jnp.maximum(jnp.maximum(m_loc[0], m_loc[1]), jnp.maximum(m_loc[2], m_loc[3]))
    acc = None
    for m, o, l in zip(m_loc, pv, l_loc):
        term = jnp.exp2(m - m_new) * jnp.concatenate([o, l], axis=0)
        acc = term if acc is None else acc + term
    acc_ref[...] = acc
    m_ref[...] = m_new


    def far_quad(jq, c):
        s_next = far_scores(4 * jnp.minimum(jq + 1, n_quads - 1), 4)
        per = 4 // FAR_PARTS
        hks = range(FAR_PARTS)
        sp = [s_ref[hk * per * blk:(hk + 1) * per * blk, :] for hk in hks]
        m_loc = [jnp.max(s, axis=0, keepdims=True) for s in sp]
        prob = [jnp.exp2(s - m) for s, m in zip(sp, m_loc)]
        l_loc = [jnp.broadcast_to(jnp.sum(p, axis=0, keepdims=True), (ONES_ROWS, blk)) for p in prob]
        pv = [_dot(values_t(4 * jq + per * hk, per), prob[hk].astype(BF16)) for hk in hks]
        m_old = m_ref[...]
        m_new = m_old
        for m in m_loc:
            m_new = jnp.maximum(m_new, m)
        acc = jnp.exp2(m_old - m_new) * acc_ref[...]
        for m, o, l in zip(m_loc, pv, l_loc):
            acc = acc + jnp.exp2(m - m_new) * jnp.concatenate([o, l], axis=0)
        acc_ref[...] = acc
        m_ref[...] = m_new
        s_ref[...] = s_next
        return c
    lax.fori_loop(0, n_quads, far_quad, 0)

    acc = acc_ref[...]
    o_ref[...] = (acc[0:HEAD_DIM] / acc[HEAD_DIM:HEAD_DIM + 1]).T.astype(o_ref.dtype)


def _moba_prompt(q, k, v, bias):
    t = q.shape[0]
    assert t % MOBA_BLOCK == 0
    nb = t // MOBA_BLOCK
    assert nb >= 4
    blk = MOBA_BLOCK
    h8 = ATTN_HEADS
    tile = pl.BlockSpec((None, blk, blk), lambda h, i: (h, 0, 0))
    return pl.pallas_call(
        functools.partial(_moba_prompt_kernel, nb=nb),
        grid=(h8, nb),
        in_specs=[pl.BlockSpec((blk, HEAD_DIM), lambda h, i: (i, h)),
                  pl.BlockSpec((t, HEAD_DIM), lambda h, i: (0, h)),
                  pl.BlockSpec((t, HEAD_DIM), lambda h, i: (0, h)),
                  tile, tile,
                  pl.BlockSpec((None, 1, HEAD_DIM), lambda h, i: (h, 0, 0))],
        out_specs=pl.BlockSpec((blk, HEAD_DIM), lambda h, i: (i, h)),
        out_shape=jax.ShapeDtypeStruct((t, ATTN_WIDTH), BF16),
        scratch_shapes=[pltpu.VMEM((t, HEAD_DIM), BF16),
                        pltpu.VMEM((nb, HEAD_DIM, blk), BF16),
                        pltpu.VMEM((nb, HEAD_DIM), F32),
                        pltpu.VMEM((nb, HEAD_DIM), F32),
                        pltpu.VMEM((nb, blk), F32),
                        pltpu.VMEM((1, blk), F32),
                        pltpu.VMEM((HEAD_DIM + ONES_ROWS, blk), F32),
                        pltpu.VMEM((4 * blk, blk), F32)],
        compiler_params=_cparams(("arbitrary", "arbitrary")),
        name="moba_prompt",
    )(q, k, v, bias["own_t"], bias["prev_t"], bias["far"])


def _diag_extract(s):
    lane = lax.broadcasted_iota(jnp.int32, (ATTN_HEADS, s.shape[1]), 1) % ATTN_HEADS
    out = jnp.zeros((ATTN_HEADS, s.shape[1]), F32)
    for hp in range(ATTN_HEADS):
        out = out + jnp.where(lane == hp, s[hp * 8:(hp + 1) * 8, :], 0.0)
    return out


def _diag_expand(p):
    lane = lax.broadcasted_iota(jnp.int32, p.shape, 1) % ATTN_HEADS
    return jnp.concatenate([jnp.where(lane == hp, p, 0.0) for hp in range(ATTN_HEADS)], axis=0)


def _class_allreduce(x, op):
    for sh in (8, 16, 32, 64):
        x = op(x, pltpu.roll(x, sh, 1))
    return x


def _class_allreduce_many(xs, op):
    for sh in (8, 16, 32, 64):
        xs = [op(x, pltpu.roll(x, sh, 1)) for x in xs]
    return xs


def _fold_tiles(x, op):
    out = x[:, 0:128]
    for c in range(1, x.shape[1] // 128):
        out = op(out, x[:, c * 128:(c + 1) * 128])
    return out


def _moba_sample_kernel(pt_ref, q_ref, ck_ref, cv_ref, kn_ref, vn_ref, blast_ref, bown_ref, bfar_ref, o_ref,
                        sc_ref, bsum_ref, bmax_ref, bexp_ref, selx_ref, m_ref, li_ref, acc_ref,
                        page_ref, sem_ref, *, nblk, grp, nbatch):
    ph = pl.program_id(1)
    step = pl.program_id(2)
    nsteps = nblk // grp
    npages = 2 * grp
    ring = SAMPLE_LOOKAHEAD + 1
    total = nbatch * 2 * nsteps
    g_lin = (pl.program_id(0) * 2 + ph) * nsteps + step

    def page_copy(src_ref, page, slot):
        return pltpu.make_async_copy(src_ref.at[0, page], page_ref.at[slot], sem_ref.at[slot])

    def request(gl):
        b2 = gl // (2 * nsteps)
        r2 = gl % (2 * nsteps)
        ph2 = r2 // nsteps
        p0 = (r2 % nsteps) * npages
        base = (gl % ring) * npages
        for src_ref, which in ((ck_ref, 0), (cv_ref, 1)):
            @pl.when(ph2 == which)
            def _():
                for u in range(npages):
                    page_copy(src_ref, pt_ref[b2, p0 + u], base + u).start()

    @pl.when(g_lin == 0)
    def _():
        for gl in range(min(SAMPLE_LOOKAHEAD, total)):
            request(gl)

    @pl.when(g_lin + SAMPLE_LOOKAHEAD < total)
    def _():
        request(g_lin + SAMPLE_LOOKAHEAD)

    slot0 = (g_lin % ring) * npages
    for u in range(npages):
        page_copy(ck_ref, 0, slot0 + u).wait()
    k_refs = v_refs = [page_ref.at[slot0 + u] for u in range(npages)]
    ntok = 8
    rows = PAGE_SIZE * ATTN_HEADS
    nkeys = 2 * rows
    far16 = jnp.concatenate([bfar_ref[...]] * (nkeys // 128), axis=1)
    lane_blk = lax.broadcasted_iota(jnp.int32, (ntok, 128), 1) // 8

    def block_bias(j):
        return jnp.where(j == nblk - 1, blast_ref[...], far16)

    def tile16(x):
        return jnp.concatenate([x] * (nkeys // 128), axis=1)

    def compact_tile(j):
        return pl.ds(pl.multiple_of((j // 16) * 128, 128), 128)

    def put_compact(ref, j, x):
        ref[:, compact_tile(j)] = jnp.where(lane_blk == j % 16, x, ref[:, compact_tile(j)])

    @pl.when((ph == 0) & (step == 0))
    def _():
        bmax_ref[...] = jnp.zeros(bmax_ref.shape, F32)
        bexp_ref[...] = jnp.zeros(bexp_ref.shape, F32)

    @pl.when(ph == 0)
    def _():
        qs = (q_ref[...] * (HEAD_DIM ** -0.5)).astype(BF16)
        gs = range(grp)
        js = [step * grp + g for g in gs]
        raw = []
        for g in gs:
            k0 = k_refs[2 * g][...]
            k1 = k_refs[2 * g + 1][...]
            bsum_ref[pl.ds(pl.multiple_of(js[g] * 8, 8), 8), :] = jnp.sum(k0, axis=0) + jnp.sum(k1, axis=0)
            k2 = jnp.concatenate([k0.reshape(rows, HEAD_DIM), k1.reshape(rows, HEAD_DIM)], axis=0).astype(BF16)
            raw.append(_dot_nt(qs, k2))
        sc = [_diag_extract(raw[g]) + block_bias(js[g]) for g in gs]
        for g in gs:
            sc_ref[js[g]] = sc[g]
        bm = _class_allreduce_many([_fold_tiles(sc[g], jnp.maximum) for g in gs], jnp.maximum)
        be = _class_allreduce_many([_fold_tiles(jnp.exp(sc[g] - tile16(bm[g])), jnp.add) for g in gs], jnp.add)
        for g in gs:
            put_compact(bmax_ref, js[g], bm[g])
            put_compact(bexp_ref, js[g], be[g])

    @pl.when((ph == 0) & (step == nsteps - 1))
    def _():
        q = q_ref[...]
        qh, ql = _split(q)
        bh, bl = _split(bsum_ref[...] * (1.0 / MOBA_BLOCK))
        scx = _diag_extract(_dot_nt(qh, bh) + _dot_nt(qh, bl) + _dot_nt(ql, bh))
        width = nblk * 8
        jidx = (lax.broadcasted_iota(jnp.int32, (ntok, width), 1) // 8).astype(F32)

        def creduce(x, op):
            y = _class_allreduce(_fold_tiles(x, op), op)
            return jnp.concatenate([y] * (width // 128), axis=1)

        selx = jnp.zeros((ntok, width), F32)
        for _ in range(MOBA_TOPK):
            mx = creduce(scx, jnp.maximum)
            first = creduce(jnp.where(scx == mx, jidx, float(nblk)), jnp.minimum)
            hit = jidx == first
            selx = jnp.where(hit, 1.0, selx)
            scx = jnp.where(hit, NEG, scx)
        selx_ref[...] = selx
        picked = selx > 0.5

        qs = (q * (HEAD_DIM ** -0.5)).astype(BF16)
        so = _diag_extract(_dot_nt(qs, kn_ref[...].astype(BF16))) + bown_ref[...]
        tq = lax.broadcasted_iota(jnp.int32, so.shape, 0)
        tk = lax.broadcasted_iota(jnp.int32, so.shape, 1) // 8
        so = jnp.concatenate([jnp.where(tk <= tq, so, NEG), jnp.full((ntok, 64), NEG, F32)], axis=1)

        bmax = bmax_ref[...]
        m = jnp.maximum(_class_allreduce(so, jnp.maximum),
                        creduce(jnp.where(picked, bmax, NEG), jnp.maximum)[:, 0:128])
        mw = jnp.concatenate([m] * (width // 128), axis=1)
        l_blocks = creduce(jnp.where(picked, bexp_ref[...] * jnp.exp(bmax - mw), 0.0), jnp.add)[:, 0:128]
        po = jnp.exp(so - m)
        li = 1.0 / (_class_allreduce(po, jnp.add) + l_blocks)
        m_ref[...] = m
        li_ref[...] = li
        pfull = _diag_expand((po * li)[:, 0:64]).astype(BF16)
        acc_ref[...] = _dot(pfull, vn_ref[...].astype(BF16))

    @pl.when(ph == 1)
    def _():
        m16 = tile16(m_ref[...])
        li16 = tile16(li_ref[...])
        gs = range(grp)
        js = [step * grp + g for g in gs]
        mine = [jnp.where(lane_blk == js[g] % 16, selx_ref[:, compact_tile(js[g])], 0.0) for g in gs]
        picked = _class_allreduce_many(mine, jnp.add)
        p = [jnp.where(tile16(picked[g]) > 0.5, jnp.exp(sc_ref[js[g]] - m16) * li16, 0.0) for g in gs]
        pv = []
        for g in gs:
            v2 = jnp.concatenate([v_refs[2 * g][...].reshape(rows, HEAD_DIM),
                                  v_refs[2 * g + 1][...].reshape(rows, HEAD_DIM)], axis=0).astype(BF16)
            pv.append(_dot(_diag_expand(p[g]).astype(BF16), v2))
        acc = acc_ref[...]
        for g in gs:
            acc = acc + pv[g]
        acc_ref[...] = acc

    @pl.when((ph == 1) & (step == nsteps - 1))
    def _():
        o_ref[...] = acc_ref[...]


def _moba_sample(qht, knew, vnew, cache_k, cache_v, page_table, bias):
    nb_, n_pages = page_table.shape
    assert n_pages % PAGES_PER_BLOCK == 0 and PAGES_PER_BLOCK == 2
    nblk = n_pages // 2
    assert nblk >= MOBA_TOPK and (nblk * 8) % 128 == 0
    nkeys = 2 * PAGE_SIZE * ATTN_HEADS
    grp = SAMPLE_BLOCKS_PER_STEP
    assert nblk % grp == 0
    nsteps = nblk // grp

    per_b = pl.BlockSpec((None, 64, HEAD_DIM), lambda b, p, j, pt: (b, 0, 0))
    cst = lambda r, c: pl.BlockSpec((r, c), lambda b, p, j, pt: (0, 0))
    hbm = pl.BlockSpec(memory_space=pl.ANY)
    nslots = (SAMPLE_LOOKAHEAD + 1) * 2 * grp
    grid_spec = pltpu.PrefetchScalarGridSpec(
        num_scalar_prefetch=1,
        grid=(nb_, 2, nsteps),
        in_specs=[per_b, hbm, hbm, per_b, per_b, cst(8, nkeys), cst(8, 64), cst(1, 128)],
        out_specs=per_b,
        scratch_shapes=[pltpu.VMEM((nblk, 8, nkeys), F32),
                        pltpu.VMEM((nblk * 8, HEAD_DIM), F32),
                        pltpu.VMEM((8, nblk * 8), F32),
                        pltpu.VMEM((8, nblk * 8), F32),
                        pltpu.VMEM((8, nblk * 8), F32),
                        pltpu.VMEM((8, 128), F32),
                        pltpu.VMEM((8, 128), F32),
                        pltpu.VMEM((64, HEAD_DIM), F32),
                        pltpu.VMEM((nslots, PAGE_SIZE, ATTN_HEADS, HEAD_DIM), F32),
                        pltpu.SemaphoreType.DMA((nslots,))])
    return pl.pallas_call(
        functools.partial(_moba_sample_kernel, nblk=nblk, grp=grp, nbatch=nb_),
        grid_spec=grid_spec,
        out_shape=jax.ShapeDtypeStruct((nb_, 64, HEAD_DIM), F32),
        compiler_params=_cparams(("arbitrary", "arbitrary", "arbitrary")),
        name="moba_sample",
    )(page_table, qht, cache_k, cache_v, knew, vnew, bias["last_s"], bias["own_s"], bias["far_s"])


def _bias_of_distance(dist, rel_bias):
    dist = jnp.maximum(dist, 0)
    exact = N_BUCKETS // 2
    log_ratio = jnp.log(jnp.maximum(dist, 1).astype(F32) / exact) / math.log(MAX_DISTANCE / exact)
    large = jnp.minimum(exact + (log_ratio * (N_BUCKETS - exact)).astype(jnp.int32), N_BUCKETS - 1)
    return rel_bias[jnp.where(dist < exact, dist, large)]


def _bias_tiles(rel_bias, past_len, dec_seq):
    blk = MOBA_BLOCK
    val = _bias_of_distance(jnp.arange(2 * blk), rel_bias)
    val_t = val.T

    def toeplitz(ext):
        n2 = ext.shape[1]
        n = n2 // 2
        return jnp.tile(ext, (1, n))[:, :n * (n2 - 1)].reshape(ext.shape[0], n, n2 - 1)[:, :, :n]

    own_t = toeplitz(val_t)
    prev_t = toeplitz(jnp.roll(val_t, -blk, axis=1))
    far_row = val[2 * blk - 1]
    far = jnp.broadcast_to(far_row[:, None, None], (ATTN_HEADS, 1, HEAD_DIM))
    assert past_len % blk == 0 and 2 * blk - 1 >= MAX_DISTANCE and dec_seq < blk
    last = jnp.stack([val[t + 1:t + 1 + blk][::-1] for t in range(dec_seq)])
    last_s = last.reshape(dec_seq, blk * ATTN_HEADS)
    own = jnp.stack([jnp.concatenate([val[:t + 1][::-1], jnp.zeros((dec_seq - 1 - t, ATTN_HEADS), F32)])
                     for t in range(dec_seq)])
    own_s = own.reshape(dec_seq, dec_seq * ATTN_HEADS)
    far_s = jnp.tile(far_row, 128 // ATTN_HEADS)[None, :]
    return dict(own_t=own_t, prev_t=prev_t, far=far, last_s=last_s, own_s=own_s, far_s=far_s)


def _np_consts():
    seg = np.arange(SEG_CHUNK) // RW_HEAD
    bones = (seg[:, None] == seg[None, :]).astype(np.float32)
    idx = np.arange(SEG_CHUNK) % RW_HEAD
    eye4 = (idx[:, None] == idx[None, :]).astype(np.float32)
    return bones, eye4


def _pad_rows(w, row0, total):
    return jnp.zeros((total, w.shape[1]), BF16).at[row0:row0 + w.shape[0]].set(w.astype(BF16))


def kernel(x_prompt, x_sample, cache_k, cache_v, page_table, state_wkv, state_shift, w_in, rel_bias, rw_mu, rw_w0, rw_w2, rw_a0, rw_a2, rw_g2, rw_k_k, rw_k_a, rw_r_k, rw_ln_g, rw_ln_b, w_up_attn, w_up_rwkv, w_o, ln1_g, ln1_b, w_ffn_gate, w_ffn_up, w_ffn_down, ln2_g, ln2_b):
    assert x_prompt.shape[0] == 1 and w_in.shape[0] == DEPTH == 1
    t_p = x_prompt.shape[1]
    nb_s, t_s, _ = x_sample.shape
    assert t_s == 8
    past_len = page_table.shape[1] * PAGE_SIZE

    w_main = w_in[0].astype(BF16)
    w_lora = jnp.pad(w_main[:, LORA_COL:GATE_COL], ((0, 0), (0, LORA_PAD - LORA_COLS)))
    w_gate = w_main[:, GATE_COL:]
    wb_up_attn, wb_up_rwkv, wb_o = w_up_attn[0].astype(BF16), w_up_rwkv[0].astype(BF16), w_o[0].astype(BF16)
    wb_gate, wb_up, wb_down = w_ffn_gate[0].astype(BF16), w_ffn_up[0].astype(BF16), w_ffn_down[0].astype(BF16)
    bones_np, eye_np = _np_consts()
    mu = rw_mu[0]
    prm = dict(
        mu=mu[None, :3 * RW_WIDTH],
        mul=jnp.pad(mu[3 * RW_WIDTH:], (0, LORA_PAD - LORA_COLS))[None, :],
        w0=rw_w0, a0=rw_a0,
        w2=_pad_rows(rw_w2[0], 0, LORA_PAD),
        a2=_pad_rows(rw_a2[0], DECAY_LORA, LORA_PAD),
        g2=_pad_rows(rw_g2[0], DECAY_LORA + AAA_LORA, LORA_PAD),
        k_k=rw_k_k, k_a=rw_k_a, r_k=rw_r_k.reshape(1, RW_WIDTH), ln_g=rw_ln_g, ln_b=rw_ln_b,
        bones=jnp.asarray(bones_np, BF16), eye4=jnp.asarray(eye_np, F32))
    bias = _bias_tiles(rel_bias, past_len, t_s)

    def group(x2d, nseq, tseq, zprev, zlprev, s0, attend, tt, tb):
        zq, new_k, new_v, zrkv = _proj_split(x2d, w_main, (ATTN_WIDTH, ATTN_WIDTH, ATTN_WIDTH, 3 * RW_WIDTH))
        zl = _proj(x2d, w_lora, 0, LORA_PAD)
        zg = _proj(x2d, w_gate, 0, 2 * D_MODEL, tn=1024)
        o_attn = attend(zq, new_k, new_v)
        r, w, k, v, kk, b, g = _rwkv_prep(zrkv.reshape(nseq, tseq, 3 * RW_WIDTH),
                                          zl.reshape(nseq, tseq, LORA_PAD), zprev, zlprev, prm, tt)
        if s0 is None:
            y, s_fin = _rwkv_chunked(r, w, k, v, kk, b)
        else:
            y, s_fin = _rwkv_scan(r, w, k, v, kk, b, s0, prm, tb)
            s_fin = _state_from_stacked(s_fin)
        flat = lambda u: u.reshape(nseq * tseq, RW_WIDTH)
        o_rwkv = _rwkv_post(flat(y), flat(r), flat(k), flat(v), flat(g), prm)
        mixed = _merge(o_attn, o_rwkv, zg, wb_up_attn, wb_up_rwkv, tn=1024)
        h = _proj_ln(mixed, wb_o, x2d, ln1_g, ln1_b, tm=1024, tn=512)
        act = _ffn_up(h, wb_gate, wb_up, tn=512)
        out = _proj_ln(act, wb_down, h, ln2_g, ln2_b, tm=1024)
        return out, new_k, new_v, s_fin

    xp = x_prompt[0]
    yp, kp, vp, sp = group(
        xp, 1, t_p,
        jnp.zeros((1, 1, 3 * RW_WIDTH), F32), jnp.zeros((1, 1, LORA_PAD), F32),
        None, lambda zq, zk, zv: _moba_prompt(zq, zk, zv, bias), 256, 8)

    xs = x_sample.reshape(nb_s * t_s, D_MODEL)
    sh = state_shift[0]
    zprev = _proj(sh, w_main, RKV_COL, 3 * RW_WIDTH)[:, None, :]
    zlprev = _proj(sh, w_lora, 0, LORA_PAD)[:, None, :]

    def attend_sample(zq, zk, zv):
        q = zq.reshape(nb_s, t_s, ATTN_HEADS, HEAD_DIM)
        qht = jnp.transpose(q, (0, 2, 1, 3)).reshape(nb_s, ATTN_HEADS * t_s, HEAD_DIM)
        kn = zk.reshape(nb_s, t_s * ATTN_HEADS, HEAD_DIM)
        vn = zv.reshape(nb_s, t_s * ATTN_HEADS, HEAD_DIM)
        o = _moba_sample(qht, kn, vn, cache_k, cache_v, page_table, bias)
        o = jnp.transpose(o.reshape(nb_s, ATTN_HEADS, t_s, HEAD_DIM), (0, 2, 1, 3))
        return o.reshape(nb_s * t_s, ATTN_WIDTH).astype(BF16)

    ys, ks, vs, ss = group(xs, nb_s, t_s, zprev, zlprev, _state_to_stacked(state_wkv[0]),
                           attend_sample, 8, 8)

    return (yp[None], ys.reshape(nb_s, t_s, D_MODEL),
            kp.reshape(1, 1, t_p, ATTN_HEADS, HEAD_DIM), vp.reshape(1, 1, t_p, ATTN_HEADS, HEAD_DIM),
            sp[None], xp[None, -1:, :],
            ks.reshape(1, nb_s, t_s, ATTN_HEADS, HEAD_DIM), vs.reshape(1, nb_s, t_s, ATTN_HEADS, HEAD_DIM),
            ss[None], x_sample[None, :, -1, :])
```
